```python
import math
import jax
import jax.numpy as jnp
from jax import lax
import numpy as np

D_MODEL = 2048
BATCH = 8
SEQ = 2048
DEPTH = 2

CTX_LEN = 256
GRID_W = 64
N_EVEN = (DEPTH + 1) // 2
N_ODD = DEPTH // 2
EPS = 1e-6

A_WIDTH = D_MODEL
A_CHUNK = 128
A_GROUP_CH = 128
A_GROUPS = A_WIDTH // A_GROUP_CH

SSD_WIDTH = D_MODEL
SSD_HEADDIM = 64
SSD_HEADS = SSD_WIDTH // SSD_HEADDIM
SSD_GROUPS = 8
SSD_STATE = 128
SSD_CONV = 3
SSD_CHUNK = 128
SSD_CONV_CH = SSD_WIDTH + 2 * SSD_GROUPS * SSD_STATE
EVEN_IN = 2 * A_WIDTH + SSD_WIDTH + SSD_CONV_CH + 2 * SSD_HEADS
EVEN_SPLITS = (A_WIDTH, 2 * A_WIDTH, 2 * A_WIDTH + SSD_WIDTH,
               2 * A_WIDTH + SSD_WIDTH + SSD_CONV_CH)
EVEN_OUT = A_WIDTH + SSD_WIDTH

ATT_HEADDIM = 64
ATT_HEADS = D_MODEL // ATT_HEADDIM
ATT_KV_HEADS = 4
ATT_Q_PER_KV = ATT_HEADS // ATT_KV_HEADS
ATT_WINDOW = 128
ATT_BLOCK = 128
ATT_SCALE = ATT_HEADDIM ** -0.5
ROPE_BASE = 10000.0

N_EXPERTS = 64
N_EXPERT_GROUPS = 8
TOP_GROUPS = 4
TOP_K = 8
EXPERT_HIDDEN = D_MODEL // 4
SHARED_HIDDEN = D_MODEL // 4
ROUTED_SCALE = 2.5
MOE_BLOCK = 256

kernel_name = "hybrid_gmlp_ssd_swa_moe_dit"


def rms_norm(x, g):
    xf = x.astype(jnp.float32)
    y = xf * lax.rsqrt(jnp.mean(xf * xf, axis=-1, keepdims=True) + EPS)
    return (y * g.astype(jnp.float32)).astype(x.dtype)


def layer_norm(x, g, b):
    xf = x.astype(jnp.float32)
    mu = jnp.mean(xf, axis=-1, keepdims=True)
    xc = xf - mu
    var = jnp.mean(xc * xc, axis=-1, keepdims=True)
    return (xc * lax.rsqrt(var + EPS) * g.astype(jnp.float32) + b.astype(jnp.float32)).astype(x.dtype)


def modulate(x, g, shift, scale):
    return rms_norm(x, g) * (1 + scale) + shift


def axial_rope_angles(n_tokens):
    n_rows = n_tokens // GRID_W
    rows = jnp.repeat(jnp.arange(n_rows, dtype=jnp.float32), GRID_W)
    cols = jnp.tile(jnp.arange(GRID_W, dtype=jnp.float32), n_rows)
    n_freq = ATT_HEADDIM // 4
    inv_freq = ROPE_BASE ** (-jnp.arange(n_freq, dtype=jnp.float32) / n_freq)
    return jnp.stack([rows[:, None] * inv_freq, cols[:, None] * inv_freq], axis=1)


def apply_axial_rope(x, ang):
    b, s, h, d = x.shape
    xr = x.astype(jnp.float32).reshape(b, s, h, 2, 2, d // 4)
    cos = jnp.cos(ang)[None, :, None]
    sin = jnp.sin(ang)[None, :, None]
    x1, x2 = xr[..., 0, :], xr[..., 1, :]
    out = jnp.stack([x1 * cos - x2 * sin, x1 * sin + x2 * cos], axis=-2)
    return out.reshape(b, s, h, d).astype(x.dtype)


def centred_depthwise_conv(x, w, b):
    k = w.shape[0]
    y = lax.conv_general_dilated(x, w[:, None, :].astype(x.dtype), window_strides=(1,),
                                 padding=[(k // 2, k // 2)],
                                 dimension_numbers=('NWC', 'WIO', 'NWC'),
                                 feature_group_count=x.shape[-1])
    return y + b.astype(x.dtype)


def chunk_gmlp(u, v, w_s, b_s, ln_g, ln_b):
    bsz, L, _ = v.shape
    u = jax.nn.gelu(u, approximate=False)
    v = layer_norm(jax.nn.gelu(v, approximate=False), ln_g, ln_b)
    vc = v.reshape(bsz, L // A_CHUNK, A_CHUNK, A_GROUPS, A_GROUP_CH)
    mixed = jnp.einsum('gpq,bnqgc->bnpgc', w_s.astype(v.dtype), vc) \
        + b_s.T.astype(v.dtype)[None, None, :, :, None]
    return u * mixed.reshape(bsz, L, A_WIDTH)


def ssd_chunk_scan(xh, dt, a_decay, bm, cm, h0):
    f32 = jnp.float32
    bsz, L, H, P = xh.shape
    G, N = bm.shape[2], bm.shape[3]
    R = H // G
    Q = SSD_CHUNK
    nc = L // Q
    x = xh.astype(f32).reshape(bsz, nc, Q, G, R, P)
    dtc = dt.astype(f32).reshape(bsz, nc, Q, G, R)
    Bc = bm.astype(f32).reshape(bsz, nc, Q, G, N)
    Cc = cm.astype(f32).reshape(bsz, nc, Q, G, N)
    a_cum = jnp.cumsum(dtc * a_decay.reshape(G, R), axis=2)
    xdt = x * dtc[..., None]
    seg = a_cum[:, :, :, None] - a_cum[:, :, None, :]
    idx = jnp.arange(Q)
    lower = (idx[:, None] >= idx[None, :])[None, None, :, :, None, None]
    decay = jnp.exp(jnp.where(lower, seg, -jnp.inf))
    cb = jnp.einsum('bcign,bcjgn->bcijg', Cc, Bc)
    y_intra = jnp.einsum('bcijg,bcijgr,bcjgrp->bcigrp', cb, decay, xdt)
    a_last = a_cum[:, :, -1:]
    states = jnp.einsum('bcjgn,bcjgr,bcjgrp->bcgrpn', Bc, jnp.exp(a_last - a_cum), xdt)
    chunk_decay = jnp.exp(a_last[:, :, 0])

    def step(h, inp):
        s_c, d_c = inp
        return h * d_c[..., None, None] + s_c, h

    h_final, h_in = lax.scan(step, h0.astype(f32).reshape(bsz, G, R, P, N),
                             (jnp.moveaxis(states, 1, 0), jnp.moveaxis(chunk_decay, 1, 0)))
    h_in = jnp.moveaxis(h_in, 0, 1)
    y_inter = jnp.einsum('bcign,bcigr,bcgrpn->bcigrp', Cc, jnp.exp(a_cum), h_in)
    y = (y_intra + y_inter).reshape(bsz, L, H, P)
    return y, h_final.reshape(bsz, H, P, N)


def ssd_bidirectional(z, xbc, dt_raw, conv_w, conv_b, a_log, dt_bias, d_skip, norm_g, h0_fwd, h0_bwd):
    f32 = jnp.float32
    bsz, L, _ = z.shape
    xbc = jax.nn.silu(centred_depthwise_conv(xbc, conv_w, conv_b))
    xs, bmat, cmat = jnp.split(xbc, [SSD_WIDTH, SSD_WIDTH + SSD_GROUPS * SSD_STATE], axis=-1)
    xh = xs.reshape(bsz, L, SSD_HEADS, SSD_HEADDIM)
    bmat = bmat.reshape(bsz, L, SSD_GROUPS, SSD_STATE)
    cmat = cmat.reshape(bsz, L, SSD_GROUPS, SSD_STATE)
    dt = jax.nn.softplus(dt_raw.astype(f32).reshape(bsz, L, 2, SSD_HEADS) + dt_bias.astype(f32))
    a = -jnp.exp(a_log.astype(f32))
    y_f, h_f = ssd_chunk_scan(xh, dt[:, :, 0], a[0], bmat, cmat, h0_fwd)
    rev = lambda t: jnp.flip(t, axis=1)
    y_b, h_b = ssd_chunk_scan(rev(xh), rev(dt[:, :, 1]), a[1], rev(bmat), rev(cmat), h0_bwd)
    y = y_f + rev(y_b) + d_skip.astype(f32)[:, None] * xh.astype(f32)
    y = y.reshape(bsz, L, SSD_WIDTH) * jax.nn.silu(z.astype(f32))
    yg = y.reshape(bsz, L, SSD_GROUPS, SSD_WIDTH // SSD_GROUPS)
    yg = yg * lax.rsqrt(jnp.mean(yg * yg, axis=-1, keepdims=True) + EPS)
    out = yg.reshape(bsz, L, SSD_WIDTH) * norm_g.astype(f32)
    return out.astype(z.dtype), h_f, h_b


def gmlp_ssd_mix(h_ctx, h_lat, w_in, gm_ws, gm_bs, gm_ln_g, gm_ln_b, conv_w, conv_b,
                 a_log, dt_bias, d_skip, ssd_norm_g, w_out, compute_ctx):
    def split_proj(h):
        return jnp.split(h @ w_in, EVEN_SPLITS, axis=-1)

    u_c, v_c, z_c, xbc_c, dt_c = split_proj(h_ctx)
    u_l, v_l, z_l, xbc_l, dt_l = split_proj(h_lat)
    bsz = h_lat.shape[0]
    zero = jnp.zeros((bsz, SSD_HEADS, SSD_HEADDIM, SSD_STATE), jnp.float32)
    ssd_params = (conv_w, conv_b, a_log, dt_bias, d_skip, ssd_norm_g)
    y_c, hf_c, hb_c = ssd_bidirectional(z_c, xbc_c, dt_c, *ssd_params, zero, zero)
    y_l, _, _ = ssd_bidirectional(z_l, xbc_l, dt_l, *ssd_params, hf_c, hb_c)
    g_l = chunk_gmlp(u_l, v_l, gm_ws, gm_bs, gm_ln_g, gm_ln_b)
    out_lat = jnp.concatenate([g_l, y_l], axis=-1) @ w_out
    out_ctx = None
    if compute_ctx:
        g_c = chunk_gmlp(u_c, v_c, gm_ws, gm_bs, gm_ln_g, gm_ln_b)
        out_ctx = jnp.concatenate([g_c, y_c], axis=-1) @ w_out
    return out_ctx, out_lat


def softmax_with_sink(logits, sink):
    m = jnp.maximum(jnp.max(logits, axis=-1, keepdims=True), sink)
    p = jnp.exp(logits - m)
    return p / (jnp.sum(p, axis=-1, keepdims=True) + jnp.exp(sink - m))


def banded_window_attention(q, k, v, k_ctx, v_ctx, sink):
    f32 = jnp.float32
    bsz, S, H, Dh = q.shape
    nb = S // ATT_BLOCK
    band = ATT_BLOCK + 2 * ATT_WINDOW
    n_ctx = k_ctx.shape[1]
    pad = ((0, 0), (ATT_WINDOW, ATT_WINDOW), (0, 0), (0, 0))
    k_pad = jnp.pad(k, pad)
    v_pad = jnp.pad(v, pad)
    kc = k_ctx.astype(f32)
    vc = v_ctx.astype(f32)
    qb = jnp.moveaxis(q.reshape(bsz, nb, ATT_BLOCK, ATT_KV_HEADS, ATT_Q_PER_KV, Dh), 1, 0)
    qi = jnp.arange(ATT_BLOCK)[:, None]
    kj = jnp.arange(band)[None, :]
    in_window = jnp.abs(kj - ATT_WINDOW - qi) <= ATT_WINDOW
    sink_b = sink.astype(f32).reshape(ATT_KV_HEADS, ATT_Q_PER_KV)[None, :, :, None, None]

    def block_fn(args):
        qblk, bi = args
        start = bi * ATT_BLOCK
        kb = lax.dynamic_slice_in_dim(k_pad, start, band, axis=1).astype(f32)
        vb = lax.dynamic_slice_in_dim(v_pad, start, band, axis=1).astype(f32)
        gpos = start - ATT_WINDOW + jnp.arange(band)
        valid = in_window & ((gpos >= 0) & (gpos < S))[None, :]
        qf = qblk.astype(f32)
        s_lat = jnp.einsum('bqkrd,bjkd->bkrqj', qf, kb) * ATT_SCALE
        s_lat = jnp.where(valid[None, None, None], s_lat, -jnp.inf)
        s_ctx = jnp.einsum('bqkrd,bckd->bkrqc', qf, kc) * ATT_SCALE
        p = softmax_with_sink(jnp.concatenate([s_ctx, s_lat], axis=-1), sink_b)
        o = jnp.einsum('bkrqc,bckd->bqkrd', p[..., :n_ctx], vc) \
            + jnp.einsum('bkrqj,bjkd->bqkrd', p[..., n_ctx:], vb)
        return o.astype(q.dtype)

    outs = lax.map(block_fn, (qb, jnp.arange(nb)))
    return jnp.moveaxis(outs, 0, 1).reshape(bsz, S, H * Dh)


def context_attention(q, k, v, sink):
    f32 = jnp.float32
    bsz, C, H, Dh = q.shape
    qg = q.astype(f32).reshape(bsz, C, ATT_KV_HEADS, ATT_Q_PER_KV, Dh)
    logits = jnp.einsum('bqkrd,bjkd->bkrqj', qg, k.astype(f32)) * ATT_SCALE
    sink_b = sink.astype(f32).reshape(ATT_KV_HEADS, ATT_Q_PER_KV)[None, :, :, None, None]
    p = softmax_with_sink(logits, sink_b)
    o = jnp.einsum('bkrqj,bjkd->bqkrd', p, v.astype(f32))
    return o.reshape(bsz, C, H * Dh).astype(q.dtype)


def window_attention_mix(h_ctx, h_lat, w_q, w_kv, q_norm_g, k_norm_g, sink, w_o, ang, compute_ctx):
    def keys_values(h):
        b, L, _ = h.shape
        k, v = jnp.split(h @ w_kv, 2, axis=-1)
        k = rms_norm(k.reshape(b, L, ATT_KV_HEADS, ATT_HEADDIM), k_norm_g)
        return k, v.reshape(b, L, ATT_KV_HEADS, ATT_HEADDIM)

    def queries(h):
        b, L, _ = h.shape
        return rms_norm((h @ w_q).reshape(b, L, ATT_HEADS, ATT_HEADDIM), q_norm_g)

    k_c, v_c = keys_values(h_ctx)
    k_l, v_l = keys_values(h_lat)
    k_l = apply_axial_rope(k_l, ang)
    q_l = apply_axial_rope(queries(h_lat), ang)
    out_lat = banded_window_attention(q_l, k_l, v_l, k_c, v_c, sink) @ w_o
    out_ctx = None
    if compute_ctx:
        out_ctx = context_attention(queries(h_ctx), k_c, v_c, sink) @ w_o
    return out_ctx, out_lat


def moe_ffn(h, router_w, router_bias, w_gate, w_up, w_down, ws_gate, ws_up, ws_down):
    f32 = jnp.float32
    T, D = h.shape
    scores = jax.nn.sigmoid(h.astype(f32) @ router_w.astype(f32))
    sel = scores + router_bias.astype(f32)
    grp = sel.reshape(T, N_EXPERT_GROUPS, N_EXPERTS // N_EXPERT_GROUPS)
    grp_score = jnp.sum(lax.top_k(grp, 2)[0], axis=-1)
    _, top_g = lax.top_k(grp_score, TOP_GROUPS)
    gmask = jnp.any(top_g[:, :, None] == jnp.arange(N_EXPERT_GROUPS), axis=1)
    emask = jnp.repeat(gmask, N_EXPERTS // N_EXPERT_GROUPS, axis=1)
    _, top_e = lax.top_k(jnp.where(emask, sel, -jnp.inf), TOP_K)
    wts = jnp.take_along_axis(scores, top_e, axis=-1)
    wts = wts / (jnp.sum(wts, axis=-1, keepdims=True) + 1e-20) * ROUTED_SCALE

    M = T * TOP_K
    e_flat = top_e.reshape(M)
    tok_flat = jnp.repeat(jnp.arange(T, dtype=jnp.int32), TOP_K)
    w_flat = wts.reshape(M)
    order = jnp.argsort(e_flat)
    e_sorted = e_flat[order]
    counts = jnp.zeros((N_EXPERTS,), jnp.int32).at[e_flat].add(1)
    padded = ((counts + MOE_BLOCK - 1) // MOE_BLOCK) * MOE_BLOCK
    pad_end = jnp.cumsum(padded)
    pad_start = pad_end - padded
    start = jnp.cumsum(counts) - counts
    dest = pad_start[e_sorted] + (jnp.arange(M, dtype=jnp.int32) - start[e_sorted])
    n_blocks = -(-M // MOE_BLOCK) + N_EXPERTS
    P = n_blocks * MOE_BLOCK
    slot_tok = jnp.full((P,), T, jnp.int32).at[dest].set(tok_flat[order])
    slot_w = jnp.zeros((P,), f32).at[dest].set(w_flat[order])
    block_start = jnp.arange(n_blocks, dtype=jnp.int32) * MOE_BLOCK
    block_expert = jnp.minimum(jnp.searchsorted(pad_end, block_start, side='right'), N_EXPERTS - 1)
    h_pad = jnp.concatenate([h, jnp.zeros((1, D), h.dtype)], axis=0)

    def body(acc, inp):
        toks, ws, e = inp
        xb = h_pad[toks]
        yb = (jax.nn.silu(xb @ w_gate[e]) * (xb @ w_up[e])) @ w_down[e]
        return acc.at[toks].add(yb.astype(f32) * ws[:, None]), None

    acc, _ = lax.scan(body, jnp.zeros((T + 1, D), f32),
                      (slot_tok.reshape(n_blocks, MOE_BLOCK), slot_w.reshape(n_blocks, MOE_BLOCK),
                       block_expert))
    shared = (jax.nn.silu(h @ ws_gate) * (h @ ws_up)) @ ws_down
    return acc[:T].astype(h.dtype) + shared


def setup_inputs(seed: int = 0) -> dict:
    key = jax.random.key(seed)
    ks = iter(jax.random.split(key, 40))
    f32 = jnp.float32

    def nrm(shape, scale):
        return jax.random.normal(next(ks), shape, f32) * scale

    def gain(shape):
        return 1.0 + nrm(shape, 0.05)

    D = D_MODEL
    x = nrm((BATCH, SEQ, D), 1.0)
    c = nrm((BATCH, D), 1.0)
    ctx = nrm((BATCH, CTX_LEN, D), 1.0)
    c_ctx = nrm((D,), 1.0)
    mod_w = nrm((DEPTH, D, 6 * D), 0.5 * D ** -0.5)
    mod_b = nrm((DEPTH, 6 * D), 0.02)
    norm1_g = gain((DEPTH, D))
    norm2_g = gain((DEPTH, D))
    ev_w_in = nrm((N_EVEN, D, EVEN_IN), D ** -0.5)
    ev_gm_ws = nrm((N_EVEN, A_GROUPS, A_CHUNK, A_CHUNK), A_CHUNK ** -0.5)
    ev_gm_bs = 1.0 + nrm((N_EVEN, A_GROUPS, A_CHUNK), 0.1)
    ev_gm_ln_g = gain((N_EVEN, A_WIDTH))
    ev_gm_ln_b = nrm((N_EVEN, A_WIDTH), 0.02)
    ev_conv_w = nrm((N_EVEN, SSD_CONV, SSD_CONV_CH), SSD_CONV ** -0.5)
    ev_conv_b = nrm((N_EVEN, SSD_CONV_CH), 0.02)
    ev_a_log = jnp.log(jax.random.uniform(next(ks), (N_EVEN, 2, SSD_HEADS), f32, 1.0, 16.0))
    dt0 = jnp.exp(jax.random.uniform(next(ks), (N_EVEN, 2, SSD_HEADS), f32,
                                     math.log(1e-3), math.log(1e-1)))
    ev_dt_bias = dt0 + jnp.log(-jnp.expm1(-dt0))
    ev_d_skip = gain((N_EVEN, SSD_HEADS))
    ev_ssd_norm_g = gain((N_EVEN, SSD_WIDTH))
    ev_w_out = nrm((N_EVEN, EVEN_OUT, D), EVEN_OUT ** -0.5)
    od_w_q = nrm((N_ODD, D, ATT_HEADS * ATT_HEADDIM), D ** -0.5)
    od_w_kv = nrm((N_ODD, D, 2 * ATT_KV_HEADS * ATT_HEADDIM), D ** -0.5)
    od_q_norm_g = gain((N_ODD, ATT_HEADDIM))
    od_k_norm_g = gain((N_ODD, ATT_HEADDIM))
    od_sink = nrm((N_ODD, ATT_HEADS), 0.5)
    od_w_o = nrm((N_ODD, ATT_HEADS * ATT_HEADDIM, D), (ATT_HEADS * ATT_HEADDIM) ** -0.5)
    moe_router_w = nrm((DEPTH, D, N_EXPERTS), D ** -0.5)
    moe_router_bias = nrm((DEPTH, N_EXPERTS), 0.01)
    moe_w_gate = nrm((DEPTH, N_EXPERTS, D, EXPERT_HIDDEN), D ** -0.5)
    moe_w_up = nrm((DEPTH, N_EXPERTS, D, EXPERT_HIDDEN), D ** -0.5)
    moe_w_down = nrm((DEPTH, N_EXPERTS, EXPERT_HIDDEN, D), EXPERT_HIDDEN ** -0.5)
    moe_ws_gate = nrm((DEPTH, D, SHARED_HIDDEN), D ** -0.5)
    moe_ws_up = nrm((DEPTH, D, SHARED_HIDDEN), D ** -0.5)
    moe_ws_down = nrm((DEPTH, SHARED_HIDDEN, D), SHARED_HIDDEN ** -0.5)
    return {"x": x, "c": c, "ctx": ctx, "c_ctx": c_ctx,
            "mod_w": mod_w, "mod_b": mod_b, "norm1_g": norm1_g, "norm2_g": norm2_g,
            "ev_w_in": ev_w_in, "ev_gm_ws": ev_gm_ws, "ev_gm_bs": ev_gm_bs,
            "ev_gm_ln_g": ev_gm_ln_g, "ev_gm_ln_b": ev_gm_ln_b,
            "ev_conv_w": ev_conv_w, "ev_conv_b": ev_conv_b, "ev_a_log": ev_a_log,
            "ev_dt_bias": ev_dt_bias, "ev_d_skip": ev_d_skip, "ev_ssd_norm_g": ev_ssd_norm_g,
            "ev_w_out": ev_w_out,
            "od_w_q": od_w_q, "od_w_kv": od_w_kv, "od_q_norm_g": od_q_norm_g,
            "od_k_norm_g": od_k_norm_g, "od_sink": od_sink, "od_w_o": od_w_o,
            "moe_router_w": moe_router_w, "moe_router_bias": moe_router_bias,
            "moe_w_gate": moe_w_gate, "moe_w_up": moe_w_up, "moe_w_down": moe_w_down,
            "moe_ws_gate": moe_ws_gate, "moe_ws_up": moe_ws_up, "moe_ws_down": moe_ws_down}


def reference(x, c, ctx, c_ctx, mod_w, mod_b, norm1_g, norm2_g,
              ev_w_in, ev_gm_ws, ev_gm_bs, ev_gm_ln_g, ev_gm_ln_b, ev_conv_w, ev_conv_b,
              ev_a_log, ev_dt_bias, ev_d_skip, ev_ssd_norm_g, ev_w_out,
              od_w_q, od_w_kv, od_q_norm_g, od_k_norm_g, od_sink, od_w_o,
              moe_router_w, moe_router_bias, moe_w_gate, moe_w_up, moe_w_down,
              moe_ws_gate, moe_ws_up, moe_ws_down):
    f32 = jnp.float32
    bsz, n_lat, d = x.shape
    n_ctx = ctx.shape[1]
    ang = axial_rope_angles(n_lat)
    silu_c = jax.nn.silu(c.astype(f32))
    silu_cc = jax.nn.silu(c_ctx.astype(f32))
    h_lat, h_ctx = x, ctx
    for layer in range(DEPTH):
        last = layer == DEPTH - 1
        j = layer // 2
        wm = mod_w[layer].astype(f32)
        bm = mod_b[layer].astype(f32)
        mod_l = jnp.split((silu_c @ wm + bm).astype(x.dtype)[:, None, :], 6, axis=-1)
        mod_c = jnp.split((silu_cc @ wm + bm).astype(x.dtype)[None, None, :], 6, axis=-1)
        a_lat = modulate(h_lat, norm1_g[layer], mod_l[0], mod_l[1])
        a_ctx = modulate(h_ctx, norm1_g[layer], mod_c[0], mod_c[1])
        if layer % 2 == 0:
            o_ctx, o_lat = gmlp_ssd_mix(a_ctx, a_lat, ev_w_in[j], ev_gm_ws[j], ev_gm_bs[j],
                                        ev_gm_ln_g[j], ev_gm_ln_b[j], ev_conv_w[j], ev_conv_b[j],
                                        ev_a_log[j], ev_dt_bias[j], ev_d_skip[j],
                                        ev_ssd_norm_g[j], ev_w_out[j], not last)
        else:
            o_ctx, o_lat = window_attention_mix(a_ctx, a_lat, od_w_q[j], od_w_kv[j],
                                                od_q_norm_g[j], od_k_norm_g[j], od_sink[j],
                                                od_w_o[j], ang, not last)
        h_lat = h_lat + mod_l[2] * o_lat
        m_lat = modulate(h_lat, norm2_g[layer], mod_l[3], mod_l[4])
        moe_args = (moe_router_w[layer], moe_router_bias[layer], moe_w_gate[layer],
                    moe_w_up[layer], moe_w_down[layer], moe_ws_gate[layer],
                    moe_ws_up[layer], moe_ws_down[layer])
        if last:
            f_lat = moe_ffn(m_lat.reshape(-1, d), *moe_args).reshape(bsz, n_lat, d)
        else:
            h_ctx = h_ctx + mod_c[2] * o_ctx
            m_ctx = modulate(h_ctx, norm2_g[layer], mod_c[3], mod_c[4])
            f = moe_ffn(jnp.concatenate([m_ctx.reshape(-1, d), m_lat.reshape(-1, d)], axis=0),
                        *moe_args)
            f_ctx = f[: bsz * n_ctx].reshape(bsz, n_ctx, d)
            f_lat = f[bsz * n_ctx:].reshape(bsz, n_lat, d)
            h_ctx = h_ctx + mod_c[5] * f_ctx
        h_lat = h_lat + mod_l[5] * f_lat
    return h_lat
```

```python
import functools

import jax
import jax.numpy as jnp
from jax import lax
from jax.experimental import pallas as pl
from jax.experimental.pallas import tpu as pltpu

f32 = jnp.float32
bf16 = jnp.bfloat16
i32 = jnp.int32
HIGHEST = lax.Precision.HIGHEST

D_MODEL = 2048
EPS = 1e-6
GRID_W = 64

A_CHUNK = 128
A_GROUPS = 16

SSD_HEADS = 32
SSD_HEADDIM = 64
SSD_GROUPS = 8
SSD_STATE = 128
SSD_CHUNK = 128
SSD_HPG = SSD_HEADS // SSD_GROUPS
SSD_GW = SSD_HPG * SSD_HEADDIM

ATT_HEADDIM = 64
ATT_HEADS = 32
ATT_KV_HEADS = 4
ATT_Q_PER_KV = 8
ATT_WINDOW = 128
ATT_BLOCK = 128
ATT_SCALE = ATT_HEADDIM ** -0.5
ROPE_BASE = 10000.0

N_EXPERTS = 64
N_EXPERT_GROUPS = 8
GROUP_SIZE = N_EXPERTS // N_EXPERT_GROUPS
TOP_GROUPS = 4
TOP_K = 8
EXPERT_HIDDEN = 512
ROUTED_SCALE = 2.5

ROW_TILE = 256
BIG_TILE = 768
MOE_BLK = 256
COMB_TILE = 64
VMEM_LIMIT = 56 * 1024 * 1024

_NT = (((1,), (1,)), ((), ()))
_TN = (((0,), (0,)), ((), ()))


def _cparams(*sem):
    return pltpu.CompilerParams(dimension_semantics=sem, vmem_limit_bytes=VMEM_LIMIT)


def _pick_tile(m, options):
    for t in options:
        if m % t == 0:
            return t
    raise ValueError(f"no tile for {m}")


def _mod_kernel(c_ref, w_ref, b_ref, o_ref):
    sc = jax.nn.silu(c_ref[...])
    o_ref[...] = jnp.dot(sc, w_ref[...], precision=HIGHEST,
                         preferred_element_type=f32) + b_ref[...]


def modulation(c_all, mod_w, mod_b, layer):
    rows, d = c_all.shape
    n = mod_w.shape[2]
    tn = 1024
    return pl.pallas_call(
        _mod_kernel,
        grid=(n // tn,),
        in_specs=[pl.BlockSpec((rows, d), lambda j: (0, 0)),
                  pl.BlockSpec((None, d, tn), lambda j: (layer, 0, j)),
                  pl.BlockSpec((None, 1, tn), lambda j: (layer, 0, j))],
        out_specs=pl.BlockSpec((rows, tn), lambda j: (0, j)),
        out_shape=jax.ShapeDtypeStruct((rows, n), f32),
        compiler_params=_cparams("arbitrary"),
        name="modulation",
    )(c_all, mod_w, mod_b.reshape(mod_b.shape[0], 1, n))


def _mod_row_map(geom, tile):
    bsz, L, n_ctx = geom
    tpb = L // tile
    nct = n_ctx // tile

    def mod_row(i):
        return jnp.where((i % tpb) < nct, bsz, i // tpb)
    return mod_row


def _lat_tile_map(geom, tile):
    bsz, L, n_ctx = geom
    lpb = (L - n_ctx) // tile
    tpb = L // tile
    nct = n_ctx // tile

    def joint(t):
        return (t // lpb) * tpb + nct + (t % lpb)
    return joint


def _norm_mod_kernel(x_ref, g_ref, mod_ref, o_ref, *, shift_idx, scale_idx):
    x = x_ref[...]
    ms = jnp.mean(x * x, axis=-1, keepdims=True)
    y = x * lax.rsqrt(ms + EPS) * g_ref[...]
    y = y * (1.0 + mod_ref[0, scale_idx:scale_idx + 1, :]) + mod_ref[0, shift_idx:shift_idx + 1, :]
    o_ref[...] = y.astype(o_ref.dtype)


def norm_modulate(h, g, mod3, geom, shift_idx, scale_idx):
    m, d = h.shape
    mod_row = _mod_row_map(geom, ROW_TILE)
    return pl.pallas_call(
        functools.partial(_norm_mod_kernel, shift_idx=shift_idx, scale_idx=scale_idx),
        grid=(m // ROW_TILE,),
        in_specs=[pl.BlockSpec((ROW_TILE, d), lambda i: (i, 0)),
                  pl.BlockSpec((1, d), lambda i: (0, 0)),
                  pl.BlockSpec((1, 6, d), lambda i: (mod_row(i), 0, 0))],
        out_specs=pl.BlockSpec((ROW_TILE, d), lambda i: (i, 0)),
        out_shape=jax.ShapeDtypeStruct((m, d), bf16),
        compiler_params=_cparams("parallel"),
        name="norm_modulate",
    )(h, g.reshape(1, d), mod3)


def _mm_kernel(a_ref, w_ref, o_ref):
    o_ref[...] = jnp.dot(a_ref[...], w_ref[...],
                         preferred_element_type=f32).astype(o_ref.dtype)


def matmul_nn(a, w, out_dtype, tn=1024):
    m, k = a.shape
    n = w.shape[1]
    tm = _pick_tile(m, (1024, 768, 512, 256))
    return pl.pallas_call(
        _mm_kernel,
        grid=(n // tn, m // tm),
        in_specs=[pl.BlockSpec((tm, k), lambda j, i: (i, 0)),
                  pl.BlockSpec((k, tn), lambda j, i: (0, j))],
        out_specs=pl.BlockSpec((tm, tn), lambda j, i: (i, j)),
        out_shape=jax.ShapeDtypeStruct((m, n), out_dtype),
        compiler_params=_cparams("parallel", "parallel"),
        name="matmul_nn",
    )(a, w)


def _dt_kernel(a_ref, w_ref, o_ref):
    o_ref[...] = lax.dot_general(w_ref[...], a_ref[...].astype(f32), _NT,
                                 precision=HIGHEST, preferred_element_type=f32)


def dt_project(a, w_dt_t):
    m, k = a.shape
    r = w_dt_t.shape[0]
    tm = _pick_tile(m, (1024, 768, 512, 256))
    return pl.pallas_call(
        _dt_kernel,
        grid=(m // tm,),
        in_specs=[pl.BlockSpec((tm, k), lambda i: (i, 0)),
                  pl.BlockSpec((r, k), lambda i: (0, 0))],
        out_specs=pl.BlockSpec((r, tm), lambda i: (0, i)),
        out_shape=jax.ShapeDtypeStruct((r, m), f32),
        compiler_params=_cparams("parallel"),
        name="dt_project",
    )(a, w_dt_t)


def _gelu_exact(x):
    return 0.5 * x * (1.0 + lax.erf(x * (2.0 ** -0.5)))


def _gmlp_kernel(u_ref, v_ref, ws_ref, bexp_ref, lng_ref, lnb_ref, o_ref):
    u = _gelu_exact(u_ref[...])
    v = _gelu_exact(v_ref[...])
    mu = jnp.mean(v, axis=-1, keepdims=True)
    vc = v - mu
    var = jnp.mean(vc * vc, axis=-1, keepdims=True)
    vn = (vc * lax.rsqrt(var + EPS) * lng_ref[...] + lnb_ref[...]).astype(bf16)
    for g in range(A_GROUPS):
        sl = slice(g * A_CHUNK, (g + 1) * A_CHUNK)
        mixed = jnp.dot(ws_ref[g], vn[:, sl], preferred_element_type=f32) + bexp_ref[:, sl]
        o_ref[:, sl] = (u[:, sl] * mixed).astype(o_ref.dtype)


def gmlp_mix(proj, ws_bf, bexp, ln_g, ln_b):
    m = proj.shape[0]
    w = A_GROUPS * A_CHUNK
    return pl.pallas_call(
        _gmlp_kernel,
        grid=(m // A_CHUNK,),
        in_specs=[pl.BlockSpec((A_CHUNK, w), lambda c: (c, 0)),
                  pl.BlockSpec((A_CHUNK, w), lambda c: (c, 1)),
                  pl.BlockSpec((A_GROUPS, A_CHUNK, A_CHUNK), lambda c: (0, 0, 0)),
                  pl.BlockSpec((A_CHUNK, w), lambda c: (0, 0)),
                  pl.BlockSpec((1, w), lambda c: (0, 0)),
                  pl.BlockSpec((1, w), lambda c: (0, 0))],
        out_specs=pl.BlockSpec((A_CHUNK, w), lambda c: (c, 0)),
        out_shape=jax.ShapeDtypeStruct((m, w), bf16),
        compiler_params=_cparams("parallel"),
        name="gmlp_mix",
    )(proj, proj, ws_bf, bexp, ln_g.reshape(1, w), ln_b.reshape(1, w))


def _head_expand(v, lane):
    out = v[:, 3:4]
    for r in (2, 1, 0):
        out = jnp.where(lane < SSD_HEADDIM * (r + 1), v[:, r:r + 1], out)
    return out


def _row_expand(v, row):
    out = v[:, 3:4]
    for r in (2, 1, 0):
        out = jnp.where(row < SSD_HEADDIM * (r + 1), v[:, r:r + 1], out)
    return out


def _ssd_kernel(x_ref, b_ref, c_ref, z_ref, dtr_ref, dtc_ref,
                cwx_ref, cwb_ref, cwc_ref, cbx_ref, cbb_ref, cbc_ref,
                alr_ref, alc_ref, dbr_ref, dbc_ref, dsk_ref, ng_ref,
                o_ref,
                xs_s, bs_s, cs_s, dtr_s, dtar_s, dtc_s, dtac_s, suf_s, yacc_s, sf_s, sb_s,
                *, n_ctx, L):
    Q = SSD_CHUNK
    nc = L // Q
    ncc = n_ctx // Q
    H = SSD_HPG

    bias_r = dbr_ref[...]
    a_r = -jnp.exp(alr_ref[...])
    for c in range(nc):
        dt = jax.nn.softplus(dtr_ref[:, c * Q:(c + 1) * Q] + bias_r)
        dtr_s[c] = dt
        dtar_s[c] = dt * a_r
    dtc = jax.nn.softplus(dtc_ref[0] + dbc_ref[0])
    dtc_s[...] = dtc
    dtac_s[...] = dtc * (-jnp.exp(alc_ref[0]))

    ii = lax.broadcasted_iota(i32, (Q, Q), 0)
    jj = lax.broadcasted_iota(i32, (Q, Q), 1)
    lower = ii >= jj
    upper = ii <= jj
    tril = lower.astype(f32)
    triu = upper.astype(f32)
    lane = lax.broadcasted_iota(i32, (Q, SSD_GW), 1)
    row = lax.broadcasted_iota(i32, (SSD_GW, SSD_STATE), 0)

    def conv_act(ref, w_ref, bias_ref, s):
        xv = ref[pl.ds(s, Q), :]
        w = xv.shape[1]
        r0 = lax.broadcasted_iota(i32, (Q, w), 0)
        prev = ref[pl.ds(jnp.maximum(s - 1, 0), 1), :]
        nxt = ref[pl.ds(jnp.minimum(s + Q, L - 1), 1), :]
        has_prev = jnp.logical_and(s != 0, s != n_ctx)
        has_next = jnp.logical_and(s + Q != n_ctx, s + Q != L)
        prev = jnp.where(has_prev, prev, 0.0)
        nxt = jnp.where(has_next, nxt, 0.0)
        xm1 = jnp.where(r0 == 0, prev, pltpu.roll(xv, 1, 0))
        xp1 = jnp.where(r0 == Q - 1, nxt, pltpu.roll(xv, Q - 1, 0))
        y = w_ref[0:1, :] * xm1 + w_ref[1:2, :] * xv + w_ref[2:3, :] * xp1 + bias_ref[...]
        return jax.nn.silu(y)

    sf_s[...] = jnp.zeros_like(sf_s)
    sb_s[...] = jnp.zeros_like(sb_s)

    def fwd_body(c, carry):
        s = pl.multiple_of(c * Q, Q)
        X = conv_act(x_ref, cwx_ref, cbx_ref, s)
        Bc = conv_act(b_ref, cwb_ref, cbb_ref, s)
        Cc = conv_act(c_ref, cwc_ref, cbc_ref, s)
        xs_s[pl.ds(s, Q), :] = X
        bs_s[pl.ds(s, Q), :] = Bc
        cs_s[pl.ds(s, Q), :] = Cc
        Bb = Bc.astype(bf16)
        Cb = Cc.astype(bf16)
        cb = lax.dot_general(Cb, Bb, _NT, preferred_element_type=f32)
        dta_c = dtac_s[pl.ds(s, Q), :]
        dta_r = dtar_s[c]
        dt_r = dtr_s[c]
        p_col = jnp.dot(tril, dta_c, precision=HIGHEST, preferred_element_type=f32)
        s_col = jnp.dot(triu, dta_c, precision=HIGHEST, preferred_element_type=f32)
        p_row = jnp.dot(dta_r, triu, precision=HIGHEST, preferred_element_type=f32)
        s_row = jnp.dot(dta_r, tril, precision=HIGHEST, preferred_element_type=f32)
        suf_s[pl.ds(s, Q), :] = s_col
        y = jnp.zeros((Q, SSD_GW), f32)
        for r in range(H):
            lf = jnp.exp(jnp.where(lower, p_col[:, r:r + 1] - p_row[r:r + 1, :], -jnp.inf))
            lb = jnp.exp(jnp.where(upper, s_col[:, H + r:H + r + 1] - s_row[H + r:H + r + 1, :],
                                   -jnp.inf))
            mr = cb * (lf * dt_r[r:r + 1, :] + lb * dt_r[H + r:H + r + 1, :])
            head = jnp.logical_and(lane >= SSD_HEADDIM * r, lane < SSD_HEADDIM * (r + 1))
            xm = jnp.where(head, X, 0.0).astype(bf16)
            y = y + jnp.dot(mr.astype(bf16), xm, preferred_element_type=f32)
        sf = sf_s[...]
        yi = lax.dot_general(Cb, sf.astype(bf16), _NT, preferred_element_type=f32)
        pf = p_col[:, 0:H]
        y = y + yi * _head_expand(jnp.exp(pf), lane)
        yacc_s[pl.ds(s, Q), :] = y
        tot = p_col[Q - 1:Q, 0:H]
        wf = dtc_s[pl.ds(s, Q), 0:H] * jnp.exp(tot - pf)
        xw = (X * _head_expand(wf, lane)).astype(bf16)
        upd = lax.dot_general(xw, Bb, _TN, preferred_element_type=f32)
        sf_s[...] = sf * _row_expand(jnp.exp(tot), row) + upd
        return carry

    lax.fori_loop(0, nc, fwd_body, 0)

    def bwd_body(k, carry):
        c = jnp.where(k < ncc, ncc - 1 - k, nc - 1 - (k - ncc))
        s = pl.multiple_of(c * Q, Q)
        X = xs_s[pl.ds(s, Q), :]
        Bb = bs_s[pl.ds(s, Q), :].astype(bf16)
        Cb = cs_s[pl.ds(s, Q), :].astype(bf16)
        ab = suf_s[pl.ds(s, Q), H:2 * H]
        tot = ab[0:1, :]
        sb = sb_s[...]
        yi = lax.dot_general(Cb, sb.astype(bf16), _NT, preferred_element_type=f32)
        y = yacc_s[pl.ds(s, Q), :] + yi * _head_expand(jnp.exp(ab), lane) + dsk_ref[...] * X
        wb = dtc_s[pl.ds(s, Q), H:2 * H] * jnp.exp(tot - ab)
        xw = (X * _head_expand(wb, lane)).astype(bf16)
        upd = lax.dot_general(xw, Bb, _TN, preferred_element_type=f32)
        sb_s[...] = sb * _row_expand(jnp.exp(tot), row) + upd
        y = y * jax.nn.silu(z_ref[pl.ds(s, Q), :])
        ms = jnp.mean(y * y, axis=-1, keepdims=True)
        o_ref[pl.ds(s, Q), :] = (y * lax.rsqrt(ms + EPS) * ng_ref[...]).astype(o_ref.dtype)
        return carry

    lax.fori_loop(0, nc, bwd_body, 0)


def ssd_mix(proj, dt_t, conv_w, conv_b, a_log, dt_bias, d_skip, norm_g, geom):
    bsz, L, n_ctx = geom
    m = proj.shape[0]
    G, H, GW, N = SSD_GROUPS, SSD_HPG, SSD_GW, SSD_STATE
    nc = L // SSD_CHUNK
    col0 = proj.shape[1] - (SSD_HEADS * SSD_HEADDIM + 2 * G * N)
    zc0 = col0 - SSD_HEADS * SSD_HEADDIM
    xb, zb = col0 // GW, zc0 // GW
    bb = (col0 + SSD_HEADS * SSD_HEADDIM) // N
    cb = bb + G
    cxb, cbb, ccb = 0, (SSD_HEADS * SSD_HEADDIM) // N, (SSD_HEADS * SSD_HEADDIM) // N + G

    def to_gdr(p):
        return p.reshape(2, G, H).transpose(1, 0, 2).reshape(G, 2 * H)

    al, db = to_gdr(a_log), to_gdr(dt_bias)
    dt_col = dt_t.reshape(G, 2 * H, m).transpose(0, 2, 1)
    dsk = jnp.repeat(d_skip, SSD_HEADDIM).reshape(1, -1)
    conv_b2 = conv_b.reshape(1, -1)
    kern = functools.partial(_ssd_kernel, n_ctx=n_ctx, L=L)
    return pl.pallas_call(
        kern,
        grid=(bsz, G),
        in_specs=[
            pl.BlockSpec((L, GW), lambda b, g: (b, xb + g)),
            pl.BlockSpec((L, N), lambda b, g: (b, bb + g)),
            pl.BlockSpec((L, N), lambda b, g: (b, cb + g)),
            pl.BlockSpec((L, GW), lambda b, g: (b, zb + g)),
            pl.BlockSpec((2 * H, L), lambda b, g: (g, b)),
            pl.BlockSpec((1, L, 2 * H), lambda b, g: (g, b, 0)),
            pl.BlockSpec((3, GW), lambda b, g: (0, cxb + g)),
            pl.BlockSpec((3, N), lambda b, g: (0, cbb + g)),
            pl.BlockSpec((3, N), lambda b, g: (0, ccb + g)),
            pl.BlockSpec((1, GW), lambda b, g: (0, cxb + g)),
            pl.BlockSpec((1, N), lambda b, g: (0, cbb + g)),
            pl.BlockSpec((1, N), lambda b, g: (0, ccb + g)),
            pl.BlockSpec((2 * H, 1), lambda b, g: (g, 0)),
            pl.BlockSpec((1, 1, 2 * H), lambda b, g: (g, 0, 0)),
            pl.BlockSpec((2 * H, 1), lambda b, g: (g, 0)),
            pl.BlockSpec((1, 1, 2 * H), lambda b, g: (g, 0, 0)),
            pl.BlockSpec((1, GW), lambda b, g: (0, g)),
            pl.BlockSpec((1, GW), lambda b, g: (0, g)),
        ],
        out_specs=pl.BlockSpec((L, GW), lambda b, g: (b, g)),
        out_shape=jax.ShapeDtypeStruct((m, G * GW), bf16),
        scratch_shapes=[
            pltpu.VMEM((L, GW), f32), pltpu.VMEM((L, N), f32), pltpu.VMEM((L, N), f32),
            pltpu.VMEM((nc, 2 * H, SSD_CHUNK), f32), pltpu.VMEM((nc, 2 * H, SSD_CHUNK), f32),
            pltpu.VMEM((L, 2 * H), f32), pltpu.VMEM((L, 2 * H), f32), pltpu.VMEM((L, 2 * H), f32),
            pltpu.VMEM((L, GW), f32),
            pltpu.VMEM((GW, N), f32), pltpu.VMEM((GW, N), f32),
        ],
        compiler_params=_cparams("parallel", "parallel"),
        name="ssd_mix",
    )(proj, proj, proj, proj, dt_t, dt_col,
      conv_w, conv_w, conv_w, conv_b2, conv_b2, conv_b2,
      al.reshape(G * 2 * H, 1), al.reshape(G, 1, 2 * H),
      db.reshape(G * 2 * H, 1), db.reshape(G, 1, 2 * H),
      dsk, norm_g.reshape(1, -1))


def _outproj_kernel(*refs, n_in, n_ctx, tpb, tm):
    a_refs = refs[:n_in]
    w_refs = refs[n_in:2 * n_in]
    h_ref, ml_ref, mc_ref, o_ref = refs[2 * n_in:]
    i = pl.program_id(1)
    acc = jnp.dot(a_refs[0][...], w_refs[0][...], preferred_element_type=f32)
    for k in range(1, n_in):
        acc = acc + jnp.dot(a_refs[k][...], w_refs[k][...], preferred_element_type=f32)
    n_ctx_rows = jnp.clip(n_ctx - (i % tpb) * tm, 0, tm)
    row = lax.broadcasted_iota(i32, acc.shape, 0)
    gate = jnp.where(row < n_ctx_rows, mc_ref[0, 2:3, :], ml_ref[0, 2:3, :])
    o_ref[...] = h_ref[...] + gate * acc


def out_project_joint(a_list, w_list, h, mod3, geom, tn=1024):
    bsz, L, n_ctx = geom
    m, d = h.shape
    tm = _pick_tile(L, (BIG_TILE, 512, 256))
    tpb = L // tm
    n_in = len(a_list)
    k = a_list[0].shape[1]
    kern = functools.partial(_outproj_kernel, n_in=n_in, n_ctx=n_ctx, tpb=tpb, tm=tm)
    in_specs = ([pl.BlockSpec((tm, k), lambda j, i: (i, 0)) for _ in range(n_in)]
                + [pl.BlockSpec((k, tn), lambda j, i: (0, j)) for _ in range(n_in)]
                + [pl.BlockSpec((tm, tn), lambda j, i: (i, j)),
                   pl.BlockSpec((1, 6, tn), lambda j, i: (i // tpb, 0, j)),
                   pl.BlockSpec((1, 6, tn), lambda j, i: (bsz, 0, j))])
    return pl.pallas_call(
        kern,
        grid=(d // tn, m // tm),
        in_specs=in_specs,
        out_specs=pl.BlockSpec((tm, tn), lambda j, i: (i, j)),
        out_shape=jax.ShapeDtypeStruct((m, d), f32),
        compiler_params=_cparams("parallel", "parallel"),
        name="out_project_joint",
    )(*a_list, *w_list, h, mod3, mod3)


def _outproj_lat_kernel(a_ref, w_ref, h_ref, ml_ref, o_ref):
    acc = jnp.dot(a_ref[...], w_ref[...], preferred_element_type=f32)
    o_ref[...] = h_ref[...] + ml_ref[0, 2:3, :] * acc


def out_project_lat(a, w, h_joint, mod3, geom, tn=1024):
    bsz, L, n_ctx = geom
    m, k = a.shape
    d = h_joint.shape[1]
    tm = ROW_TILE
    joint = _lat_tile_map(geom, tm)
    lpb = (L - n_ctx) // tm
    return pl.pallas_call(
        _outproj_lat_kernel,
        grid=(d // tn, m // tm),
        in_specs=[pl.BlockSpec((tm, k), lambda j, i: (i, 0)),
                  pl.BlockSpec((k, tn), lambda j, i: (0, j)),
                  pl.BlockSpec((tm, tn), lambda j, i: (joint(i), j)),
                  pl.BlockSpec((1, 6, tn), lambda j, i: (i // lpb, 0, j))],
        out_specs=pl.BlockSpec((tm, tn), lambda j, i: (i, j)),
        out_shape=jax.ShapeDtypeStruct((m, d), f32),
        compiler_params=_cparams("parallel", "parallel"),
        name="out_project_lat",
    )(a, w, h_joint, mod3)


def _qkv_kernel(a_ref, w_ref, gq_ref, gk_ref, cos_ref, sin_ref, o_ref, *, n_qblk):
    j = pl.program_id(1)
    acc = lax.dot_general(w_ref[...], a_ref[...], _NT, preferred_element_type=f32)

    @pl.when(j <= n_qblk)
    def _():
        g = jnp.where(j < n_qblk, gq_ref[...] * ATT_SCALE, gk_ref[...])
        cs = cos_ref[...]
        sn = sin_ref[...]
        hd = ATT_HEADDIM
        q4 = hd // 4
        for hh in range(acc.shape[0] // hd):
            blk = acc[hh * hd:(hh + 1) * hd, :]
            ms = jnp.mean(blk * blk, axis=0, keepdims=True)
            xn = blk * lax.rsqrt(ms + EPS) * g
            sw = jnp.concatenate([xn[q4:2 * q4], xn[0:q4], xn[3 * q4:4 * q4], xn[2 * q4:3 * q4]],
                                 axis=0)
            o_ref[hh * hd:(hh + 1) * hd, :] = xn * cs + sw * sn

    @pl.when(j > n_qblk)
    def _():
        o_ref[...] = acc


def qkv_project_t(a, w_t, gq, gk, cos_t, sin_t, geom):
    bsz, L, n_ctx = geom
    m, k = a.shape
    r = w_t.shape[0]
    rb = 256
    tm = _pick_tile(L, (BIG_TILE, 512, 256))
    tpb = L // tm
    n_qblk = (ATT_HEADS * ATT_HEADDIM) // rb
    return pl.pallas_call(
        functools.partial(_qkv_kernel, n_qblk=n_qblk),
        grid=(m // tm, r // rb),
        in_specs=[pl.BlockSpec((tm, k), lambda i, j: (i, 0)),
                  pl.BlockSpec((rb, k), lambda i, j: (j, 0)),
                  pl.BlockSpec((ATT_HEADDIM, 1), lambda i, j: (0, 0)),
                  pl.BlockSpec((ATT_HEADDIM, 1), lambda i, j: (0, 0)),
                  pl.BlockSpec((ATT_HEADDIM, tm), lambda i, j: (0, i % tpb)),
                  pl.BlockSpec((ATT_HEADDIM, tm), lambda i, j: (0, i % tpb))],
        out_specs=pl.BlockSpec((rb, tm), lambda i, j: (j, i)),
        out_shape=jax.ShapeDtypeStruct((r, m), f32),
        compiler_params=_cparams("parallel", "parallel"),
        name="qkv_project_t",
    )(a, w_t, gq.reshape(-1, 1), gk.reshape(-1, 1), cos_t, sin_t)


def _attn_kernel(sink_ref, q_ref, kc_ref, k0_ref, k1_ref, k2_ref,
                 vc_ref, v0_ref, v1_ref, v2_ref, o_ref, ot_s, *, n_lat, n_ctx):
    qi = pl.program_id(1)
    kvh = pl.program_id(2)
    blk = ATT_BLOCK
    hd = ATT_HEADDIM
    kt = jnp.concatenate([kc_ref[...], k0_ref[...], k1_ref[...], k2_ref[...]], axis=1)
    vt = jnp.concatenate([vc_ref[...], v0_ref[...], v1_ref[...], v2_ref[...]], axis=1)
    kt = kt.astype(bf16)
    vt = vt.astype(bf16)
    nk = n_ctx + 3 * blk
    krow = lax.broadcasted_iota(i32, (nk, blk), 0)
    qcol = lax.broadcasted_iota(i32, (nk, blk), 1)
    kpos = (qi - 1) * blk + (krow - n_ctx)
    qpos = qi * blk + qcol
    band_ok = jnp.logical_and(jnp.abs(kpos - qpos) <= ATT_WINDOW,
                              jnp.logical_and(kpos >= 0, kpos < n_lat))
    valid = jnp.logical_or(krow < n_ctx, band_ok)
    for r in range(ATT_Q_PER_KV):
        qt = q_ref[r * hd:(r + 1) * hd, :].astype(bf16)
        s = lax.dot_general(kt, qt, _TN, preferred_element_type=f32)
        s = jnp.where(valid, s, -jnp.inf)
        sink = sink_ref[kvh * ATT_Q_PER_KV + r]
        mx = jnp.maximum(jnp.max(s, axis=0, keepdims=True), sink)
        p = jnp.exp(s - mx)
        den = jnp.sum(p, axis=0, keepdims=True) + jnp.exp(sink - mx)
        o = jnp.dot(vt, p.astype(bf16), preferred_element_type=f32)
        ot_s[r * hd:(r + 1) * hd, :] = o / den
    o_ref[...] = ot_s[...].T.astype(o_ref.dtype)


def window_attention(qkv_t, sink, geom):
    bsz, L, n_ctx = geom
    n_lat = L - n_ctx
    blk = ATT_BLOCK
    hd = ATT_HEADDIM
    nqb = n_lat // blk
    cpb = L // blk
    cb0 = n_ctx // blk
    q_rows = ATT_Q_PER_KV * hd
    k_rb = (ATT_HEADS * hd) // hd
    v_rb = k_rb + ATT_KV_HEADS

    def band(rb0, off):
        def imap(b, qi, kvh, sink_ref):
            kb = jnp.clip(qi + off, 0, nqb - 1)
            return (rb0 + kvh, b * cpb + cb0 + kb)
        return pl.BlockSpec((hd, blk), imap)

    def ctx(rb0):
        return pl.BlockSpec((hd, n_ctx), lambda b, qi, kvh, s: (rb0 + kvh, b * (L // n_ctx)))

    kern = functools.partial(_attn_kernel, n_lat=n_lat, n_ctx=n_ctx)
    grid_spec = pltpu.PrefetchScalarGridSpec(
        num_scalar_prefetch=1,
        grid=(bsz, nqb, ATT_KV_HEADS),
        in_specs=[pl.BlockSpec((q_rows, blk), lambda b, qi, kvh, s: (kvh, b * cpb + cb0 + qi)),
                  ctx(k_rb), band(k_rb, -1), band(k_rb, 0), band(k_rb, 1),
                  ctx(v_rb), band(v_rb, -1), band(v_rb, 0), band(v_rb, 1)],
        out_specs=pl.BlockSpec((blk, q_rows), lambda b, qi, kvh, s: (b * nqb + qi, kvh)),
        scratch_shapes=[pltpu.VMEM((q_rows, blk), f32)],
    )
    return pl.pallas_call(
        kern,
        grid_spec=grid_spec,
        out_shape=jax.ShapeDtypeStruct((bsz * n_lat, ATT_HEADS * hd), bf16),
        compiler_params=_cparams("parallel", "parallel", "parallel"),
        name="window_attention",
    )(sink, *([qkv_t] * 9))


def _first_max(v, idx, n):
    m = jnp.max(v, axis=0, keepdims=True)
    first = jnp.min(jnp.where(v == m, idx, n), axis=0, keepdims=True)
    return m, first


def _route_kernel(h_ref, g_ref, mod_ref, rw_ref, rb_ref, wsg_ref, wsu_ref, wsd_ref,
                  m_ref, sh_ref, te_ref, wt_ref, pos_ref, cnt_ref, carry_s):
    i = pl.program_id(0)
    T = h_ref.shape[0]
    E, NG, GS = N_EXPERTS, N_EXPERT_GROUPS, GROUP_SIZE

    @pl.when(i == 0)
    def _():
        carry_s[...] = jnp.zeros_like(carry_s)

    x = h_ref[...]
    ms = jnp.mean(x * x, axis=-1, keepdims=True)
    y = x * lax.rsqrt(ms + EPS) * g_ref[...]
    mt = y * (1.0 + mod_ref[0, 4:5, :]) + mod_ref[0, 3:4, :]
    m_ref[...] = mt
    mb = mt.astype(bf16)

    hg = jnp.dot(mb, wsg_ref[...], preferred_element_type=f32)
    hu = jnp.dot(mb, wsu_ref[...], preferred_element_type=f32)
    act = (jax.nn.silu(hg) * hu).astype(bf16)
    sh_ref[...] = jnp.dot(act, wsd_ref[...], preferred_element_type=f32)

    logits = lax.dot_general(rw_ref[...], mt, _NT, precision=HIGHEST,
                             preferred_element_type=f32)
    scores = jax.nn.sigmoid(logits)
    sel = scores + rb_ref[...]
    eidx = lax.broadcasted_iota(i32, (E, T), 0)
    midx = lax.broadcasted_iota(i32, (GS, T), 0)
    gidx = lax.broadcasted_iota(i32, (NG, T), 0)

    gs = jnp.zeros((NG, T), f32)
    for g in range(NG):
        blk = sel[g * GS:(g + 1) * GS, :]
        m1, f1 = _first_max(blk, midx, GS)
        m2 = jnp.max(jnp.where(midx == f1, -jnp.inf, blk), axis=0, keepdims=True)
        gs = jnp.where(gidx == g, m1 + m2, gs)
    gmask = jnp.zeros((NG, T), jnp.bool_)
    work = gs
    for _ in range(TOP_GROUPS):
        _, f = _first_max(work, gidx, NG)
        hit = gidx == f
        gmask = jnp.logical_or(gmask, hit)
        work = jnp.where(hit, -jnp.inf, work)
    cand = jnp.concatenate(
        [jnp.where(gmask[g:g + 1, :], sel[g * GS:(g + 1) * GS, :], -jnp.inf) for g in range(NG)],
        axis=0)
    hits = []
    chosen = jnp.zeros((E, T), jnp.bool_)
    work = cand
    for k in range(TOP_K):
        _, f = _first_max(work, eidx, E)
        hit = eidx == f
        hits.append(hit)
        chosen = jnp.logical_or(chosen, hit)
        work = jnp.where(hit, -jnp.inf, work)
        te_ref[k:k + 1, :] = f
    wts = [jnp.sum(jnp.where(h, scores, 0.0), axis=0, keepdims=True) for h in hits]
    tot = wts[0]
    for w in wts[1:]:
        tot = tot + w
    for k in range(TOP_K):
        wt_ref[k:k + 1, :] = wts[k] / (tot + 1e-20) * ROUTED_SCALE

    t0 = lax.broadcasted_iota(i32, (T, T), 0)
    t1 = lax.broadcasted_iota(i32, (T, T), 1)
    before = (t0 < t1).astype(bf16)
    chosen_f = chosen.astype(f32)
    cnt = jnp.dot(chosen_f.astype(bf16), before, preferred_element_type=f32) + carry_s[:, 0:1]
    for k in range(TOP_K):
        pos_ref[k:k + 1, :] = jnp.sum(jnp.where(hits[k], cnt, 0.0), axis=0,
                                      keepdims=True).astype(i32)
    new_carry = carry_s[...] + jnp.sum(chosen_f, axis=1, keepdims=True)
    carry_s[...] = new_carry
    cnt_ref[...] = new_carry.astype(i32)


def moe_route(h, g, mod3, rw_t, rbias, wsg, wsu, wsd, geom, lat_only):
    m, d = h.shape
    bsz, L, n_ctx = geom
    T = ROW_TILE
    if lat_only:
        lpb = (L - n_ctx) // T
        mod_row = lambda i: i // lpb
    else:
        mod_row = _mod_row_map(geom, T)
    E = N_EXPERTS
    hid = wsg.shape[1]
    const2 = lambda i: (0, 0)
    outs = pl.pallas_call(
        _route_kernel,
        grid=(m // T,),
        in_specs=[pl.BlockSpec((T, d), lambda i: (i, 0)),
                  pl.BlockSpec((1, d), const2),
                  pl.BlockSpec((1, 6, d), lambda i: (mod_row(i), 0, 0)),
                  pl.BlockSpec((E, d), const2),
                  pl.BlockSpec((E, 1), const2),
                  pl.BlockSpec((d, hid), const2),
                  pl.BlockSpec((d, hid), const2),
                  pl.BlockSpec((hid, d), const2)],
        out_specs=[pl.BlockSpec((T, d), lambda i: (i, 0)),
                   pl.BlockSpec((T, d), lambda i: (i, 0)),
                   pl.BlockSpec((TOP_K, T), lambda i: (0, i)),
                   pl.BlockSpec((TOP_K, T), lambda i: (0, i)),
                   pl.BlockSpec((TOP_K, T), lambda i: (0, i)),
                   pl.BlockSpec((E, 128), const2)],
        out_shape=[jax.ShapeDtypeStruct((m, d), f32),
                   jax.ShapeDtypeStruct((m, d), f32),
                   jax.ShapeDtypeStruct((TOP_K, m), i32),
                   jax.ShapeDtypeStruct((TOP_K, m), f32),
                   jax.ShapeDtypeStruct((TOP_K, m), i32),
                   jax.ShapeDtypeStruct((E, 128), i32)],
        scratch_shapes=[pltpu.VMEM((E, 128), f32)],
        compiler_params=_cparams("arbitrary"),
        name="moe_route",
    )(h, g.reshape(1, d), mod3, rw_t, rbias.reshape(E, 1), wsg, wsu, wsd)
    return outs


def _gmm_kernel(be_ref, nused_ref, tokc_ref, tokn_ref, m_hbm, wg_ref, wu_ref, wd_ref,
                o_ref, xbuf, sem):
    i = pl.program_id(0)
    nused = nused_ref[0]
    slot = i % 2

    def start_gather(tok_ref, dst_slot):
        def body(r, carry):
            t = tok_ref[0, 0, r]
            pltpu.make_async_copy(m_hbm.at[pl.ds(t, 1), :],
                                  xbuf.at[dst_slot, pl.ds(r, 1), :],
                                  sem.at[dst_slot]).start()
            return carry
        lax.fori_loop(0, MOE_BLK, body, 0)

    @pl.when(jnp.logical_and(i == 0, nused > 0))
    def _():
        start_gather(tokc_ref, 0)

    @pl.when(i + 1 < nused)
    def _():
        start_gather(tokn_ref, 1 - slot)

    @pl.when(i < nused)
    def _():
        pltpu.make_async_copy(m_hbm.at[pl.ds(0, MOE_BLK), :], xbuf.at[slot],
                              sem.at[slot]).wait()
        xb = xbuf[slot].astype(bf16)
        hg = jnp.dot(xb, wg_ref[0], preferred_element_type=f32)
        hu = jnp.dot(xb, wu_ref[0], preferred_element_type=f32)
        act = (jax.nn.silu(hg) * hu).astype(bf16)
        o_ref[...] = jnp.dot(act, wd_ref[0], preferred_element_type=f32)

    @pl.when(i >= nused)
    def _():
        o_ref[...] = jnp.zeros_like(o_ref)


def moe_experts(m_rows, slot_tok, block_expert, n_used, wg, wu, wd):
    n_blocks = block_expert.shape[0]
    d = m_rows.shape[1]
    hid = wg.shape[2]
    tok3 = slot_tok.reshape(n_blocks, 1, MOE_BLK)
    smem_blk = lambda imap: pl.BlockSpec((1, 1, MOE_BLK), imap, memory_space=pltpu.SMEM)
    grid_spec = pltpu.PrefetchScalarGridSpec(
        num_scalar_prefetch=2,
        grid=(n_blocks,),
        in_specs=[smem_blk(lambda i, be, nu: (i, 0, 0)),
                  smem_blk(lambda i, be, nu: (jnp.minimum(i + 1, n_blocks - 1), 0, 0)),
                  pl.BlockSpec(memory_space=pl.ANY),
                  pl.BlockSpec((1, d, hid), lambda i, be, nu: (be[i], 0, 0)),
                  pl.BlockSpec((1, d, hid), lambda i, be, nu: (be[i], 0, 0)),
                  pl.BlockSpec((1, hid, d), lambda i, be, nu: (be[i], 0, 0))],
        out_specs=pl.BlockSpec((MOE_BLK, d), lambda i, be, nu: (i, 0)),
        scratch_shapes=[pltpu.VMEM((2, MOE_BLK, d), f32),
                        pltpu.SemaphoreType.DMA((2,))],
    )
    return pl.pallas_call(
        _gmm_kernel,
        grid_spec=grid_spec,
        out_shape=jax.ShapeDtypeStruct((n_blocks * MOE_BLK, d), f32),
        compiler_params=_cparams("arbitrary"),
        name="moe_experts",
    )(block_expert, n_used, tok3, tok3, m_rows, wg, wu, wd)


def _comb_kernel(dc_ref, dn_ref, ys_hbm, w_ref, sh_ref, h_ref, mod_ref, o_ref, gbuf, sem,
                 *, n_tiles):
    i = pl.program_id(0)
    slot = i % 2
    T = COMB_TILE

    def start_gather(d_ref, dst_slot):
        def body(r, carry):
            for k in range(TOP_K):
                t = d_ref[0, k, r]
                pltpu.make_async_copy(ys_hbm.at[pl.ds(t, 1), :],
                                      gbuf.at[dst_slot, k, pl.ds(r, 1), :],
                                      sem.at[dst_slot]).start()
            return carry
        lax.fori_loop(0, T, body, 0)

    @pl.when(i == 0)
    def _():
        start_gather(dc_ref, 0)

    @pl.when(i + 1 < n_tiles)
    def _():
        start_gather(dn_ref, 1 - slot)

    for k in range(TOP_K):
        pltpu.make_async_copy(ys_hbm.at[pl.ds(0, T), :], gbuf.at[slot, k],
                              sem.at[slot]).wait()
    w = w_ref[...]
    acc = gbuf[slot, 0] * w[:, 0:1]
    for k in range(1, TOP_K):
        acc = acc + gbuf[slot, k] * w[:, k:k + 1]
    o_ref[...] = h_ref[...] + mod_ref[0, 5:6, :] * (acc + sh_ref[...])


def moe_combine(y_sorted, dest_t, wts_t, shared, h, mod3, geom, lat_only):
    m, d = h.shape
    bsz, L, n_ctx = geom
    T = COMB_TILE
    n_tiles = m // T
    if lat_only:
        lpb = (L - n_ctx) // T
        mod_row = lambda i: i // lpb
    else:
        mod_row = _mod_row_map(geom, T)
    smem_blk = lambda imap: pl.BlockSpec((1, TOP_K, T), imap, memory_space=pltpu.SMEM)
    return pl.pallas_call(
        functools.partial(_comb_kernel, n_tiles=n_tiles),
        grid=(n_tiles,),
        in_specs=[smem_blk(lambda i: (i, 0, 0)),
                  smem_blk(lambda i: (jnp.minimum(i + 1, n_tiles - 1), 0, 0)),
                  pl.BlockSpec(memory_space=pl.ANY),
                  pl.BlockSpec((T, TOP_K), lambda i: (i, 0)),
                  pl.BlockSpec((T, d), lambda i: (i, 0)),
                  pl.BlockSpec((T, d), lambda i: (i, 0)),
                  pl.BlockSpec((1, 6, d), lambda i: (mod_row(i), 0, 0))],
        out_specs=pl.BlockSpec((T, d), lambda i: (i, 0)),
        out_shape=jax.ShapeDtypeStruct((m, d), f32),
        scratch_shapes=[pltpu.VMEM((2, TOP_K, T, d), f32),
                        pltpu.SemaphoreType.DMA((2,))],
        compiler_params=_cparams("arbitrary"),
        name="moe_combine",
    )(dest_t, dest_t, y_sorted, wts_t, shared, h, mod3)


def moe_layer(h, g2, mod3, layer, rw, rbias, wg, wu, wd, wsg, wsu, wsd, geom, lat_only):
    m, d = h.shape
    E = N_EXPERTS
    mt, shared, top_e, wts, pos, cnt = moe_route(
        h, g2, mod3, rw.T, rbias, wsg.astype(bf16), wsu.astype(bf16), wsd.astype(bf16),
        geom, lat_only)
    counts = cnt[:, 0]
    padded = ((counts + MOE_BLK - 1) // MOE_BLK) * MOE_BLK
    pad_end = jnp.cumsum(padded)
    pad_start = pad_end - padded
    dest = pad_start[top_e] + pos
    n_blocks = (m * TOP_K) // MOE_BLK + E
    tok = jnp.broadcast_to(jnp.arange(m, dtype=i32)[None, :], (TOP_K, m))
    slot_tok = jnp.zeros((n_blocks * MOE_BLK,), i32).at[dest.reshape(-1)].set(tok.reshape(-1))
    block_start = jnp.arange(n_blocks, dtype=i32) * MOE_BLK
    block_expert = jnp.minimum(jnp.searchsorted(pad_end, block_start, side='right'),
                               E - 1).astype(i32)
    n_used = (pad_end[-1:] // MOE_BLK).astype(i32)
    y_sorted = moe_experts(mt, slot_tok, block_expert, n_used,
                           wg.astype(bf16), wu.astype(bf16), wd.astype(bf16))
    T = COMB_TILE
    dest_t = dest.reshape(TOP_K, m // T, T).transpose(1, 0, 2)
    return moe_combine(y_sorted, dest_t, wts.T, shared, h, mod3, geom, lat_only)


def _rope_tables_t(n_ctx, n_lat):
    n_rows = n_lat // GRID_W
    rows = jnp.repeat(jnp.arange(n_rows, dtype=f32), GRID_W)
    cols = jnp.tile(jnp.arange(GRID_W, dtype=f32), n_rows)
    n_freq = ATT_HEADDIM // 4
    inv_freq = ROPE_BASE ** (-jnp.arange(n_freq, dtype=f32) / n_freq)
    ar = rows[None, :] * inv_freq[:, None]
    ac = cols[None, :] * inv_freq[:, None]
    cos_t = jnp.concatenate([jnp.cos(ar), jnp.cos(ar), jnp.cos(ac), jnp.cos(ac)], axis=0)
    sin_t = jnp.concatenate([-jnp.sin(ar), jnp.sin(ar), -jnp.sin(ac), jnp.sin(ac)], axis=0)
    cos_t = jnp.concatenate([jnp.ones((ATT_HEADDIM, n_ctx), f32), cos_t], axis=1)
    sin_t = jnp.concatenate([jnp.zeros((ATT_HEADDIM, n_ctx), f32), sin_t], axis=1)
    return cos_t, sin_t


def even_layer_mix(h, mod3, g1, w_in, gm_ws, gm_bs, gm_ln_g, gm_ln_b, conv_w, conv_b,
                   a_log, dt_bias, d_skip, ssd_norm_g, w_out, geom):
    a = norm_modulate(h, g1, mod3, geom, 0, 1)
    n_main = w_in.shape[1] - 2 * SSD_HEADS
    proj = matmul_nn(a, w_in[:, :n_main].astype(bf16), f32)
    w_dt = w_in[:, n_main:].reshape(-1, 2, SSD_GROUPS, SSD_HPG).transpose(2, 1, 3, 0)
    dt_t = dt_project(a, w_dt.reshape(2 * SSD_HEADS, -1))
    bexp = jnp.repeat(gm_bs.T, A_CHUNK, axis=1)
    g_out = gmlp_mix(proj, gm_ws.astype(bf16), bexp, gm_ln_g, gm_ln_b)
    y_out = ssd_mix(proj, dt_t, conv_w, conv_b, a_log, dt_bias, d_skip, ssd_norm_g, geom)
    aw = gm_ws.shape[0] * gm_ws.shape[1]
    w_o = w_out.astype(bf16)
    return out_project_joint([g_out, y_out], [w_o[:aw], w_o[aw:]], h, mod3, geom)


def odd_layer_mix_lat(h, mod3, g1, w_q, w_kv, q_norm_g, k_norm_g, sink, w_o, geom):
    bsz, L, n_ctx = geom
    a = norm_modulate(h, g1, mod3, geom, 0, 1)
    w_t = jnp.concatenate([w_q, w_kv], axis=1).T.astype(bf16)
    cos_t, sin_t = _rope_tables_t(n_ctx, L - n_ctx)
    qkv_t = qkv_project_t(a, w_t, q_norm_g, k_norm_g, cos_t, sin_t, geom)
    att = window_attention(qkv_t, sink, geom)
    return out_project_lat(att, w_o.astype(bf16), h, mod3, geom)


def kernel(x, c, ctx, c_ctx, mod_w, mod_b, norm1_g, norm2_g, ev_w_in, ev_gm_ws, ev_gm_bs, ev_gm_ln_g, ev_gm_ln_b, ev_conv_w, ev_conv_b, ev_a_log, ev_dt_bias, ev_d_skip, ev_ssd_norm_g, ev_w_out, od_w_q, od_w_kv, od_q_norm_g, od_k_norm_g, od_sink, od_w_o, moe_router_w, moe_router_bias, moe_w_gate, moe_w_up, moe_w_down, moe_ws_gate, moe_ws_up, moe_ws_down):
    bsz, n_lat, d = x.shape
    n_ctx = ctx.shape[1]
    L = n_ctx + n_lat
    geom = (bsz, L, n_ctx)
    depth = mod_w.shape[0]
    assert depth == 2 and n_ctx == ROW_TILE and n_lat % ROW_TILE == 0 and bsz < 16

    h = jnp.concatenate([ctx, x], axis=1).reshape(bsz * L, d)
    c_all = jnp.zeros((16, d), f32).at[:bsz].set(c).at[bsz].set(c_ctx)

    mod3 = modulation(c_all, mod_w, mod_b, 0).reshape(16, 6, d)
    h = even_layer_mix(h, mod3, norm1_g[0], ev_w_in[0], ev_gm_ws[0], ev_gm_bs[0], ev_gm_ln_g[0],
                       ev_gm_ln_b[0], ev_conv_w[0], ev_conv_b[0], ev_a_log[0], ev_dt_bias[0],
                       ev_d_skip[0], ev_ssd_norm_g[0], ev_w_out[0], geom)
    h = moe_layer(h, norm2_g[0], mod3, 0, moe_router_w[0], moe_router_bias[0], moe_w_gate[0],
                  moe_w_up[0], moe_w_down[0], moe_ws_gate[0], moe_ws_up[0], moe_ws_down[0],
                  geom, lat_only=False)

    mod3 = modulation(c_all, mod_w, mod_b, 1).reshape(16, 6, d)
    h_lat = odd_layer_mix_lat(h, mod3, norm1_g[1], od_w_q[0], od_w_kv[0], od_q_norm_g[0],
                              od_k_norm_g[0], od_sink[0], od_w_o[0], geom)
    h_lat = moe_layer(h_lat, norm2_g[1], mod3, 1, moe_router_w[1], moe_router_bias[1],
                      moe_w_gate[1], moe_w_up[1], moe_w_down[1], moe_ws_gate[1], moe_ws_up[1],
                      moe_ws_down[1], geom, lat_only=True)
    return h_lat.reshape(bsz, n_lat, d)
```

```python
import functools

import jax
import jax.numpy as jnp
from jax import lax
from jax.experimental import pallas as pl
from jax.experimental.pallas import tpu as pltpu

f32 = jnp.float32
bf16 = jnp.bfloat16
i32 = jnp.int32
HIGHEST = lax.Precision.HIGHEST

D_MODEL = 2048
EPS = 1e-6
GRID_W = 64

A_CHUNK = 128
A_GROUPS = 16

SSD_HEADS = 32
SSD_HEADDIM = 64
SSD_GROUPS = 8
SSD_STATE = 128
SSD_CHUNK = 128
SSD_HPG = SSD_HEADS // SSD_GROUPS
SSD_GW = SSD_HPG * SSD_HEADDIM

ATT_HEADDIM = 64
ATT_HEADS = 32
ATT_KV_HEADS = 4
ATT_Q_PER_KV = 8
ATT_WINDOW = 128
ATT_BLOCK = 128
ATT_SCALE = ATT_HEADDIM ** -0.5
ROPE_BASE = 10000.0

N_EXPERTS = 64
N_EXPERT_GROUPS = 8
GROUP_SIZE = N_EXPERTS // N_EXPERT_GROUPS
TOP_GROUPS = 4
TOP_K = 8
EXPERT_HIDDEN = 512
ROUTED_SCALE = 2.5

ROW_TILE = 256
BIG_TILE = 768
MOE_BLK = 256
COMB_TILE = 64
VMEM_LIMIT = 56 * 1024 * 1024

_NT = (((1,), (1,)), ((), ()))
_TN = (((0,), (0,)), ((), ()))


def _cparams(*sem):
    return pltpu.CompilerParams(dimension_semantics=sem, vmem_limit_bytes=VMEM_LIMIT)


def _pick_tile(m, options):
    for t in options:
        if m % t == 0:
            return t
    raise ValueError(f"no tile for {m}")


def _mod_kernel(c_ref, w_ref, b_ref, o_ref):
    sc = jax.nn.silu(c_ref[...])
    o_ref[...] = jnp.dot(sc, w_ref[...], precision=HIGHEST,
                         preferred_element_type=f32) + b_ref[...]


def modulation(c_all, mod_w, mod_b, layer):
    rows, d = c_all.shape
    n = mod_w.shape[2]
    tn = 1024
    return pl.pallas_call(
        _mod_kernel,
        grid=(n // tn,),
        in_specs=[pl.BlockSpec((rows, d), lambda j: (0, 0)),
                  pl.BlockSpec((None, d, tn), lambda j: (layer, 0, j)),
                  pl.BlockSpec((None, 1, tn), lambda j: (layer, 0, j))],
        out_specs=pl.BlockSpec((rows, tn), lambda j: (0, j)),
        out_shape=jax.ShapeDtypeStruct((rows, n), f32),
        compiler_params=_cparams("arbitrary"),
        name="modulation",
    )(c_all, mod_w, mod_b.reshape(mod_b.shape[0], 1, n))


def _mod_row_map(geom, tile):
    bsz, L, n_ctx = geom
    tpb = L // tile
    nct = n_ctx // tile

    def mod_row(i):
        return jnp.where((i % tpb) < nct, bsz, i // tpb)
    return mod_row


def _lat_tile_map(geom, tile):
    bsz, L, n_ctx = geom
    lpb = (L - n_ctx) // tile
    tpb = L // tile
    nct = n_ctx // tile

    def joint(t):
        return (t // lpb) * tpb + nct + (t % lpb)
    return joint


def _norm_mod_kernel(x_ref, g_ref, mod_ref, o_ref, *, shift_idx, scale_idx):
    x = x_ref[...]
    ms = jnp.mean(x * x, axis=-1, keepdims=True)
    y = x * lax.rsqrt(ms + EPS) * g_ref[...]
    y = y * (1.0 + mod_ref[0, scale_idx:scale_idx + 1, :]) + mod_ref[0, shift_idx:shift_idx + 1, :]
    o_ref[...] = y.astype(o_ref.dtype)


def norm_modulate(h, g, mod3, geom, shift_idx, scale_idx):
    m, d = h.shape
    mod_row = _mod_row_map(geom, ROW_TILE)
    return pl.pallas_call(
        functools.partial(_norm_mod_kernel, shift_idx=shift_idx, scale_idx=scale_idx),
        grid=(m // ROW_TILE,),
        in_specs=[pl.BlockSpec((ROW_TILE, d), lambda i: (i, 0)),
                  pl.BlockSpec((1, d), lambda i: (0, 0)),
                  pl.BlockSpec((1, 6, d), lambda i: (mod_row(i), 0, 0))],
        out_specs=pl.BlockSpec((ROW_TILE, d), lambda i: (i, 0)),
        out_shape=jax.ShapeDtypeStruct((m, d), bf16),
        compiler_params=_cparams("parallel"),
        name="norm_modulate",
    )(h, g.reshape(1, d), mod3)


def _mm_kernel(a_ref, w_ref, o_ref):
    o_ref[...] = jnp.dot(a_ref[...], w_ref[...],
                         preferred_element_type=f32).astype(o_ref.dtype)


def matmul_nn(a, w, out_dtype, tn=1024):
    m, k = a.shape
    n = w.shape[1]
    tm = _pick_tile(m, (1024, 768, 512, 256))
    return pl.pallas_call(
        _mm_kernel,
        grid=(n // tn, m // tm),
        in_specs=[pl.BlockSpec((tm, k), lambda j, i: (i, 0)),
                  pl.BlockSpec((k, tn), lambda j, i: (0, j))],
        out_specs=pl.BlockSpec((tm, tn), lambda j, i: (i, j)),
        out_shape=jax.ShapeDtypeStruct((m, n), out_dtype),
        compiler_params=_cparams("parallel", "parallel"),
        name="matmul_nn",
    )(a, w)


def _dt_kernel(a_ref, w_ref, o_ref):
    o_ref[...] = lax.dot_general(w_ref[...], a_ref[...].astype(f32), _NT,
                                 precision=HIGHEST, preferred_element_type=f32)


def dt_project(a, w_dt_t):
    m, k = a.shape
    r = w_dt_t.shape[0]
    tm = _pick_tile(m, (1024, 768, 512, 256))
    return pl.pallas_call(
        _dt_kernel,
        grid=(m // tm,),
        in_specs=[pl.BlockSpec((tm, k), lambda i: (i, 0)),
                  pl.BlockSpec((r, k), lambda i: (0, 0))],
        out_specs=pl.BlockSpec((r, tm), lambda i: (0, i)),
        out_shape=jax.ShapeDtypeStruct((r, m), f32),
        compiler_params=_cparams("parallel"),
        name="dt_project",
    )(a, w_dt_t)


def _gelu_exact(x):
    return 0.5 * x * (1.0 + lax.erf(x * (2.0 ** -0.5)))


def _gmlp_kernel(u_ref, v_ref, ws_ref, bexp_ref, lng_ref, lnb_ref, o_ref):
    u = _gelu_exact(u_ref[...])
    v = _gelu_exact(v_ref[...])
    mu = jnp.mean(v, axis=-1, keepdims=True)
    vc = v - mu
    var = jnp.mean(vc * vc, axis=-1, keepdims=True)
    vn = (vc * lax.rsqrt(var + EPS) * lng_ref[...] + lnb_ref[...]).astype(bf16)
    for g in range(A_GROUPS):
        sl = slice(g * A_CHUNK, (g + 1) * A_CHUNK)
        mixed = jnp.dot(ws_ref[g], vn[:, sl], preferred_element_type=f32) + bexp_ref[:, sl]
        o_ref[:, sl] = (u[:, sl] * mixed).astype(o_ref.dtype)


def gmlp_mix(proj, ws_bf, bexp, ln_g, ln_b):
    m = proj.shape[0]
    w = A_GROUPS * A_CHUNK
    return pl.pallas_call(
        _gmlp_kernel,
        grid=(m // A_CHUNK,),
        in_specs=[pl.BlockSpec((A_CHUNK, w), lambda c: (c, 0)),
                  pl.BlockSpec((A_CHUNK, w), lambda c: (c, 1)),
                  pl.BlockSpec((A_GROUPS, A_CHUNK, A_CHUNK), lambda c: (0, 0, 0)),
                  pl.BlockSpec((A_CHUNK, w), lambda c: (0, 0)),
                  pl.BlockSpec((1, w), lambda c: (0, 0)),
                  pl.BlockSpec((1, w), lambda c: (0, 0))],
        out_specs=pl.BlockSpec((A_CHUNK, w), lambda c: (c, 0)),
        out_shape=jax.ShapeDtypeStruct((m, w), bf16),
        compiler_params=_cparams("parallel"),
        name="gmlp_mix",
    )(proj, proj, ws_bf, bexp, ln_g.reshape(1, w), ln_b.reshape(1, w))


def _head_expand(v, lane):
    out = v[:, 3:4]
    for r in (2, 1, 0):
        out = jnp.where(lane < SSD_HEADDIM * (r + 1), v[:, r:r + 1], out)
    return out


def _row_expand(v, row):
    out = v[:, 3:4]
    for r in (2, 1, 0):
        out = jnp.where(row < SSD_HEADDIM * (r + 1), v[:, r:r + 1], out)
    return out


def _ssd_kernel(x_ref, b_ref, c_ref, z_ref, dtr_ref, dtc_ref,
                cwx_ref, cwb_ref, cwc_ref, cbx_ref, cbb_ref, cbc_ref,
                alr_ref, alc_ref, dbr_ref, dbc_ref, dsk_ref, ng_ref,
                o_ref,
                xs_s, bs_s, cs_s, dtr_s, dtar_s, dtc_s, dtac_s, suf_s, yacc_s, sf_s, sb_s,
                *, n_ctx, L):
    Q = SSD_CHUNK
    nc = L // Q
    ncc = n_ctx // Q
    H = SSD_HPG

    bias_r = dbr_ref[...]
    a_r = -jnp.exp(alr_ref[...])
    for c in range(nc):
        dt = jax.nn.softplus(dtr_ref[:, c * Q:(c + 1) * Q] + bias_r)
        dtr_s[c] = dt
        dtar_s[c] = dt * a_r
    dtc = jax.nn.softplus(dtc_ref[0] + dbc_ref[0])
    dtc_s[...] = dtc
    dtac_s[...] = dtc * (-jnp.exp(alc_ref[0]))

    ii = lax.broadcasted_iota(i32, (Q, Q), 0)
    jj = lax.broadcasted_iota(i32, (Q, Q), 1)
    lower = ii >= jj
    upper = ii <= jj
    tril = lower.astype(f32)
    triu = upper.astype(f32)
    lane = lax.broadcasted_iota(i32, (Q, SSD_GW), 1)
    row = lax.broadcasted_iota(i32, (SSD_GW, SSD_STATE), 0)

    def conv_act(ref, w_ref, bias_ref, s):
        xv = ref[pl.ds(s, Q), :]
        w = xv.shape[1]
        r0 = lax.broadcasted_iota(i32, (Q, w), 0)
        prev = ref[pl.ds(jnp.maximum(s - 1, 0), 1), :]
        nxt = ref[pl.ds(jnp.minimum(s + Q, L - 1), 1), :]
        has_prev = jnp.logical_and(s != 0, s != n_ctx)
        has_next = jnp.logical_and(s + Q != n_ctx, s + Q != L)
        prev = jnp.where(has_prev, prev, 0.0)
        nxt = jnp.where(has_next, nxt, 0.0)
        xm1 = jnp.where(r0 == 0, prev, pltpu.roll(xv, 1, 0))
        xp1 = jnp.where(r0 == Q - 1, nxt, pltpu.roll(xv, Q - 1, 0))
        y = w_ref[0:1, :] * xm1 + w_ref[1:2, :] * xv + w_ref[2:3, :] * xp1 + bias_ref[...]
        return jax.nn.silu(y)

    sf_s[...] = jnp.zeros_like(sf_s)
    sb_s[...] = jnp.zeros_like(sb_s)

    def fwd_body(c, carry):
        s = pl.multiple_of(c * Q, Q)
        X = conv_act(x_ref, cwx_ref, cbx_ref, s)
        Bc = conv_act(b_ref, cwb_ref, cbb_ref, s)
        Cc = conv_act(c_ref, cwc_ref, cbc_ref, s)
        xs_s[pl.ds(s, Q), :] = X
        bs_s[pl.ds(s, Q), :] = Bc
        cs_s[pl.ds(s, Q), :] = Cc
        Bb = Bc.astype(bf16)
        Cb = Cc.astype(bf16)
        cb = lax.dot_general(Cb, Bb, _NT, preferred_element_type=f32)
        dta_c = dtac_s[pl.ds(s, Q), :]
        dta_r = dtar_s[c]
        dt_r = dtr_s[c]
        p_col = jnp.dot(tril, dta_c, precision=HIGHEST, preferred_element_type=f32)
        s_col = jnp.dot(triu, dta_c, precision=HIGHEST, preferred_element_type=f32)
        p_row = jnp.dot(dta_r, triu, precision=HIGHEST, preferred_element_type=f32)
        s_row = jnp.dot(dta_r, tril, precision=HIGHEST, preferred_element_type=f32)
        suf_s[pl.ds(s, Q), :] = s_col
        y = jnp.zeros((Q, SSD_GW), f32)
        for r in range(H):
            lf = jnp.exp(jnp.where(lower, p_col[:, r:r + 1] - p_row[r:r + 1, :], -jnp.inf))
            lb = jnp.exp(jnp.where(upper, s_col[:, H + r:H + r + 1] - s_row[H + r:H + r + 1, :],
                                   -jnp.inf))
            mr = cb * (lf * dt_r[r:r + 1, :] + lb * dt_r[H + r:H + r + 1, :])
            head = jnp.logical_and(lane >= SSD_HEADDIM * r, lane < SSD_HEADDIM * (r + 1))
            xm = jnp.where(head, X, 0.0).astype(bf16)
            y = y + jnp.dot(mr.astype(bf16), xm, preferred_element_type=f32)
        sf = sf_s[...]
        yi = lax.dot_general(Cb, sf.astype(bf16), _NT, preferred_element_type=f32)
        pf = p_col[:, 0:H]
        y = y + yi * _head_expand(jnp.exp(pf), lane)
        yacc_s[pl.ds(s, Q), :] = y
        tot = p_col[Q - 1:Q, 0:H]
        wf = dtc_s[pl.ds(s, Q), 0:H] * jnp.exp(tot - pf)
        xw = (X * _head_expand(wf, lane)).astype(bf16)
        upd = lax.dot_general(xw, Bb, _TN, preferred_element_type=f32)
        sf_s[...] = sf * _row_expand(jnp.exp(tot), row) + upd
        return carry

    lax.fori_loop(0, nc, fwd_body, 0)

    def bwd_body(k, carry):
        c = jnp.where(k < ncc, ncc - 1 - k, nc - 1 - (k - ncc))
        s = pl.multiple_of(c * Q, Q)
        X = xs_s[pl.ds(s, Q), :]
        Bb = bs_s[pl.ds(s, Q), :].astype(bf16)
        Cb = cs_s[pl.ds(s, Q), :].astype(bf16)
        ab = suf_s[pl.ds(s, Q), H:2 * H]
        tot = ab[0:1, :]
        sb = sb_s[...]
        yi = lax.dot_general(Cb, sb.astype(bf16), _NT, preferred_element_type=f32)
        y = yacc_s[pl.ds(s, Q), :] + yi * _head_expand(jnp.exp(ab), lane) + dsk_ref[...] * X
        wb = dtc_s[pl.ds(s, Q), H:2 * H] * jnp.exp(tot - ab)
        xw = (X * _head_expand(wb, lane)).astype(bf16)
        upd = lax.dot_general(xw, Bb, _TN, preferred_element_type=f32)
        sb_s[...] = sb * _row_expand(jnp.exp(tot), row) + upd
        y = y * jax.nn.silu(z_ref[pl.ds(s, Q), :])
        ms = jnp.mean(y * y, axis=-1, keepdims=True)
        o_ref[pl.ds(s, Q), :] = (y * lax.rsqrt(ms + EPS) * ng_ref[...]).astype(o_ref.dtype)
        return carry

    lax.fori_loop(0, nc, bwd_body, 0)


def ssd_mix(proj, dt_t, conv_w, conv_b, a_log, dt_bias, d_skip, norm_g, geom):
    bsz, L, n_ctx = geom
    m = proj.shape[0]
    G, H, GW, N = SSD_GROUPS, SSD_HPG, SSD_GW, SSD_STATE
    nc = L // SSD_CHUNK
    col0 = proj.shape[1] - (SSD_HEADS * SSD_HEADDIM + 2 * G * N)
    zc0 = col0 - SSD_HEADS * SSD_HEADDIM
    xb, zb = col0 // GW, zc0 // GW
    bb = (col0 + SSD_HEADS * SSD_HEADDIM) // N
    cb = bb + G
    cxb, cbb, ccb = 0, (SSD_HEADS * SSD_HEADDIM) // N, (SSD_HEADS * SSD_HEADDIM) // N + G

    def to_gdr(p):
        return p.reshape(2, G, H).transpose(1, 0, 2).reshape(G, 2 * H)

    al, db = to_gdr(a_log), to_gdr(dt_bias)
    dt_col = dt_t.reshape(G, 2 * H, m).transpose(0, 2, 1)
    dsk = jnp.repeat(d_skip, SSD_HEADDIM).reshape(1, -1)
    conv_b2 = conv_b.reshape(1, -1)
    kern = functools.partial(_ssd_kernel, n_ctx=n_ctx, L=L)
    return pl.pallas_call(
        kern,
        grid=(bsz, G),
        in_specs=[
            pl.BlockSpec((L, GW), lambda b, g: (b, xb + g)),
            pl.BlockSpec((L, N), lambda b, g: (b, bb + g)),
            pl.BlockSpec((L, N), lambda b, g: (b, cb + g)),
            pl.BlockSpec((L, GW), lambda b, g: (b, zb + g)),
            pl.BlockSpec((2 * H, L), lambda b, g: (g, b)),
            pl.BlockSpec((1, L, 2 * H), lambda b, g: (g, b, 0)),
            pl.BlockSpec((3, GW), lambda b, g: (0, cxb + g)),
            pl.BlockSpec((3, N), lambda b, g: (0, cbb + g)),
            pl.BlockSpec((3, N), lambda b, g: (0, ccb + g)),
            pl.BlockSpec((1, GW), lambda b, g: (0, cxb + g)),
            pl.BlockSpec((1, N), lambda b, g: (0, cbb + g)),
            pl.BlockSpec((1, N), lambda b, g: (0, ccb + g)),
            pl.BlockSpec((2 * H, 1), lambda b, g: (g, 0)),
            pl.BlockSpec((1, 1, 2 * H), lambda b, g: (g, 0, 0)),
            pl.BlockSpec((2 * H, 1), lambda b, g: (g, 0)),
            pl.BlockSpec((1, 1, 2 * H), lambda b, g: (g, 0, 0)),
            pl.BlockSpec((1, GW), lambda b, g: (0, g)),
            pl.BlockSpec((1, GW), lambda b, g: (0, g)),
        ],
        out_specs=pl.BlockSpec((L, GW), lambda b, g: (b, g)),
        out_shape=jax.ShapeDtypeStruct((m, G * GW), bf16),
        scratch_shapes=[
            pltpu.VMEM((L, GW), f32), pltpu.VMEM((L, N), f32), pltpu.VMEM((L, N), f32),
            pltpu.VMEM((nc, 2 * H, SSD_CHUNK), f32), pltpu.VMEM((nc, 2 * H, SSD_CHUNK), f32),
            pltpu.VMEM((L, 2 * H), f32), pltpu.VMEM((L, 2 * H), f32), pltpu.VMEM((L, 2 * H), f32),
            pltpu.VMEM((L, GW), f32),
            pltpu.VMEM((GW, N), f32), pltpu.VMEM((GW, N), f32),
        ],
        compiler_params=_cparams("parallel", "parallel"),
        name="ssd_mix",
    )(proj, proj, proj, proj, dt_t, dt_col,
      conv_w, conv_w, conv_w, conv_b2, conv_b2, conv_b2,
      al.reshape(G * 2 * H, 1), al.reshape(G, 1, 2 * H),
      db.reshape(G * 2 * H, 1), db.reshape(G, 1, 2 * H),
      dsk, norm_g.reshape(1, -1))


def _outproj_kernel(*refs, n_in, n_ctx, tpb, tm):
    a_refs = refs[:n_in]
    w_refs = refs[n_in:2 * n_in]
    h_ref, ml_ref, mc_ref, o_ref = refs[2 * n_in:]
    i = pl.program_id(1)
    acc = jnp.dot(a_refs[0][...], w_refs[0][...], preferred_element_type=f32)
    for k in range(1, n_in):
        acc = acc + jnp.dot(a_refs[k][...], w_refs[k][...], preferred_element_type=f32)
    n_ctx_rows = jnp.clip(n_ctx - (i % tpb) * tm, 0, tm)
    row = lax.broadcasted_iota(i32, acc.shape, 0)
    gate = jnp.where(row < n_ctx_rows, mc_ref[0, 2:3, :], ml_ref[0, 2:3, :])
    o_ref[...] = h_ref[...] + gate * acc


def out_project_joint(a_list, w_list, h, mod3, geom, tn=1024):
    bsz, L, n_ctx = geom
    m, d = h.shape
    tm = _pick_tile(L, (BIG_TILE, 512, 256))
    tpb = L // tm
    n_in = len(a_list)
    k = a_list[0].shape[1]
    kern = functools.partial(_outproj_kernel, n_in=n_in, n_ctx=n_ctx, tpb=tpb, tm=tm)
    in_specs = ([pl.BlockSpec((tm, k), lambda j, i: (i, 0)) for _ in range(n_in)]
                + [pl.BlockSpec((k, tn), lambda j, i: (0, j)) for _ in range(n_in)]
                + [pl.BlockSpec((tm, tn), lambda j, i: (i, j)),
                   pl.BlockSpec((1, 6, tn), lambda j, i: (i // tpb, 0, j)),
                   pl.BlockSpec((1, 6, tn), lambda j, i: (bsz, 0, j))])
    return pl.pallas_call(
        kern,
        grid=(d // tn, m // tm),
        in_specs=in_specs,
        out_specs=pl.BlockSpec((tm, tn), lambda j, i: (i, j)),
        out_shape=jax.ShapeDtypeStruct((m, d), f32),
        compiler_params=_cparams("parallel", "parallel"),
        name="out_project_joint",
    )(*a_list, *w_list, h, mod3, mod3)


def _outproj_lat_kernel(a_ref, w_ref, h_ref, ml_ref, o_ref):
    acc = jnp.dot(a_ref[...], w_ref[...], preferred_element_type=f32)
    o_ref[...] = h_ref[...] + ml_ref[0, 2:3, :] * acc


def out_project_lat(a, w, h_joint, mod3, geom, tn=1024):
    bsz, L, n_ctx = geom
    m, k = a.shape
    d = h_joint.shape[1]
    tm = ROW_TILE
    joint = _lat_tile_map(geom, tm)
    lpb = (L - n_ctx) // tm
    return pl.pallas_call(
        _outproj_lat_kernel,
        grid=(d // tn, m // tm),
        in_specs=[pl.BlockSpec((tm, k), lambda j, i: (i, 0)),
                  pl.BlockSpec((k, tn), lambda j, i: (0, j)),
                  pl.BlockSpec((tm, tn), lambda j, i: (joint(i), j)),
                  pl.BlockSpec((1, 6, tn), lambda j, i: (i // lpb, 0, j))],
        out_specs=pl.BlockSpec((tm, tn), lambda j, i: (i, j)),
        out_shape=jax.ShapeDtypeStruct((m, d), f32),
        compiler_params=_cparams("parallel", "parallel"),
        name="out_project_lat",
    )(a, w, h_joint, mod3)


def _qkv_kernel(a_ref, w_ref, gq_ref, gk_ref, cos_ref, sin_ref, o_ref, *, n_qblk):
    j = pl.program_id(1)
    acc = lax.dot_general(w_ref[...], a_ref[...], _NT, preferred_element_type=f32)

    @pl.when(j <= n_qblk)
    def _():
        g = jnp.where(j < n_qblk, gq_ref[...] * ATT_SCALE, gk_ref[...])
        cs = cos_ref[...]
        sn = sin_ref[...]
        hd = ATT_HEADDIM
        q4 = hd // 4
        for hh in range(acc.shape[0] // hd):
            blk = acc[hh * hd:(hh + 1) * hd, :]
            ms = jnp.mean(blk * blk, axis=0, keepdims=True)
            xn = blk * lax.rsqrt(ms + EPS) * g
            sw = jnp.concatenate([xn[q4:2 * q4], xn[0:q4], xn[3 * q4:4 * q4], xn[2 * q4:3 * q4]],
                                 axis=0)
            o_ref[hh * hd:(hh + 1) * hd, :] = xn * cs + sw * sn

    @pl.when(j > n_qblk)
    def _():
        o_ref[...] = acc


def qkv_project_t(a, w_t, gq, gk, cos_t, sin_t, geom):
    bsz, L, n_ctx = geom
    m, k = a.shape
    r = w_t.shape[0]
    rb = 256
    tm = _pick_tile(L, (BIG_TILE, 512, 256))
    tpb = L // tm
    n_qblk = (ATT_HEADS * ATT_HEADDIM) // rb
    return pl.pallas_call(
        functools.partial(_qkv_kernel, n_qblk=n_qblk),
        grid=(m // tm, r // rb),
        in_specs=[pl.BlockSpec((tm, k), lambda i, j: (i, 0)),
                  pl.BlockSpec((rb, k), lambda i, j: (j, 0)),
                  pl.BlockSpec((ATT_HEADDIM, 1), lambda i, j: (0, 0)),
                  pl.BlockSpec((ATT_HEADDIM, 1), lambda i, j: (0, 0)),
                  pl.BlockSpec((ATT_HEADDIM, tm), lambda i, j: (0, i % tpb)),
                  pl.BlockSpec((ATT_HEADDIM, tm), lambda i, j: (0, i % tpb))],
        out_specs=pl.BlockSpec((rb, tm), lambda i, j: (j, i)),
        out_shape=jax.ShapeDtypeStruct((r, m), f32),
        compiler_params=_cparams("parallel", "parallel"),
        name="qkv_project_t",
    )(a, w_t, gq.reshape(-1, 1), gk.reshape(-1, 1), cos_t, sin_t)


def _attn_kernel(sink_ref, q_ref, kc_ref, k0_ref, k1_ref, k2_ref,
                 vc_ref, v0_ref, v1_ref, v2_ref, o_ref, ot_s, *, n_lat, n_ctx):
    qi = pl.program_id(1)
    kvh = pl.program_id(2)
    blk = ATT_BLOCK
    hd = ATT_HEADDIM
    kt = jnp.concatenate([kc_ref[...], k0_ref[...], k1_ref[...], k2_ref[...]], axis=1)
    vt = jnp.concatenate([vc_ref[...], v0_ref[...], v1_ref[...], v2_ref[...]], axis=1)
    kt = kt.astype(bf16)
    vt = vt.astype(bf16)
    nk = n_ctx + 3 * blk
    krow = lax.broadcasted_iota(i32, (nk, blk), 0)
    qcol = lax.broadcasted_iota(i32, (nk, blk), 1)
    kpos = (qi - 1) * blk + (krow - n_ctx)
    qpos = qi * blk + qcol
    band_ok = jnp.logical_and(jnp.abs(kpos - qpos) <= ATT_WINDOW,
                              jnp.logical_and(kpos >= 0, kpos < n_lat))
    valid = jnp.logical_or(krow < n_ctx, band_ok)
    for r in range(ATT_Q_PER_KV):
        qt = q_ref[r * hd:(r + 1) * hd, :].astype(bf16)
        s = lax.dot_general(kt, qt, _TN, preferred_element_type=f32)
        s = jnp.where(valid, s, -jnp.inf)
        sink = sink_ref[kvh * ATT_Q_PER_KV + r]
        mx = jnp.maximum(jnp.max(s, axis=0, keepdims=True), sink)
        p = jnp.exp(s - mx)
        den = jnp.sum(p, axis=0, keepdims=True) + jnp.exp(sink - mx)
        o = jnp.dot(vt, p.astype(bf16), preferred_element_type=f32)
        ot_s[r * hd:(r + 1) * hd, :] = o / den
    o_ref[...] = ot_s[...].T.astype(o_ref.dtype)


def window_attention(qkv_t, sink, geom):
    bsz, L, n_ctx = geom
    n_lat = L - n_ctx
    blk = ATT_BLOCK
    hd = ATT_HEADDIM
    nqb = n_lat // blk
    cpb = L // blk
    cb0 = n_ctx // blk
    q_rows = ATT_Q_PER_KV * hd
    k_rb = (ATT_HEADS * hd) // hd
    v_rb = k_rb + ATT_KV_HEADS

    def band(rb0, off):
        def imap(b, qi, kvh, sink_ref):
            kb = jnp.clip(qi + off, 0, nqb - 1)
            return (rb0 + kvh, b * cpb + cb0 + kb)
        return pl.BlockSpec((hd, blk), imap)

    def ctx(rb0):
        return pl.BlockSpec((hd, n_ctx), lambda b, qi, kvh, s: (rb0 + kvh, b * (L // n_ctx)))

    kern = functools.partial(_attn_kernel, n_lat=n_lat, n_ctx=n_ctx)
    grid_spec = pltpu.PrefetchScalarGridSpec(
        num_scalar_prefetch=1,
        grid=(bsz, nqb, ATT_KV_HEADS),
        in_specs=[pl.BlockSpec((q_rows, blk), lambda b, qi, kvh, s: (kvh, b * cpb + cb0 + qi)),
                  ctx(k_rb), band(k_rb, -1), band(k_rb, 0), band(k_rb, 1),
                  ctx(v_rb), band(v_rb, -1), band(v_rb, 0), band(v_rb, 1)],
        out_specs=pl.BlockSpec((blk, q_rows), lambda b, qi, kvh, s: (b * nqb + qi, kvh)),
        scratch_shapes=[pltpu.VMEM((q_rows, blk), f32)],
    )
    return pl.pallas_call(
        kern,
        grid_spec=grid_spec,
        out_shape=jax.ShapeDtypeStruct((bsz * n_lat, ATT_HEADS * hd), bf16),
        compiler_params=_cparams("parallel", "parallel", "parallel"),
        name="window_attention",
    )(sink, *([qkv_t] * 9))


def _first_max(v, idx, n):
    m = jnp.max(v, axis=0, keepdims=True)
    first = jnp.min(jnp.where(v == m, idx, n), axis=0, keepdims=True)
    return m, first


def _route_kernel(h_ref, g_ref, mod_ref, rw_ref, rb_ref, wsg_ref, wsu_ref, wsd_ref,
                  m_ref, sh_ref, te_ref, wt_ref, pos_ref, cnt_ref, carry_s):
    i = pl.program_id(0)
    T = h_ref.shape[0]
    E, NG, GS = N_EXPERTS, N_EXPERT_GROUPS, GROUP_SIZE

    @pl.when(i == 0)
    def _():
        carry_s[...] = jnp.zeros_like(carry_s)

    x = h_ref[...]
    ms = jnp.mean(x * x, axis=-1, keepdims=True)
    y = x * lax.rsqrt(ms + EPS) * g_ref[...]
    mt = y * (1.0 + mod_ref[0, 4:5, :]) + mod_ref[0, 3:4, :]
    m_ref[...] = mt
    mb = mt.astype(bf16)

    hg = jnp.dot(mb, wsg_ref[...], preferred_element_type=f32)
    hu = jnp.dot(mb, wsu_ref[...], preferred_element_type=f32)
    act = (jax.nn.silu(hg) * hu).astype(bf16)
    sh_ref[...] = jnp.dot(act, wsd_ref[...], preferred_element_type=f32)

    logits = lax.dot_general(rw_ref[...], mt, _NT, precision=HIGHEST,
                             preferred_element_type=f32)
    scores = jax.nn.sigmoid(logits)
    sel = scores + rb_ref[...]
    eidx = lax.broadcasted_iota(i32, (E, T), 0)
    midx = lax.broadcasted_iota(i32, (GS, T), 0)
    gidx = lax.broadcasted_iota(i32, (NG, T), 0)

    gs = jnp.zeros((NG, T), f32)
    for g in range(NG):
        blk = sel[g * GS:(g + 1) * GS, :]
        m1, f1 = _first_max(blk, midx, GS)
        m2 = jnp.max(jnp.where(midx == f1, -jnp.inf, blk), axis=0, keepdims=True)
        gs = jnp.where(gidx == g, m1 + m2, gs)
    gmask = jnp.zeros((NG, T), jnp.bool_)
    work = gs
    for _ in range(TOP_GROUPS):
        _, f = _first_max(work, gidx, NG)
        hit = gidx == f
        gmask = jnp.logical_or(gmask, hit)
        work = jnp.where(hit, -jnp.inf, work)
    cand = jnp.concatenate(
        [jnp.where(gmask[g:g + 1, :], sel[g * GS:(g + 1) * GS, :], -jnp.inf) for g in range(NG)],
        axis=0)
    hits = []
    chosen = jnp.zeros((E, T), jnp.bool_)
    work = cand
    for k in range(TOP_K):
        _, f = _first_max(work, eidx, E)
        hit = eidx == f
        hits.append(hit)
        chosen = jnp.logical_or(chosen, hit)
        work = jnp.where(hit, -jnp.inf, work)
        te_ref[k:k + 1, :] = f
    wts = [jnp.sum(jnp.where(h, scores, 0.0), axis=0, keepdims=True) for h in hits]
    tot = wts[0]
    for w in wts[1:]:
        tot = tot + w
    for k in range(TOP_K):
        wt_ref[k:k + 1, :] = wts[k] / (tot + 1e-20) * ROUTED_SCALE

    t0 = lax.broadcasted_iota(i32, (T, T), 0)
    t1 = lax.broadcasted_iota(i32, (T, T), 1)
    before = (t0 < t1).astype(bf16)
    chosen_f = chosen.astype(f32)
    cnt = jnp.dot(chosen_f.astype(bf16), before, preferred_element_type=f32) + carry_s[:, 0:1]
    for k in range(TOP_K):
        pos_ref[k:k + 1, :] = jnp.sum(jnp.where(hits[k], cnt, 0.0), axis=0,
                                      keepdims=True).astype(i32)
    new_carry = carry_s[...] + jnp.sum(chosen_f, axis=1, keepdims=True)
    carry_s[...] = new_carry
    cnt_ref[...] = new_carry.astype(i32)


def moe_route(h, g, mod3, rw_t, rbias, wsg, wsu, wsd, geom, lat_only):
    m, d = h.shape
    bsz, L, n_ctx = geom
    T = ROW_TILE
    if lat_only:
        lpb = (L - n_ctx) // T
        mod_row = lambda i: i // lpb
    else:
        mod_row = _mod_row_map(geom, T)
    E = N_EXPERTS
    hid = wsg.shape[1]
    const2 = lambda i: (0, 0)
    outs = pl.pallas_call(
        _route_kernel,
        grid=(m // T,),
        in_specs=[pl.BlockSpec((T, d), lambda i: (i, 0)),
                  pl.BlockSpec((1, d), const2),
                  pl.BlockSpec((1, 6, d), lambda i: (mod_row(i), 0, 0)),
                  pl.BlockSpec((E, d), const2),
                  pl.BlockSpec((E, 1), const2),
                  pl.BlockSpec((d, hid), const2),
                  pl.BlockSpec((d, hid), const2),
                  pl.BlockSpec((hid, d), const2)],
        out_specs=[pl.BlockSpec((T, d), lambda i: (i, 0)),
                   pl.BlockSpec((T, d), lambda i: (i, 0)),
                   pl.BlockSpec((TOP_K, T), lambda i: (0, i)),
                   pl.BlockSpec((TOP_K, T), lambda i: (0, i)),
                   pl.BlockSpec((TOP_K, T), lambda i: (0, i)),
                   pl.BlockSpec((E, 128), const2)],
        out_shape=[jax.ShapeDtypeStruct((m, d), f32),
                   jax.ShapeDtypeStruct((m, d), f32),
                   jax.ShapeDtypeStruct((TOP_K, m), i32),
                   jax.ShapeDtypeStruct((TOP_K, m), f32),
                   jax.ShapeDtypeStruct((TOP_K, m), i32),
                   jax.ShapeDtypeStruct((E, 128), i32)],
        scratch_shapes=[pltpu.VMEM((E, 128), f32)],
        compiler_params=_cparams("arbitrary"),
        name="moe_route",
    )(h, g.reshape(1, d), mod3, rw_t, rbias.reshape(E, 1), wsg, wsu, wsd)
    return outs


def _gmm_kernel(be_ref, nused_ref, tokc_ref, tokn_ref, m_hbm, wg_ref, wu_ref, wd_ref,
                o_ref, xbuf, wg_s, wu_s, wd_s, sem):
    i = pl.program_id(0)
    nused = nused_ref[0]
    slot = i % 2

    def start_gather(tok_ref, dst_slot):
        for r in range(MOE_BLK):
            t = tok_ref[0, 0, r]
            pltpu.make_async_copy(m_hbm.at[pl.ds(t, 1), :],
                                  xbuf.at[dst_slot, pl.ds(r, 1), :],
                                  sem.at[dst_slot]).start()

    def wait_gather(dst_slot):
        pltpu.make_async_copy(m_hbm.at[pl.ds(0, MOE_BLK), :], xbuf.at[dst_slot],
                              sem.at[dst_slot]).wait()

    @pl.when(jnp.logical_and(i == 0, nused > 0))
    def _():
        start_gather(tokc_ref, 0)

    prev = be_ref[jnp.maximum(i - 1, 0)]
    new_expert = jnp.logical_or(i == 0, be_ref[i] != prev)

    @pl.when(jnp.logical_and(i < nused, new_expert))
    def _():
        wg_s[...] = wg_ref[...].astype(bf16)
        wu_s[...] = wu_ref[...].astype(bf16)
        wd_s[...] = wd_ref[...].astype(bf16)

    @pl.when(i < nused)
    def _():
        wait_gather(slot)
        start_gather(tokn_ref, 1 - slot)
        xb = xbuf[slot].astype(bf16)
        hg = jnp.dot(xb, wg_s[...], preferred_element_type=f32)
        hu = jnp.dot(xb, wu_s[...], preferred_element_type=f32)
        act = (jax.nn.silu(hg) * hu).astype(bf16)
        o_ref[...] = jnp.dot(act, wd_s[...], preferred_element_type=f32)

    @pl.when(i >= nused)
    def _():
        o_ref[...] = jnp.zeros_like(o_ref)

    @pl.when(jnp.logical_and(i == nused, nused > 0))
    def _():
        wait_gather(slot)


def moe_experts(m_rows, slot_tok, block_expert, n_used, wg, wu, wd, layer):
    n_steps = block_expert.shape[0]
    d = m_rows.shape[1]
    hid = wg.shape[3]
    tok3 = slot_tok.reshape(n_steps, 1, MOE_BLK)
    smem_blk = lambda imap: pl.BlockSpec((1, 1, MOE_BLK), imap, memory_space=pltpu.SMEM)
    grid_spec = pltpu.PrefetchScalarGridSpec(
        num_scalar_prefetch=2,
        grid=(n_steps,),
        in_specs=[smem_blk(lambda i, be, nu: (i, 0, 0)),
                  smem_blk(lambda i, be, nu: (jnp.minimum(i + 1, n_steps - 1), 0, 0)),
                  pl.BlockSpec(memory_space=pl.ANY),
                  pl.BlockSpec((None, None, d, hid), lambda i, be, nu: (layer, be[i], 0, 0)),
                  pl.BlockSpec((None, None, d, hid), lambda i, be, nu: (layer, be[i], 0, 0)),
                  pl.BlockSpec((None, None, hid, d), lambda i, be, nu: (layer, be[i], 0, 0))],
        out_specs=pl.BlockSpec((MOE_BLK, d), lambda i, be, nu: (i, 0)),
        scratch_shapes=[pltpu.VMEM((2, MOE_BLK, d), f32),
                        pltpu.VMEM((d, hid), bf16), pltpu.VMEM((d, hid), bf16),
                        pltpu.VMEM((hid, d), bf16),
                        pltpu.SemaphoreType.DMA((2,))],
    )
    return pl.pallas_call(
        _gmm_kernel,
        grid_spec=grid_spec,
        out_shape=jax.ShapeDtypeStruct((n_steps * MOE_BLK, d), f32),
        compiler_params=_cparams("arbitrary"),
        name="moe_experts",
    )(block_expert, n_used, tok3, tok3, m_rows, wg, wu, wd)


def _comb_kernel(dc_ref, dn_ref, ys_hbm, w_ref, sh_ref, h_ref, mod_ref, o_ref, gbuf, sem,
                 *, n_tiles):
    i = pl.program_id(0)
    slot = i % 2
    T = COMB_TILE

    def start_gather(d_ref, dst_slot):
        for r in range(T):
            for k in range(TOP_K):
                t = d_ref[0, k, r]
                pltpu.make_async_copy(ys_hbm.at[pl.ds(t, 1), :],
                                      gbuf.at[dst_slot, k, pl.ds(r, 1), :],
                                      sem.at[dst_slot]).start()

    @pl.when(i == 0)
    def _():
        start_gather(dc_ref, 0)

    def step(issue_next):
        for k in range(TOP_K):
            pltpu.make_async_copy(ys_hbm.at[pl.ds(0, T), :], gbuf.at[slot, k],
                                  sem.at[slot]).wait()
        if issue_next:
            start_gather(dn_ref, 1 - slot)
        w = w_ref[...]
        acc = gbuf[slot, 0] * w[:, 0:1]
        for k in range(1, TOP_K):
            acc = acc + gbuf[slot, k] * w[:, k:k + 1]
        o_ref[...] = h_ref[...] + mod_ref[0, 5:6, :] * (acc + sh_ref[...])

    @pl.when(i + 1 < n_tiles)
    def _():
        step(True)

    @pl.when(i + 1 >= n_tiles)
    def _():
        step(False)


def moe_combine(y_sorted, dest_t, wts_t, shared, h, mod3, geom, lat_only):
    m, d = h.shape
    bsz, L, n_ctx = geom
    T = COMB_TILE
    n_tiles = m // T
    if lat_only:
        lpb = (L - n_ctx) // T
        mod_row = lambda i: i // lpb
    else:
        mod_row = _mod_row_map(geom, T)
    smem_blk = lambda imap: pl.BlockSpec((1, TOP_K, T), imap, memory_space=pltpu.SMEM)
    return pl.pallas_call(
        functools.partial(_comb_kernel, n_tiles=n_tiles),
        grid=(n_tiles,),
        in_specs=[smem_blk(lambda i: (i, 0, 0)),
                  smem_blk(lambda i: (jnp.minimum(i + 1, n_tiles - 1), 0, 0)),
                  pl.BlockSpec(memory_space=pl.ANY),
                  pl.BlockSpec((T, TOP_K), lambda i: (i, 0)),
                  pl.BlockSpec((T, d), lambda i: (i, 0)),
                  pl.BlockSpec((T, d), lambda i: (i, 0)),
                  pl.BlockSpec((1, 6, d), lambda i: (mod_row(i), 0, 0))],
        out_specs=pl.BlockSpec((T, d), lambda i: (i, 0)),
        out_shape=jax.ShapeDtypeStruct((m, d), f32),
        scratch_shapes=[pltpu.VMEM((2, TOP_K, T, d), f32),
                        pltpu.SemaphoreType.DMA((2,))],
        compiler_params=_cparams("arbitrary"),
        name="moe_combine",
    )(dest_t, dest_t, y_sorted, wts_t, shared, h, mod3)


def moe_layer(h, g2, mod3, layer, rw, rbias, wg, wu, wd, wsg, wsu, wsd, geom, lat_only):
    m, d = h.shape
    E = N_EXPERTS
    mt, shared, top_e, wts, pos, cnt = moe_route(
        h, g2, mod3, rw.T, rbias, wsg.astype(bf16), wsu.astype(bf16), wsd.astype(bf16),
        geom, lat_only)
    counts = cnt[:, 0]
    padded = ((counts + MOE_BLK - 1) // MOE_BLK) * MOE_BLK
    pad_end = jnp.cumsum(padded)
    pad_start = pad_end - padded
    eids = jnp.arange(E, dtype=i32)[:, None, None]
    dest = pos + jnp.sum(jnp.where(top_e[None] == eids, pad_start[:, None, None], 0), axis=0)
    n_steps = (m * TOP_K) // MOE_BLK + E + 1
    tok = jnp.broadcast_to(jnp.arange(m, dtype=i32)[None, :], (TOP_K, m))
    slot_tok = jnp.zeros((n_steps * MOE_BLK,), i32).at[dest.reshape(-1)].set(tok.reshape(-1))
    block_start = jnp.arange(n_steps, dtype=i32) * MOE_BLK
    block_expert = jnp.minimum(
        jnp.sum((pad_end[None, :] <= block_start[:, None]).astype(i32), axis=1), E - 1)
    n_used = (pad_end[-1:] // MOE_BLK).astype(i32)
    y_sorted = moe_experts(mt, slot_tok, block_expert, n_used, wg, wu, wd, layer)
    T = COMB_TILE
    dest_t = dest.reshape(TOP_K, m // T, T).transpose(1, 0, 2)
    return moe_combine(y_sorted, dest_t, wts.T, shared, h, mod3, geom, lat_only)


def _rope_tables_t(n_ctx, n_lat):
    n_rows = n_lat // GRID_W
    rows = jnp.repeat(jnp.arange(n_rows, dtype=f32), GRID_W)
    cols = jnp.tile(jnp.arange(GRID_W, dtype=f32), n_rows)
    n_freq = ATT_HEADDIM // 4
    inv_freq = ROPE_BASE ** (-jnp.arange(n_freq, dtype=f32) / n_freq)
    ar = rows[None, :] * inv_freq[:, None]
    ac = cols[None, :] * inv_freq[:, None]
    cos_t = jnp.concatenate([jnp.cos(ar), jnp.cos(ar), jnp.cos(ac), jnp.cos(ac)], axis=0)
    sin_t = jnp.concatenate([-jnp.sin(ar), jnp.sin(ar), -jnp.sin(ac), jnp.sin(ac)], axis=0)
    cos_t = jnp.concatenate([jnp.ones((ATT_HEADDIM, n_ctx), f32), cos_t], axis=1)
    sin_t = jnp.concatenate([jnp.zeros((ATT_HEADDIM, n_ctx), f32), sin_t], axis=1)
    return cos_t, sin_t


def even_layer_mix(h, mod3, g1, w_in, gm_ws, gm_bs, gm_ln_g, gm_ln_b, conv_w, conv_b,
                   a_log, dt_bias, d_skip, ssd_norm_g, w_out, geom):
    a = norm_modulate(h, g1, mod3, geom, 0, 1)
    n_main = w_in.shape[1] - 2 * SSD_HEADS
    proj = matmul_nn(a, w_in[:, :n_main].astype(bf16), f32)
    w_dt = w_in[:, n_main:].reshape(-1, 2, SSD_GROUPS, SSD_HPG).transpose(2, 1, 3, 0)
    dt_t = dt_project(a, w_dt.reshape(2 * SSD_HEADS, -1))
    bexp = jnp.repeat(gm_bs.T, A_CHUNK, axis=1)
    g_out = gmlp_mix(proj, gm_ws.astype(bf16), bexp, gm_ln_g, gm_ln_b)
    y_out = ssd_mix(proj, dt_t, conv_w, conv_b, a_log, dt_bias, d_skip, ssd_norm_g, geom)
    aw = gm_ws.shape[0] * gm_ws.shape[1]
    w_o = w_out.astype(bf16)
    return out_project_joint([g_out, y_out], [w_o[:aw], w_o[aw:]], h, mod3, geom)


def odd_layer_mix_lat(h, mod3, g1, w_q, w_kv, q_norm_g, k_norm_g, sink, w_o, geom):
    bsz, L, n_ctx = geom
    a = norm_modulate(h, g1, mod3, geom, 0, 1)
    w_t = jnp.concatenate([w_q, w_kv], axis=1).T.astype(bf16)
    cos_t, sin_t = _rope_tables_t(n_ctx, L - n_ctx)
    qkv_t = qkv_project_t(a, w_t, q_norm_g, k_norm_g, cos_t, sin_t, geom)
    att = window_attention(qkv_t, sink, geom)
    return out_project_lat(att, w_o.astype(bf16), h, mod3, geom)


def kernel(x, c, ctx, c_ctx, mod_w, mod_b, norm1_g, norm2_g, ev_w_in, ev_gm_ws, ev_gm_bs, ev_gm_ln_g, ev_gm_ln_b, ev_conv_w, ev_conv_b, ev_a_log, ev_dt_bias, ev_d_skip, ev_ssd_norm_g, ev_w_out, od_w_q, od_w_kv, od_q_norm_g, od_k_norm_g, od_sink, od_w_o, moe_router_w, moe_router_bias, moe_w_gate, moe_w_up, moe_w_down, moe_ws_gate, moe_ws_up, moe_ws_down):
    bsz, n_lat, d = x.shape
    n_ctx = ctx.shape[1]
    L = n_ctx + n_lat
    geom = (bsz, L, n_ctx)
    depth = mod_w.shape[0]
    assert depth == 2 and n_ctx == ROW_TILE and n_lat % ROW_TILE == 0 and bsz < 16

    h = jnp.concatenate([ctx, x], axis=1).reshape(bsz * L, d)
    c_all = jnp.zeros((16, d), f32).at[:bsz].set(c).at[bsz].set(c_ctx)

    mod3 = modulation(c_all, mod_w, mod_b, 0).reshape(16, 6, d)
    h = even_layer_mix(h, mod3, norm1_g[0], ev_w_in[0], ev_gm_ws[0], ev_gm_bs[0], ev_gm_ln_g[0],
                       ev_gm_ln_b[0], ev_conv_w[0], ev_conv_b[0], ev_a_log[0], ev_dt_bias[0],
                       ev_d_skip[0], ev_ssd_norm_g[0], ev_w_out[0], geom)
    h = moe_layer(h, norm2_g[0], mod3, 0, moe_router_w[0], moe_router_bias[0], moe_w_gate,
                  moe_w_up, moe_w_down, moe_ws_gate[0], moe_ws_up[0], moe_ws_down[0],
                  geom, lat_only=False)

    mod3 = modulation(c_all, mod_w, mod_b, 1).reshape(16, 6, d)
    h_lat = odd_layer_mix_lat(h, mod3, norm1_g[1], od_w_q[0], od_w_kv[0], od_q_norm_g[0],
                              od_k_norm_g[0], od_sink[0], od_w_o[0], geom)
    h_lat = moe_layer(h_lat, norm2_g[1], mod3, 1, moe_router_w[1], moe_router_bias[1],
                      moe_w_gate, moe_w_up, moe_w_down, moe_ws_gate[1], moe_ws_up[1],
                      moe_ws_down[1], geom, lat_only=True)
    return h_lat.reshape(bsz, n_lat, d)
```

```python
import functools

import jax
import jax.numpy as jnp
from jax import lax
from jax.experimental import pallas as pl
from jax.experimental.pallas import tpu as pltpu

f32 = jnp.float32
bf16 = jnp.bfloat16
i32 = jnp.int32
HIGHEST = lax.Precision.HIGHEST

D_MODEL = 2048
EPS = 1e-6
GRID_W = 64

A_CHUNK = 128
A_GROUPS = 16

SSD_HEADS = 32
SSD_HEADDIM = 64
SSD_GROUPS = 8
SSD_STATE = 128
SSD_CHUNK = 128
SSD_HPG = SSD_HEADS // SSD_GROUPS
SSD_GW = SSD_HPG * SSD_HEADDIM

ATT_HEADDIM = 64
ATT_HEADS = 32
ATT_KV_HEADS = 4
ATT_Q_PER_KV = 8
ATT_WINDOW = 128
ATT_BLOCK = 128
ATT_SCALE = ATT_HEADDIM ** -0.5
ROPE_BASE = 10000.0

N_EXPERTS = 64
N_EXPERT_GROUPS = 8
GROUP_SIZE = N_EXPERTS // N_EXPERT_GROUPS
TOP_GROUPS = 4
TOP_K = 8
EXPERT_HIDDEN = 512
ROUTED_SCALE = 2.5

ROW_TILE = 256
BIG_TILE = 768
MOE_BLK = 256
COMB_TILE = 64
VMEM_LIMIT = 56 * 1024 * 1024

_NT = (((1,), (1,)), ((), ()))
_TN = (((0,), (0,)), ((), ()))


def _cparams(*sem):
    return pltpu.CompilerParams(dimension_semantics=sem, vmem_limit_bytes=VMEM_LIMIT)


def _pack_bf16_pair(x):
    c = x.shape[1] // 2
    lo = lax.bitcast_convert_type(x[:, :c].astype(bf16).astype(f32), jnp.uint32)
    hi = lax.bitcast_convert_type(x[:, c:].astype(bf16).astype(f32), jnp.uint32)
    return (lo >> 16) | hi


def _unpack_bf16_pair(p):
    lo = lax.bitcast_convert_type(p << 16, f32)
    hi = lax.bitcast_convert_type(p & jnp.uint32(0xFFFF0000), f32)
    return lo, hi


def _pick_tile(m, options):
    for t in options:
        if m % t == 0:
            return t
    raise ValueError(f"no tile for {m}")


def _mod_kernel(c_ref, w_ref, b_ref, o_ref):
    sc = jax.nn.silu(c_ref[...])
    o_ref[...] = jnp.dot(sc, w_ref[...], precision=HIGHEST,
                         preferred_element_type=f32) + b_ref[...]


def modulation(c_all, mod_w, mod_b, layer):
    rows, d = c_all.shape
    n = mod_w.shape[2]
    tn = 1024
    return pl.pallas_call(
        _mod_kernel,
        grid=(n // tn,),
        in_specs=[pl.BlockSpec((rows, d), lambda j: (0, 0)),
                  pl.BlockSpec((None, d, tn), lambda j: (layer, 0, j)),
                  pl.BlockSpec((None, 1, tn), lambda j: (layer, 0, j))],
        out_specs=pl.BlockSpec((rows, tn), lambda j: (0, j)),
        out_shape=jax.ShapeDtypeStruct((rows, n), f32),
        compiler_params=_cparams("arbitrary"),
        name="modulation",
    )(c_all, mod_w, mod_b.reshape(mod_b.shape[0], 1, n))


def _mod_row_map(geom, tile):
    bsz, L, n_ctx = geom
    tpb = L // tile
    nct = n_ctx // tile

    def mod_row(i):
        return jnp.where((i % tpb) < nct, bsz, i // tpb)
    return mod_row


def _lat_tile_map(geom, tile):
    bsz, L, n_ctx = geom
    lpb = (L - n_ctx) // tile
    tpb = L // tile
    nct = n_ctx // tile

    def joint(t):
        return (t // lpb) * tpb + nct + (t % lpb)
    return joint


def _norm_mod_kernel(x_ref, g_ref, mod_ref, o_ref, *, shift_idx, scale_idx):
    x = x_ref[...]
    ms = jnp.mean(x * x, axis=-1, keepdims=True)
    y = x * lax.rsqrt(ms + EPS) * g_ref[...]
    y = y * (1.0 + mod_ref[0, scale_idx:scale_idx + 1, :]) + mod_ref[0, shift_idx:shift_idx + 1, :]
    o_ref[...] = y.astype(o_ref.dtype)


def norm_modulate(h, g, mod3, geom, shift_idx, scale_idx):
    m, d = h.shape
    mod_row = _mod_row_map(geom, ROW_TILE)
    return pl.pallas_call(
        functools.partial(_norm_mod_kernel, shift_idx=shift_idx, scale_idx=scale_idx),
        grid=(m // ROW_TILE,),
        in_specs=[pl.BlockSpec((ROW_TILE, d), lambda i: (i, 0)),
                  pl.BlockSpec((1, d), lambda i: (0, 0)),
                  pl.BlockSpec((1, 6, d), lambda i: (mod_row(i), 0, 0))],
        out_specs=pl.BlockSpec((ROW_TILE, d), lambda i: (i, 0)),
        out_shape=jax.ShapeDtypeStruct((m, d), bf16),
        compiler_params=_cparams("parallel"),
        name="norm_modulate",
    )(h, g.reshape(1, d), mod3)


def _mm_kernel(a_ref, w_ref, o_ref):
    o_ref[...] = jnp.dot(a_ref[...], w_ref[...],
                         preferred_element_type=f32).astype(o_ref.dtype)


def matmul_nn(a, w, out_dtype, tn=1024):
    m, k = a.shape
    n = w.shape[1]
    tm = _pick_tile(m, (1024, 768, 512, 256))
    return pl.pallas_call(
        _mm_kernel,
        grid=(n // tn, m // tm),
        in_specs=[pl.BlockSpec((tm, k), lambda j, i: (i, 0)),
                  pl.BlockSpec((k, tn), lambda j, i: (0, j))],
        out_specs=pl.BlockSpec((tm, tn), lambda j, i: (i, j)),
        out_shape=jax.ShapeDtypeStruct((m, n), out_dtype),
        compiler_params=_cparams("parallel", "parallel"),
        name="matmul_nn",
    )(a, w)


def _dt_kernel(a_ref, w_ref, o_ref):
    o_ref[...] = lax.dot_general(w_ref[...], a_ref[...].astype(f32), _NT,
                                 precision=HIGHEST, preferred_element_type=f32)


def dt_project(a, w_dt_t):
    m, k = a.shape
    r = w_dt_t.shape[0]
    tm = _pick_tile(m, (1024, 768, 512, 256))
    return pl.pallas_call(
        _dt_kernel,
        grid=(m // tm,),
        in_specs=[pl.BlockSpec((tm, k), lambda i: (i, 0)),
                  pl.BlockSpec((r, k), lambda i: (0, 0))],
        out_specs=pl.BlockSpec((r, tm), lambda i: (0, i)),
        out_shape=jax.ShapeDtypeStruct((r, m), f32),
        compiler_params=_cparams("parallel"),
        name="dt_project",
    )(a, w_dt_t)


def _gelu_exact(x):
    return 0.5 * x * (1.0 + lax.erf(x * (2.0 ** -0.5)))


def _gmlp_kernel(u_ref, v_ref, ws_ref, bexp_ref, lng_ref, lnb_ref, o_ref):
    u = _gelu_exact(u_ref[...])
    v = _gelu_exact(v_ref[...])
    mu = jnp.mean(v, axis=-1, keepdims=True)
    vc = v - mu
    var = jnp.mean(vc * vc, axis=-1, keepdims=True)
    vn = (vc * lax.rsqrt(var + EPS) * lng_ref[...] + lnb_ref[...]).astype(bf16)
    for g in range(A_GROUPS):
        sl = slice(g * A_CHUNK, (g + 1) * A_CHUNK)
        mixed = jnp.dot(ws_ref[g], vn[:, sl], preferred_element_type=f32) + bexp_ref[:, sl]
        o_ref[:, sl] = (u[:, sl] * mixed).astype(o_ref.dtype)


def gmlp_mix(proj, ws_bf, bexp, ln_g, ln_b):
    m = proj.shape[0]
    w = A_GROUPS * A_CHUNK
    return pl.pallas_call(
        _gmlp_kernel,
        grid=(m // A_CHUNK,),
        in_specs=[pl.BlockSpec((A_CHUNK, w), lambda c: (c, 0)),
                  pl.BlockSpec((A_CHUNK, w), lambda c: (c, 1)),
                  pl.BlockSpec((A_GROUPS, A_CHUNK, A_CHUNK), lambda c: (0, 0, 0)),
                  pl.BlockSpec((A_CHUNK, w), lambda c: (0, 0)),
                  pl.BlockSpec((1, w), lambda c: (0, 0)),
                  pl.BlockSpec((1, w), lambda c: (0, 0))],
        out_specs=pl.BlockSpec((A_CHUNK, w), lambda c: (c, 0)),
        out_shape=jax.ShapeDtypeStruct((m, w), bf16),
        compiler_params=_cparams("parallel"),
        name="gmlp_mix",
    )(proj, proj, ws_bf, bexp, ln_g.reshape(1, w), ln_b.reshape(1, w))


def _head_expand(v, lane):
    out = v[:, 3:4]
    for r in (2, 1, 0):
        out = jnp.where(lane < SSD_HEADDIM * (r + 1), v[:, r:r + 1], out)
    return out


def _row_expand(v, row):
    out = v[:, 3:4]
    for r in (2, 1, 0):
        out = jnp.where(row < SSD_HEADDIM * (r + 1), v[:, r:r + 1], out)
    return out


def _ssd_kernel(x_ref, b_ref, c_ref, z_ref, dtr_ref, dtc_ref,
                cwx_ref, cwb_ref, cwc_ref, cbx_ref, cbb_ref, cbc_ref,
                alr_ref, alc_ref, dbr_ref, dbc_ref, dsk_ref, ng_ref,
                o_ref,
                xs_s, bs_s, cs_s, dtr_s, dtar_s, dtc_s, dtac_s, suf_s, yacc_s, sf_s, sb_s,
                *, n_ctx, L):
    Q = SSD_CHUNK
    nc = L // Q
    ncc = n_ctx // Q
    H = SSD_HPG

    bias_r = dbr_ref[...]
    a_r = -jnp.exp(alr_ref[...])
    for c in range(nc):
        dt = jax.nn.softplus(dtr_ref[:, c * Q:(c + 1) * Q] + bias_r)
        dtr_s[c] = dt
        dtar_s[c] = dt * a_r
    dtc = jax.nn.softplus(dtc_ref[0] + dbc_ref[0])
    dtc_s[...] = dtc
    dtac_s[...] = dtc * (-jnp.exp(alc_ref[0]))

    ii = lax.broadcasted_iota(i32, (Q, Q), 0)
    jj = lax.broadcasted_iota(i32, (Q, Q), 1)
    lower = ii >= jj
    upper = ii <= jj
    tril = lower.astype(f32)
    triu = upper.astype(f32)
    lane = lax.broadcasted_iota(i32, (Q, SSD_GW), 1)
    row = lax.broadcasted_iota(i32, (SSD_GW, SSD_STATE), 0)

    def conv_act(ref, w_ref, bias_ref, s):
        xv = ref[pl.ds(s, Q), :]
        w = xv.shape[1]
        r0 = lax.broadcasted_iota(i32, (Q, w), 0)
        prev = ref[pl.ds(jnp.maximum(s - 1, 0), 1), :]
        nxt = ref[pl.ds(jnp.minimum(s + Q, L - 1), 1), :]
        has_prev = jnp.logical_and(s != 0, s != n_ctx)
        has_next = jnp.logical_and(s + Q != n_ctx, s + Q != L)
        prev = jnp.where(has_prev, prev, 0.0)
        nxt = jnp.where(has_next, nxt, 0.0)
        xm1 = jnp.where(r0 == 0, prev, pltpu.roll(xv, 1, 0))
        xp1 = jnp.where(r0 == Q - 1, nxt, pltpu.roll(xv, Q - 1, 0))
        y = w_ref[0:1, :] * xm1 + w_ref[1:2, :] * xv + w_ref[2:3, :] * xp1 + bias_ref[...]
        return jax.nn.silu(y)

    sf_s[...] = jnp.zeros_like(sf_s)
    sb_s[...] = jnp.zeros_like(sb_s)

    def fwd_body(c, carry):
        s = pl.multiple_of(c * Q, Q)
        X = conv_act(x_ref, cwx_ref, cbx_ref, s)
        Bc = conv_act(b_ref, cwb_ref, cbb_ref, s)
        Cc = conv_act(c_ref, cwc_ref, cbc_ref, s)
        xs_s[pl.ds(s, Q), :] = X
        bs_s[pl.ds(s, Q), :] = Bc
        cs_s[pl.ds(s, Q), :] = Cc
        Bb = Bc.astype(bf16)
        Cb = Cc.astype(bf16)
        cb = lax.dot_general(Cb, Bb, _NT, preferred_element_type=f32)
        dta_c = dtac_s[pl.ds(s, Q), :]
        dta_r = dtar_s[c]
        dt_r = dtr_s[c]
        p_col = jnp.dot(tril, dta_c, precision=HIGHEST, preferred_element_type=f32)
        s_col = jnp.dot(triu, dta_c, precision=HIGHEST, preferred_element_type=f32)
        p_row = jnp.dot(dta_r, triu, precision=HIGHEST, preferred_element_type=f32)
        s_row = jnp.dot(dta_r, tril, precision=HIGHEST, preferred_element_type=f32)
        suf_s[pl.ds(s, Q), :] = s_col
        y = jnp.zeros((Q, SSD_GW), f32)
        for r in range(H):
            lf = jnp.exp(jnp.where(lower, p_col[:, r:r + 1] - p_row[r:r + 1, :], -jnp.inf))
            lb = jnp.exp(jnp.where(upper, s_col[:, H + r:H + r + 1] - s_row[H + r:H + r + 1, :],
                                   -jnp.inf))
            mr = cb * (lf * dt_r[r:r + 1, :] + lb * dt_r[H + r:H + r + 1, :])
            head = jnp.logical_and(lane >= SSD_HEADDIM * r, lane < SSD_HEADDIM * (r + 1))
            xm = jnp.where(head, X, 0.0).astype(bf16)
            y = y + jnp.dot(mr.astype(bf16), xm, preferred_element_type=f32)
        sf = sf_s[...]
        yi = lax.dot_general(Cb, sf.astype(bf16), _NT, preferred_element_type=f32)
        pf = p_col[:, 0:H]
        y = y + yi * _head_expand(jnp.exp(pf), lane)
        yacc_s[pl.ds(s, Q), :] = y
        tot = p_col[Q - 1:Q, 0:H]
        wf = dtc_s[pl.ds(s, Q), 0:H] * jnp.exp(tot - pf)
        xw = (X * _head_expand(wf, lane)).astype(bf16)
        upd = lax.dot_general(xw, Bb, _TN, preferred_element_type=f32)
        sf_s[...] = sf * _row_expand(jnp.exp(tot), row) + upd
        return carry

    lax.fori_loop(0, nc, fwd_body, 0)

    def bwd_body(k, carry):
        c = jnp.where(k < ncc, ncc - 1 - k, nc - 1 - (k - ncc))
        s = pl.multiple_of(c * Q, Q)
        X = xs_s[pl.ds(s, Q), :]
        Bb = bs_s[pl.ds(s, Q), :].astype(bf16)
        Cb = cs_s[pl.ds(s, Q), :].astype(bf16)
        ab = suf_s[pl.ds(s, Q), H:2 * H]
        tot = ab[0:1, :]
        sb = sb_s[...]
        yi = lax.dot_general(Cb, sb.astype(bf16), _NT, preferred_element_type=f32)
        y = yacc_s[pl.ds(s, Q), :] + yi * _head_expand(jnp.exp(ab), lane) + dsk_ref[...] * X
        wb = dtc_s[pl.ds(s, Q), H:2 * H] * jnp.exp(tot - ab)
        xw = (X * _head_expand(wb, lane)).astype(bf16)
        upd = lax.dot_general(xw, Bb, _TN, preferred_element_type=f32)
        sb_s[...] = sb * _row_expand(jnp.exp(tot), row) + upd
        y = y * jax.nn.silu(z_ref[pl.ds(s, Q), :])
        ms = jnp.mean(y * y, axis=-1, keepdims=True)
        o_ref[pl.ds(s, Q), :] = (y * lax.rsqrt(ms + EPS) * ng_ref[...]).astype(o_ref.dtype)
        return carry

    lax.fori_loop(0, nc, bwd_body, 0)


def ssd_mix(proj, dt_t, conv_w, conv_b, a_log, dt_bias, d_skip, norm_g, geom):
    bsz, L, n_ctx = geom
    m = proj.shape[0]
    G, H, GW, N = SSD_GROUPS, SSD_HPG, SSD_GW, SSD_STATE
    nc = L // SSD_CHUNK
    col0 = proj.shape[1] - (SSD_HEADS * SSD_HEADDIM + 2 * G * N)
    zc0 = col0 - SSD_HEADS * SSD_HEADDIM
    xb, zb = col0 // GW, zc0 // GW
    bb = (col0 + SSD_HEADS * SSD_HEADDIM) // N
    cb = bb + G
    cxb, cbb, ccb = 0, (SSD_HEADS * SSD_HEADDIM) // N, (SSD_HEADS * SSD_HEADDIM) // N + G

    def to_gdr(p):
        return p.reshape(2, G, H).transpose(1, 0, 2).reshape(G, 2 * H)

    al, db = to_gdr(a_log), to_gdr(dt_bias)
    dt_col = dt_t.reshape(G, 2 * H, m).transpose(0, 2, 1)
    dsk = jnp.repeat(d_skip, SSD_HEADDIM).reshape(1, -1)
    conv_b2 = conv_b.reshape(1, -1)
    kern = functools.partial(_ssd_kernel, n_ctx=n_ctx, L=L)
    return pl.pallas_call(
        kern,
        grid=(bsz, G),
        in_specs=[
            pl.BlockSpec((L, GW), lambda b, g: (b, xb + g)),
            pl.BlockSpec((L, N), lambda b, g: (b, bb + g)),
            pl.BlockSpec((L, N), lambda b, g: (b, cb + g)),
            pl.BlockSpec((L, GW), lambda b, g: (b, zb + g)),
            pl.BlockSpec((2 * H, L), lambda b, g: (g, b)),
            pl.BlockSpec((1, L, 2 * H), lambda b, g: (g, b, 0)),
            pl.BlockSpec((3, GW), lambda b, g: (0, cxb + g)),
            pl.BlockSpec((3, N), lambda b, g: (0, cbb + g)),
            pl.BlockSpec((3, N), lambda b, g: (0, ccb + g)),
            pl.BlockSpec((1, GW), lambda b, g: (0, cxb + g)),
            pl.BlockSpec((1, N), lambda b, g: (0, cbb + g)),
            pl.BlockSpec((1, N), lambda b, g: (0, ccb + g)),
            pl.BlockSpec((2 * H, 1), lambda b, g: (g, 0)),
            pl.BlockSpec((1, 1, 2 * H), lambda b, g: (g, 0, 0)),
            pl.BlockSpec((2 * H, 1), lambda b, g: (g, 0)),
            pl.BlockSpec((1, 1, 2 * H), lambda b, g: (g, 0, 0)),
            pl.BlockSpec((1, GW), lambda b, g: (0, g)),
            pl.BlockSpec((1, GW), lambda b, g: (0, g)),
        ],
        out_specs=pl.BlockSpec((L, GW), lambda b, g: (b, g)),
        out_shape=jax.ShapeDtypeStruct((m, G * GW), bf16),
        scratch_shapes=[
            pltpu.VMEM((L, GW), f32), pltpu.VMEM((L, N), f32), pltpu.VMEM((L, N), f32),
            pltpu.VMEM((nc, 2 * H, SSD_CHUNK), f32), pltpu.VMEM((nc, 2 * H, SSD_CHUNK), f32),
            pltpu.VMEM((L, 2 * H), f32), pltpu.VMEM((L, 2 * H), f32), pltpu.VMEM((L, 2 * H), f32),
            pltpu.VMEM((L, GW), f32),
            pltpu.VMEM((GW, N), f32), pltpu.VMEM((GW, N), f32),
        ],
        compiler_params=_cparams("parallel", "parallel"),
        name="ssd_mix",
    )(proj, proj, proj, proj, dt_t, dt_col,
      conv_w, conv_w, conv_w, conv_b2, conv_b2, conv_b2,
      al.reshape(G * 2 * H, 1), al.reshape(G, 1, 2 * H),
      db.reshape(G * 2 * H, 1), db.reshape(G, 1, 2 * H),
      dsk, norm_g.reshape(1, -1))


def _outproj_kernel(*refs, n_in, n_ctx, tpb, tm):
    a_refs = refs[:n_in]
    w_refs = refs[n_in:2 * n_in]
    h_ref, ml_ref, mc_ref, o_ref = refs[2 * n_in:]
    i = pl.program_id(1)
    acc = jnp.dot(a_refs[0][...], w_refs[0][...], preferred_element_type=f32)
    for k in range(1, n_in):
        acc = acc + jnp.dot(a_refs[k][...], w_refs[k][...], preferred_element_type=f32)
    n_ctx_rows = jnp.clip(n_ctx - (i % tpb) * tm, 0, tm)
    row = lax.broadcasted_iota(i32, acc.shape, 0)
    gate = jnp.where(row < n_ctx_rows, mc_ref[0, 2:3, :], ml_ref[0, 2:3, :])
    o_ref[...] = h_ref[...] + gate * acc


def out_project_joint(a_list, w_list, h, mod3, geom, tn=1024):
    bsz, L, n_ctx = geom
    m, d = h.shape
    tm = _pick_tile(L, (BIG_TILE, 512, 256))
    tpb = L // tm
    n_in = len(a_list)
    k = a_list[0].shape[1]
    kern = functools.partial(_outproj_kernel, n_in=n_in, n_ctx=n_ctx, tpb=tpb, tm=tm)
    in_specs = ([pl.BlockSpec((tm, k), lambda j, i: (i, 0)) for _ in range(n_in)]
                + [pl.BlockSpec((k, tn), lambda j, i: (0, j)) for _ in range(n_in)]
                + [pl.BlockSpec((tm, tn), lambda j, i: (i, j)),
                   pl.BlockSpec((1, 6, tn), lambda j, i: (i // tpb, 0, j)),
                   pl.BlockSpec((1, 6, tn), lambda j, i: (bsz, 0, j))])
    return pl.pallas_call(
        kern,
        grid=(d // tn, m // tm),
        in_specs=in_specs,
        out_specs=pl.BlockSpec((tm, tn), lambda j, i: (i, j)),
        out_shape=jax.ShapeDtypeStruct((m, d), f32),
        compiler_params=_cparams("parallel", "parallel"),
        name="out_project_joint",
    )(*a_list, *w_list, h, mod3, mod3)


def _outproj_lat_kernel(a_ref, w_ref, h_ref, ml_ref, o_ref):
    acc = jnp.dot(a_ref[...], w_ref[...], preferred_element_type=f32)
    o_ref[...] = h_ref[...] + ml_ref[0, 2:3, :] * acc


def out_project_lat(a, w, h_joint, mod3, geom, tn=1024):
    bsz, L, n_ctx = geom
    m, k = a.shape
    d = h_joint.shape[1]
    tm = ROW_TILE
    joint = _lat_tile_map(geom, tm)
    lpb = (L - n_ctx) // tm
    return pl.pallas_call(
        _outproj_lat_kernel,
        grid=(d // tn, m // tm),
        in_specs=[pl.BlockSpec((tm, k), lambda j, i: (i, 0)),
                  pl.BlockSpec((k, tn), lambda j, i: (0, j)),
                  pl.BlockSpec((tm, tn), lambda j, i: (joint(i), j)),
                  pl.BlockSpec((1, 6, tn), lambda j, i: (i // lpb, 0, j))],
        out_specs=pl.BlockSpec((tm, tn), lambda j, i: (i, j)),
        out_shape=jax.ShapeDtypeStruct((m, d), f32),
        compiler_params=_cparams("parallel", "parallel"),
        name="out_project_lat",
    )(a, w, h_joint, mod3)


def _qkv_kernel(a_ref, w_ref, gq_ref, gk_ref, cos_ref, sin_ref, o_ref, *, n_qblk):
    j = pl.program_id(1)
    acc = lax.dot_general(w_ref[...], a_ref[...], _NT, preferred_element_type=f32)

    @pl.when(j <= n_qblk)
    def _():
        g = jnp.where(j < n_qblk, gq_ref[...] * ATT_SCALE, gk_ref[...])
        cs = cos_ref[...]
        sn = sin_ref[...]
        hd = ATT_HEADDIM
        q4 = hd // 4
        for hh in range(acc.shape[0] // hd):
            blk = acc[hh * hd:(hh + 1) * hd, :]
            ms = jnp.mean(blk * blk, axis=0, keepdims=True)
            xn = blk * lax.rsqrt(ms + EPS) * g
            sw = jnp.concatenate([xn[q4:2 * q4], xn[0:q4], xn[3 * q4:4 * q4], xn[2 * q4:3 * q4]],
                                 axis=0)
            o_ref[hh * hd:(hh + 1) * hd, :] = xn * cs + sw * sn

    @pl.when(j > n_qblk)
    def _():
        o_ref[...] = acc


def qkv_project_t(a, w_t, gq, gk, cos_t, sin_t, geom):
    bsz, L, n_ctx = geom
    m, k = a.shape
    r = w_t.shape[0]
    rb = 256
    tm = _pick_tile(L, (BIG_TILE, 512, 256))
    tpb = L // tm
    n_qblk = (ATT_HEADS * ATT_HEADDIM) // rb
    return pl.pallas_call(
        functools.partial(_qkv_kernel, n_qblk=n_qblk),
        grid=(m // tm, r // rb),
        in_specs=[pl.BlockSpec((tm, k), lambda i, j: (i, 0)),
                  pl.BlockSpec((rb, k), lambda i, j: (j, 0)),
                  pl.BlockSpec((ATT_HEADDIM, 1), lambda i, j: (0, 0)),
                  pl.BlockSpec((ATT_HEADDIM, 1), lambda i, j: (0, 0)),
                  pl.BlockSpec((ATT_HEADDIM, tm), lambda i, j: (0, i % tpb)),
                  pl.BlockSpec((ATT_HEADDIM, tm), lambda i, j: (0, i % tpb))],
        out_specs=pl.BlockSpec((rb, tm), lambda i, j: (j, i)),
        out_shape=jax.ShapeDtypeStruct((r, m), f32),
        compiler_params=_cparams("parallel", "parallel"),
        name="qkv_project_t",
    )(a, w_t, gq.reshape(-1, 1), gk.reshape(-1, 1), cos_t, sin_t)


def _attn_kernel(sink_ref, q_ref, kc_ref, k0_ref, k1_ref, k2_ref,
                 vc_ref, v0_ref, v1_ref, v2_ref, o_ref, ot_s, *, n_lat, n_ctx):
    qi = pl.program_id(1)
    kvh = pl.program_id(2)
    blk = ATT_BLOCK
    hd = ATT_HEADDIM
    kt = jnp.concatenate([kc_ref[...], k0_ref[...], k1_ref[...], k2_ref[...]], axis=1)
    vt = jnp.concatenate([vc_ref[...], v0_ref[...], v1_ref[...], v2_ref[...]], axis=1)
    kt = kt.astype(bf16)
    vt = vt.astype(bf16)
    R = ATT_Q_PER_KV
    t = lax.broadcasted_iota(i32, (blk, blk), 0)
    c = lax.broadcasted_iota(i32, (blk, blk), 1)
    m_prev = jnp.logical_and(t >= c, qi >= 1)
    m_next = jnp.logical_and(t <= c, (qi + 1) * blk < n_lat)
    q_all = jnp.concatenate([q_ref[r * hd:(r + 1) * hd, :].astype(bf16) for r in range(R)],
                            axis=1)
    s_all = lax.dot_general(kt, q_all, _TN, preferred_element_type=f32)
    p_list, dens = [], []
    for r in range(R):
        s = s_all[:, r * blk:(r + 1) * blk]
        parts = [s[:n_ctx],
                 jnp.where(m_prev, s[n_ctx:n_ctx + blk], -jnp.inf),
                 s[n_ctx + blk:n_ctx + 2 * blk],
                 jnp.where(m_next, s[n_ctx + 2 * blk:], -jnp.inf)]
        sink = sink_ref[kvh * R + r]
        mx = sink
        for x in parts:
            mx = jnp.maximum(jnp.max(x, axis=0, keepdims=True), mx)
        ps = [jnp.exp(x - mx) for x in parts]
        den = jnp.exp(sink - mx)
        for x in ps:
            den = den + jnp.sum(x, axis=0, keepdims=True)
        p_list.append(jnp.concatenate(ps, axis=0).astype(bf16))
        dens.append(den)
    p_all = jnp.concatenate(p_list, axis=1)
    o_all = jnp.dot(vt, p_all, preferred_element_type=f32)
    for r in range(R):
        ot_s[r * hd:(r + 1) * hd, :] = o_all[:, r * blk:(r + 1) * blk] / dens[r]
    o_ref[...] = ot_s[...].T.astype(o_ref.dtype)


def window_attention(qkv_t, sink, geom):
    bsz, L, n_ctx = geom
    n_lat = L - n_ctx
    blk = ATT_BLOCK
    hd = ATT_HEADDIM
    nqb = n_lat // blk
    cpb = L // blk
    cb0 = n_ctx // blk
    q_rows = ATT_Q_PER_KV * hd
    k_rb = (ATT_HEADS * hd) // hd
    v_rb = k_rb + ATT_KV_HEADS

    def band(rb0, off):
        def imap(b, qi, kvh, sink_ref):
            kb = jnp.clip(qi + off, 0, nqb - 1)
            return (rb0 + kvh, b * cpb + cb0 + kb)
        return pl.BlockSpec((hd, blk), imap)

    def ctx(rb0):
        return pl.BlockSpec((hd, n_ctx), lambda b, qi, kvh, s: (rb0 + kvh, b * (L // n_ctx)))

    kern = functools.partial(_attn_kernel, n_lat=n_lat, n_ctx=n_ctx)
    grid_spec = pltpu.PrefetchScalarGridSpec(
        num_scalar_prefetch=1,
        grid=(bsz, nqb, ATT_KV_HEADS),
        in_specs=[pl.BlockSpec((q_rows, blk), lambda b, qi, kvh, s: (kvh, b * cpb + cb0 + qi)),
                  ctx(k_rb), band(k_rb, -1), band(k_rb, 0), band(k_rb, 1),
                  ctx(v_rb), band(v_rb, -1), band(v_rb, 0), band(v_rb, 1)],
        out_specs=pl.BlockSpec((blk, q_rows), lambda b, qi, kvh, s: (b * nqb + qi, kvh)),
        scratch_shapes=[pltpu.VMEM((q_rows, blk), f32)],
    )
    return pl.pallas_call(
        kern,
        grid_spec=grid_spec,
        out_shape=jax.ShapeDtypeStruct((bsz * n_lat, ATT_HEADS * hd), bf16),
        compiler_params=_cparams("parallel", "parallel", "parallel"),
        name="window_attention",
    )(sink, *([qkv_t] * 9))


def _first_max(v, idx, n):
    m = jnp.max(v, axis=0, keepdims=True)
    first = jnp.min(jnp.where(v == m, idx, n), axis=0, keepdims=True)
    return m, first


def _route_kernel(h_ref, g_ref, mod_ref, rw_ref, rb_ref, wsg_ref, wsu_ref, wsd_ref,
                  m_ref, sh_ref, te_ref, wt_ref, pos_ref, cnt_ref, carry_s):
    i = pl.program_id(0)
    T = h_ref.shape[0]
    E, NG, GS = N_EXPERTS, N_EXPERT_GROUPS, GROUP_SIZE

    @pl.when(i == 0)
    def _():
        carry_s[...] = jnp.zeros_like(carry_s)

    x = h_ref[...]
    ms = jnp.mean(x * x, axis=-1, keepdims=True)
    y = x * lax.rsqrt(ms + EPS) * g_ref[...]
    mt = y * (1.0 + mod_ref[0, 4:5, :]) + mod_ref[0, 3:4, :]
    m_ref[...] = _pack_bf16_pair(mt)
    mb = mt.astype(bf16)

    hg = jnp.dot(mb, wsg_ref[...], preferred_element_type=f32)
    hu = jnp.dot(mb, wsu_ref[...], preferred_element_type=f32)
    act = (jax.nn.silu(hg) * hu).astype(bf16)
    sh_ref[...] = jnp.dot(act, wsd_ref[...], preferred_element_type=f32)

    logits = lax.dot_general(rw_ref[...], mt, _NT, precision=HIGHEST,
                             preferred_element_type=f32)
    scores = jax.nn.sigmoid(logits)
    sel = scores + rb_ref[...]
    eidx = lax.broadcasted_iota(i32, (E, T), 0)
    midx = lax.broadcasted_iota(i32, (GS, T), 0)
    gidx = lax.broadcasted_iota(i32, (NG, T), 0)

    gs = jnp.zeros((NG, T), f32)
    for g in range(NG):
        blk = sel[g * GS:(g + 1) * GS, :]
        m1, f1 = _first_max(blk, midx, GS)
        m2 = jnp.max(jnp.where(midx == f1, -jnp.inf, blk), axis=0, keepdims=True)
        gs = jnp.where(gidx == g, m1 + m2, gs)
    gmask = jnp.zeros((NG, T), jnp.bool_)
    work = gs
    for _ in range(TOP_GROUPS):
        _, f = _first_max(work, gidx, NG)
        hit = gidx == f
        gmask = jnp.logical_or(gmask, hit)
        work = jnp.where(hit, -jnp.inf, work)
    cand = jnp.concatenate(
        [jnp.where(gmask[g:g + 1, :], sel[g * GS:(g + 1) * GS, :], -jnp.inf) for g in range(NG)],
        axis=0)
    hits = []
    chosen = jnp.zeros((E, T), jnp.bool_)
    work = cand
    for k in range(TOP_K):
        _, f = _first_max(work, eidx, E)
        hit = eidx == f
        hits.append(hit)
        chosen = jnp.logical_or(chosen, hit)
        work = jnp.where(hit, -jnp.inf, work)
        te_ref[k:k + 1, :] = f
    wts = [jnp.sum(jnp.where(h, scores, 0.0), axis=0, keepdims=True) for h in hits]
    tot = wts[0]
    for w in wts[1:]:
        tot = tot + w
    for k in range(TOP_K):
        wt_ref[k:k + 1, :] = wts[k] / (tot + 1e-20) * ROUTED_SCALE

    t0 = lax.broadcasted_iota(i32, (T, T), 0)
    t1 = lax.broadcasted_iota(i32, (T, T), 1)
    before = (t0 < t1).astype(bf16)
    chosen_f = chosen.astype(f32)
    cnt = jnp.dot(chosen_f.astype(bf16), before, preferred_element_type=f32) + carry_s[:, 0:1]
    for k in range(TOP_K):
        pos_ref[k:k + 1, :] = jnp.sum(jnp.where(hits[k], cnt, 0.0), axis=0,
                                      keepdims=True).astype(i32)
    new_carry = carry_s[...] + jnp.sum(chosen_f, axis=1, keepdims=True)
    carry_s[...] = new_carry
    cnt_ref[...] = new_carry.astype(i32)


def moe_route(h, g, mod3, rw_t, rbias, wsg, wsu, wsd, geom, lat_only):
    m, d = h.shape
    bsz, L, n_ctx = geom
    T = ROW_TILE
    if lat_only:
        lpb = (L - n_ctx) // T
        mod_row = lambda i: i // lpb
    else:
        mod_row = _mod_row_map(geom, T)
    E = N_EXPERTS
    hid = wsg.shape[1]
    const2 = lambda i: (0, 0)
    outs = pl.pallas_call(
        _route_kernel,
        grid=(m // T,),
        in_specs=[pl.BlockSpec((T, d), lambda i: (i, 0)),
                  pl.BlockSpec((1, d), const2),
                  pl.BlockSpec((1, 6, d), lambda i: (mod_row(i), 0, 0)),
                  pl.BlockSpec((E, d), const2),
                  pl.BlockSpec((E, 1), const2),
                  pl.BlockSpec((d, hid), const2),
                  pl.BlockSpec((d, hid), const2),
                  pl.BlockSpec((hid, d), const2)],
        out_specs=[pl.BlockSpec((T, d // 2), lambda i: (i, 0)),
                   pl.BlockSpec((T, d), lambda i: (i, 0)),
                   pl.BlockSpec((TOP_K, T), lambda i: (0, i)),
                   pl.BlockSpec((TOP_K, T), lambda i: (0, i)),
                   pl.BlockSpec((TOP_K, T), lambda i: (0, i)),
                   pl.BlockSpec((E, 128), const2)],
        out_shape=[jax.ShapeDtypeStruct((m, d // 2), jnp.uint32),
                   jax.ShapeDtypeStruct((m, d), f32),
                   jax.ShapeDtypeStruct((TOP_K, m), i32),
                   jax.ShapeDtypeStruct((TOP_K, m), f32),
                   jax.ShapeDtypeStruct((TOP_K, m), i32),
                   jax.ShapeDtypeStruct((E, 128), i32)],
        scratch_shapes=[pltpu.VMEM((E, 128), f32)],
        compiler_params=_cparams("arbitrary"),
        name="moe_route",
    )(h, g.reshape(1, d), mod3, rw_t, rbias.reshape(E, 1), wsg, wsu, wsd)
    return outs


def _gmm_kernel(be_ref, nused_ref, tokc_ref, tokn_ref, m_hbm, wg_ref, wu_ref, wd_ref,
                o_ref, xbuf, wg_s, wu_s, wd_s, sem):
    i = pl.program_id(0)
    nused = nused_ref[0]
    slot = i % 2

    def start_gather(tok_ref, dst_slot):
        for r in range(MOE_BLK):
            t = tok_ref[0, 0, r]
            pltpu.make_async_copy(m_hbm.at[pl.ds(t, 1), :],
                                  xbuf.at[dst_slot, pl.ds(r, 1), :],
                                  sem.at[dst_slot]).start()

    def wait_gather(dst_slot):
        pltpu.make_async_copy(m_hbm.at[pl.ds(0, MOE_BLK), :], xbuf.at[dst_slot],
                              sem.at[dst_slot]).wait()

    @pl.when(jnp.logical_and(i == 0, nused > 0))
    def _():
        start_gather(tokc_ref, 0)

    prev = be_ref[jnp.maximum(i - 1, 0)]
    new_expert = jnp.logical_or(i == 0, be_ref[i] != prev)

    @pl.when(jnp.logical_and(i < nused, new_expert))
    def _():
        wg_s[...] = wg_ref[...].astype(bf16)
        wu_s[...] = wu_ref[...].astype(bf16)
        wd_s[...] = wd_ref[...].astype(bf16)

    @pl.when(i < nused)
    def _():
        wait_gather(slot)
        x_lo, x_hi = _unpack_bf16_pair(xbuf[slot])
        x_lo = x_lo.astype(bf16)
        x_hi = x_hi.astype(bf16)
        start_gather(tokn_ref, 1 - slot)
        kh = x_lo.shape[1]
        hg = (jnp.dot(x_lo, wg_s[:kh, :], preferred_element_type=f32)
              + jnp.dot(x_hi, wg_s[kh:, :], preferred_element_type=f32))
        hu = (jnp.dot(x_lo, wu_s[:kh, :], preferred_element_type=f32)
              + jnp.dot(x_hi, wu_s[kh:, :], preferred_element_type=f32))
        act = (jax.nn.silu(hg) * hu).astype(bf16)
        o_ref[...] = _pack_bf16_pair(jnp.dot(act, wd_s[...], preferred_element_type=f32))

    @pl.when(i >= nused)
    def _():
        o_ref[...] = jnp.zeros_like(o_ref)

    @pl.when(jnp.logical_and(i == nused, nused > 0))
    def _():
        wait_gather(slot)


def moe_experts(m_rows, slot_tok, block_expert, n_used, wg, wu, wd, layer):
    n_steps = block_expert.shape[0]
    d = wg.shape[2]
    dp = m_rows.shape[1]
    hid = wg.shape[3]
    tok3 = slot_tok.reshape(n_steps, 1, MOE_BLK)
    smem_blk = lambda imap: pl.BlockSpec((1, 1, MOE_BLK), imap, memory_space=pltpu.SMEM)
    grid_spec = pltpu.PrefetchScalarGridSpec(
        num_scalar_prefetch=2,
        grid=(n_steps,),
        in_specs=[smem_blk(lambda i, be, nu: (i, 0, 0)),
                  smem_blk(lambda i, be, nu: (jnp.minimum(i + 1, n_steps - 1), 0, 0)),
                  pl.BlockSpec(memory_space=pl.ANY),
                  pl.BlockSpec((None, None, d, hid), lambda i, be, nu: (layer, be[i], 0, 0)),
                  pl.BlockSpec((None, None, d, hid), lambda i, be, nu: (layer, be[i], 0, 0)),
                  pl.BlockSpec((None, None, hid, d), lambda i, be, nu: (layer, be[i], 0, 0))],
        out_specs=pl.BlockSpec((MOE_BLK, dp), lambda i, be, nu: (i, 0)),
        scratch_shapes=[pltpu.VMEM((2, MOE_BLK, dp), jnp.uint32),
                        pltpu.VMEM((d, hid), bf16), pltpu.VMEM((d, hid), bf16),
                        pltpu.VMEM((hid, d), bf16),
                        pltpu.SemaphoreType.DMA((2,))],
    )
    return pl.pallas_call(
        _gmm_kernel,
        grid_spec=grid_spec,
        out_shape=jax.ShapeDtypeStruct((n_steps * MOE_BLK, dp), jnp.uint32),
        compiler_params=_cparams("arbitrary"),
        name="moe_experts",
    )(block_expert, n_used, tok3, tok3, m_rows, wg, wu, wd)


def _comb_kernel(dc_ref, dn_ref, ys_hbm, w_ref, sh_ref, h_ref, mod_ref, o_ref,
                 gbuf0, gbuf1, sem, *, n_tiles):
    i = pl.program_id(0)
    T = COMB_TILE
    gbufs = (gbuf0, gbuf1)

    def start_gather(d_ref, p):
        for r in range(T):
            for k in range(TOP_K):
                t = d_ref[0, k, r]
                pltpu.make_async_copy(ys_hbm.at[pl.ds(t, 1), :],
                                      gbufs[p].at[k, pl.ds(r, 1), :],
                                      sem.at[p]).start()

    @pl.when(i == 0)
    def _():
        start_gather(dc_ref, 0)

    def step(p, issue_next):
        for k in range(TOP_K):
            pltpu.make_async_copy(ys_hbm.at[pl.ds(0, T), :], gbufs[p].at[k],
                                  sem.at[p]).wait()
        if issue_next:
            start_gather(dn_ref, 1 - p)
        w = w_ref[...]
        half = o_ref.shape[1] // 2
        acc_lo = jnp.zeros((T, half), f32)
        acc_hi = jnp.zeros((T, half), f32)
        for k in range(TOP_K):
            lo, hi = _unpack_bf16_pair(gbufs[p][k])
            acc_lo = acc_lo + lo * w[:, k:k + 1]
            acc_hi = acc_hi + hi * w[:, k:k + 1]
        gate = mod_ref[0, 5:6, :]
        o_ref[:, :half] = h_ref[:, :half] + gate[:, :half] * (acc_lo + sh_ref[:, :half])
        o_ref[:, half:] = h_ref[:, half:] + gate[:, half:] * (acc_hi + sh_ref[:, half:])

    last = n_tiles - 1
    for p in (0, 1):
        @pl.when(jnp.logical_and(i % 2 == p, i < last))
        def _(p=p):
            step(p, True)

    @pl.when(i == last)
    def _():
        step(last % 2, False)


def moe_combine(y_sorted, dest_t, wts_t, shared, h, mod3, geom, lat_only):
    m, d = h.shape
    bsz, L, n_ctx = geom
    T = COMB_TILE
    n_tiles = m // T
    if lat_only:
        lpb = (L - n_ctx) // T
        mod_row = lambda i: i // lpb
    else:
        mod_row = _mod_row_map(geom, T)
    smem_blk = lambda imap: pl.BlockSpec((1, TOP_K, T), imap, memory_space=pltpu.SMEM)
    return pl.pallas_call(
        functools.partial(_comb_kernel, n_tiles=n_tiles),
        grid=(n_tiles,),
        in_specs=[smem_blk(lambda i: (i, 0, 0)),
                  smem_blk(lambda i: (jnp.minimum(i + 1, n_tiles - 1), 0, 0)),
                  pl.BlockSpec(memory_space=pl.ANY),
                  pl.BlockSpec((T, TOP_K), lambda i: (i, 0)),
                  pl.BlockSpec((T, d), lambda i: (i, 0)),
                  pl.BlockSpec((T, d), lambda i: (i, 0)),
                  pl.BlockSpec((1, 6, d), lambda i: (mod_row(i), 0, 0))],
        out_specs=pl.BlockSpec((T, d), lambda i: (i, 0)),
        out_shape=jax.ShapeDtypeStruct((m, d), f32),
        scratch_shapes=[pltpu.VMEM((TOP_K, T, d // 2), jnp.uint32),
                        pltpu.VMEM((TOP_K, T, d // 2), jnp.uint32),
                        pltpu.SemaphoreType.DMA((2,))],
        compiler_params=_cparams("arbitrary"),
        name="moe_combine",
    )(dest_t, dest_t, y_sorted, wts_t, shared, h, mod3)


def moe_layer(h, g2, mod3, layer, rw, rbias, wg, wu, wd, wsg, wsu, wsd, geom, lat_only):
    m, d = h.shape
    E = N_EXPERTS
    mt, shared, top_e, wts, pos, cnt = moe_route(
        h, g2, mod3, rw.T, rbias, wsg.astype(bf16), wsu.astype(bf16), wsd.astype(bf16),
        geom, lat_only)
    counts = cnt[:, 0]
    padded = ((counts + MOE_BLK - 1) // MOE_BLK) * MOE_BLK
    pad_end = jnp.cumsum(padded)
    pad_start = pad_end - padded
    eids = jnp.arange(E, dtype=i32)[:, None, None]
    dest = pos + jnp.sum(jnp.where(top_e[None] == eids, pad_start[:, None, None], 0), axis=0)
    n_steps = (m * TOP_K) // MOE_BLK + E + 1
    tok = jnp.broadcast_to(jnp.arange(m, dtype=i32)[None, :], (TOP_K, m))
    slot_tok = jnp.zeros((n_steps * MOE_BLK,), i32).at[dest.reshape(-1)].set(tok.reshape(-1))
    block_start = jnp.arange(n_steps, dtype=i32) * MOE_BLK
    block_expert = jnp.minimum(
        jnp.sum((pad_end[None, :] <= block_start[:, None]).astype(i32), axis=1), E - 1)
    n_used = (pad_end[-1:] // MOE_BLK).astype(i32)
    y_sorted = moe_experts(mt, slot_tok, block_expert, n_used, wg, wu, wd, layer)
    T = COMB_TILE
    dest_t = dest.reshape(TOP_K, m // T, T).transpose(1, 0, 2)
    return moe_combine(y_sorted, dest_t, wts.T, shared, h, mod3, geom, lat_only)


def _rope_tables_t(n_ctx, n_lat):
    n_rows = n_lat // GRID_W
    rows = jnp.repeat(jnp.arange(n_rows, dtype=f32), GRID_W)
    cols = jnp.tile(jnp.arange(GRID_W, dtype=f32), n_rows)
    n_freq = ATT_HEADDIM // 4
    inv_freq = ROPE_BASE ** (-jnp.arange(n_freq, dtype=f32) / n_freq)
    ar = rows[None, :] * inv_freq[:, None]
    ac = cols[None, :] * inv_freq[:, None]
    cos_t = jnp.concatenate([jnp.cos(ar), jnp.cos(ar), jnp.cos(ac), jnp.cos(ac)], axis=0)
    sin_t = jnp.concatenate([-jnp.sin(ar), jnp.sin(ar), -jnp.sin(ac), jnp.sin(ac)], axis=0)
    cos_t = jnp.concatenate([jnp.ones((ATT_HEADDIM, n_ctx), f32), cos_t], axis=1)
    sin_t = jnp.concatenate([jnp.zeros((ATT_HEADDIM, n_ctx), f32), sin_t], axis=1)
    return cos_t, sin_t


def even_layer_mix(h, mod3, g1, w_in, gm_ws, gm_bs, gm_ln_g, gm_ln_b, conv_w, conv_b,
                   a_log, dt_bias, d_skip, ssd_norm_g, w_out, geom):
    a = norm_modulate(h, g1, mod3, geom, 0, 1)
    n_main = w_in.shape[1] - 2 * SSD_HEADS
    proj = matmul_nn(a, w_in[:, :n_main].astype(bf16), f32)
    w_dt = w_in[:, n_main:].reshape(-1, 2, SSD_GROUPS, SSD_HPG).transpose(2, 1, 3, 0)
    dt_t = dt_project(a, w_dt.reshape(2 * SSD_HEADS, -1))
    bexp = jnp.repeat(gm_bs.T, A_CHUNK, axis=1)
    g_out = gmlp_mix(proj, gm_ws.astype(bf16), bexp, gm_ln_g, gm_ln_b)
    y_out = ssd_mix(proj, dt_t, conv_w, conv_b, a_log, dt_bias, d_skip, ssd_norm_g, geom)
    aw = gm_ws.shape[0] * gm_ws.shape[1]
    w_o = w_out.astype(bf16)
    return out_project_joint([g_out, y_out], [w_o[:aw], w_o[aw:]], h, mod3, geom)


def odd_layer_mix_lat(h, mod3, g1, w_q, w_kv, q_norm_g, k_norm_g, sink, w_o, geom):
    bsz, L, n_ctx = geom
    a = norm_modulate(h, g1, mod3, geom, 0, 1)
    w_t = jnp.concatenate([w_q, w_kv], axis=1).T.astype(bf16)
    cos_t, sin_t = _rope_tables_t(n_ctx, L - n_ctx)
    qkv_t = qkv_project_t(a, w_t, q_norm_g, k_norm_g, cos_t, sin_t, geom)
    att = window_attention(qkv_t, sink, geom)
    return out_project_lat(att, w_o.astype(bf16), h, mod3, geom)


def kernel(x, c, ctx, c_ctx, mod_w, mod_b, norm1_g, norm2_g, ev_w_in, ev_gm_ws, ev_gm_bs, ev_gm_ln_g, ev_gm_ln_b, ev_conv_w, ev_conv_b, ev_a_log, ev_dt_bias, ev_d_skip, ev_ssd_norm_g, ev_w_out, od_w_q, od_w_kv, od_q_norm_g, od_k_norm_g, od_sink, od_w_o, moe_router_w, moe_router_bias, moe_w_gate, moe_w_up, moe_w_down, moe_ws_gate, moe_ws_up, moe_ws_down):
    bsz, n_lat, d = x.shape
    n_ctx = ctx.shape[1]
    L = n_ctx + n_lat
    geom = (bsz, L, n_ctx)
    depth = mod_w.shape[0]
    assert depth == 2 and n_ctx == ROW_TILE and n_lat % ROW_TILE == 0 and bsz < 16

    h = jnp.concatenate([ctx, x], axis=1).reshape(bsz * L, d)
    c_all = jnp.zeros((16, d), f32).at[:bsz].set(c).at[bsz].set(c_ctx)

    mod3 = modulation(c_all, mod_w, mod_b, 0).reshape(16, 6, d)
    h = even_layer_mix(h, mod3, norm1_g[0], ev_w_in[0], ev_gm_ws[0], ev_gm_bs[0], ev_gm_ln_g[0],
                       ev_gm_ln_b[0], ev_conv_w[0], ev_conv_b[0], ev_a_log[0], ev_dt_bias[0],
                       ev_d_skip[0], ev_ssd_norm_g[0], ev_w_out[0], geom)
    h = moe_layer(h, norm2_g[0], mod3, 0, moe_router_w[0], moe_router_bias[0], moe_w_gate,
                  moe_w_up, moe_w_down, moe_ws_gate[0], moe_ws_up[0], moe_ws_down[0],
                  geom, lat_only=False)

    mod3 = modulation(c_all, mod_w, mod_b, 1).reshape(16, 6, d)
    h_lat = odd_layer_mix_lat(h, mod3, norm1_g[1], od_w_q[0], od_w_kv[0], od_q_norm_g[0],
                              od_k_norm_g[0], od_sink[0], od_w_o[0], geom)
    h_lat = moe_layer(h_lat, norm2_g[1], mod3, 1, moe_router_w[1], moe_router_bias[1],
                      moe_w_gate, moe_w_up, moe_w_down, moe_ws_gate[1], moe_ws_up[1],
                      moe_ws_down[1], geom, lat_only=True)
    return h_lat.reshape(bsz, n_lat, d)
```

```python
import functools

import jax
import jax.numpy as jnp
from jax import lax
from jax.experimental import pallas as pl
from jax.experimental.pallas import tpu as pltpu

f32 = jnp.float32
bf16 = jnp.bfloat16
i32 = jnp.int32
HIGHEST = lax.Precision.HIGHEST

D_MODEL = 2048
EPS = 1e-6
GRID_W = 64

A_CHUNK = 128
A_GROUPS = 16

SSD_HEADS = 32
SSD_HEADDIM = 64
SSD_GROUPS = 8
SSD_STATE = 128
SSD_CHUNK = 128
SSD_HPG = SSD_HEADS // SSD_GROUPS
SSD_GW = SSD_HPG * SSD_HEADDIM

ATT_HEADDIM = 64
ATT_HEADS = 32
ATT_KV_HEADS = 4
ATT_Q_PER_KV = 8
ATT_WINDOW = 128
ATT_BLOCK = 128
ATT_SCALE = ATT_HEADDIM ** -0.5
ROPE_BASE = 10000.0

N_EXPERTS = 64
N_EXPERT_GROUPS = 8
GROUP_SIZE = N_EXPERTS // N_EXPERT_GROUPS
TOP_GROUPS = 4
TOP_K = 8
EXPERT_HIDDEN = 512
ROUTED_SCALE = 2.5

ROW_TILE = 256
BIG_TILE = 768
MOE_BLK = 256
COMB_TILE = 64
VMEM_LIMIT = 56 * 1024 * 1024

_NT = (((1,), (1,)), ((), ()))
_TN = (((0,), (0,)), ((), ()))


def _cparams(*sem):
    return pltpu.CompilerParams(dimension_semantics=sem, vmem_limit_bytes=VMEM_LIMIT)


def _pack_bf16_pair(x):
    c = x.shape[1] // 2
    lo = lax.bitcast_convert_type(x[:, :c].astype(bf16).astype(f32), jnp.uint32)
    hi = lax.bitcast_convert_type(x[:, c:].astype(bf16).astype(f32), jnp.uint32)
    return (lo >> 16) | hi


def _unpack_bf16_pair(p):
    lo = lax.bitcast_convert_type(p << 16, f32)
    hi = lax.bitcast_convert_type(p & jnp.uint32(0xFFFF0000), f32)
    return lo, hi


def _pick_tile(m, options):
    for t in options:
        if m % t == 0:
            return t
    raise ValueError(f"no tile for {m}")


def _mod_kernel(c_ref, w_ref, b_ref, o_ref):
    sc = jax.nn.silu(c_ref[...])
    o_ref[...] = jnp.dot(sc, w_ref[...], precision=HIGHEST,
                         preferred_element_type=f32) + b_ref[...]


def modulation(c_all, mod_w, mod_b, layer):
    rows, d = c_all.shape
    n = mod_w.shape[2]
    tn = 1024
    return pl.pallas_call(
        _mod_kernel,
        grid=(n // tn,),
        in_specs=[pl.BlockSpec((rows, d), lambda j: (0, 0)),
                  pl.BlockSpec((None, d, tn), lambda j: (layer, 0, j)),
                  pl.BlockSpec((None, 1, tn), lambda j: (layer, 0, j))],
        out_specs=pl.BlockSpec((rows, tn), lambda j: (0, j)),
        out_shape=jax.ShapeDtypeStruct((rows, n), f32),
        compiler_params=_cparams("arbitrary"),
        name="modulation",
    )(c_all, mod_w, mod_b.reshape(mod_b.shape[0], 1, n))


def _mod_row_map(geom, tile):
    bsz, L, n_ctx = geom
    tpb = L // tile
    nct = n_ctx // tile

    def mod_row(i):
        return jnp.where((i % tpb) < nct, bsz, i // tpb)
    return mod_row


def _lat_tile_map(geom, tile):
    bsz, L, n_ctx = geom
    lpb = (L - n_ctx) // tile
    tpb = L // tile
    nct = n_ctx // tile

    def joint(t):
        return (t // lpb) * tpb + nct + (t % lpb)
    return joint


def _norm_mod_kernel(x_ref, g_ref, mod_ref, o_ref, *, shift_idx, scale_idx):
    x = x_ref[...]
    ms = jnp.mean(x * x, axis=-1, keepdims=True)
    y = x * lax.rsqrt(ms + EPS) * g_ref[...]
    y = y * (1.0 + mod_ref[0, scale_idx:scale_idx + 1, :]) + mod_ref[0, shift_idx:shift_idx + 1, :]
    o_ref[...] = y.astype(o_ref.dtype)


def norm_modulate(h, g, mod3, geom, shift_idx, scale_idx):
    m, d = h.shape
    mod_row = _mod_row_map(geom, ROW_TILE)
    return pl.pallas_call(
        functools.partial(_norm_mod_kernel, shift_idx=shift_idx, scale_idx=scale_idx),
        grid=(m // ROW_TILE,),
        in_specs=[pl.BlockSpec((ROW_TILE, d), lambda i: (i, 0)),
                  pl.BlockSpec((1, d), lambda i: (0, 0)),
                  pl.BlockSpec((1, 6, d), lambda i: (mod_row(i), 0, 0))],
        out_specs=pl.BlockSpec((ROW_TILE, d), lambda i: (i, 0)),
        out_shape=jax.ShapeDtypeStruct((m, d), bf16),
        compiler_params=_cparams("parallel"),
        name="norm_modulate",
    )(h, g.reshape(1, d), mod3)


def _mm_kernel(a_ref, w_ref, o_ref):
    o_ref[...] = jnp.dot(a_ref[...], w_ref[...],
                         preferred_element_type=f32).astype(o_ref.dtype)


def matmul_nn(a, w, out_dtype, tn=1024):
    m, k = a.shape
    n = w.shape[1]
    tm = _pick_tile(m, (1024, 768, 512, 256))
    return pl.pallas_call(
        _mm_kernel,
        grid=(n // tn, m // tm),
        in_specs=[pl.BlockSpec((tm, k), lambda j, i: (i, 0)),
                  pl.BlockSpec((k, tn), lambda j, i: (0, j))],
        out_specs=pl.BlockSpec((tm, tn), lambda j, i: (i, j)),
        out_shape=jax.ShapeDtypeStruct((m, n), out_dtype),
        compiler_params=_cparams("parallel", "parallel"),
        name="matmul_nn",
    )(a, w)


def _dt_kernel(a_ref, w_ref, o_ref):
    o_ref[...] = lax.dot_general(w_ref[...], a_ref[...].astype(f32), _NT,
                                 precision=HIGHEST, preferred_element_type=f32)


def dt_project(a, w_dt_t):
    m, k = a.shape
    r = w_dt_t.shape[0]
    tm = _pick_tile(m, (1024, 768, 512, 256))
    return pl.pallas_call(
        _dt_kernel,
        grid=(m // tm,),
        in_specs=[pl.BlockSpec((tm, k), lambda i: (i, 0)),
                  pl.BlockSpec((r, k), lambda i: (0, 0))],
        out_specs=pl.BlockSpec((r, tm), lambda i: (0, i)),
        out_shape=jax.ShapeDtypeStruct((r, m), f32),
        compiler_params=_cparams("parallel"),
        name="dt_project",
    )(a, w_dt_t)


def _gelu_exact(x):
    return 0.5 * x * (1.0 + lax.erf(x * (2.0 ** -0.5)))


def _gmlp_kernel(u_ref, v_ref, ws_ref, bexp_ref, lng_ref, lnb_ref, o_ref):
    u = _gelu_exact(u_ref[...])
    v = _gelu_exact(v_ref[...])
    mu = jnp.mean(v, axis=-1, keepdims=True)
    vc = v - mu
    var = jnp.mean(vc * vc, axis=-1, keepdims=True)
    vn = (vc * lax.rsqrt(var + EPS) * lng_ref[...] + lnb_ref[...]).astype(bf16)
    for g in range(A_GROUPS):
        sl = slice(g * A_CHUNK, (g + 1) * A_CHUNK)
        mixed = jnp.dot(ws_ref[g], vn[:, sl], preferred_element_type=f32) + bexp_ref[:, sl]
        o_ref[:, sl] = (u[:, sl] * mixed).astype(o_ref.dtype)


def gmlp_mix(proj, ws_bf, bexp, ln_g, ln_b):
    m = proj.shape[0]
    w = A_GROUPS * A_CHUNK
    return pl.pallas_call(
        _gmlp_kernel,
        grid=(m // A_CHUNK,),
        in_specs=[pl.BlockSpec((A_CHUNK, w), lambda c: (c, 0)),
                  pl.BlockSpec((A_CHUNK, w), lambda c: (c, 1)),
                  pl.BlockSpec((A_GROUPS, A_CHUNK, A_CHUNK), lambda c: (0, 0, 0)),
                  pl.BlockSpec((A_CHUNK, w), lambda c: (0, 0)),
                  pl.BlockSpec((1, w), lambda c: (0, 0)),
                  pl.BlockSpec((1, w), lambda c: (0, 0))],
        out_specs=pl.BlockSpec((A_CHUNK, w), lambda c: (c, 0)),
        out_shape=jax.ShapeDtypeStruct((m, w), bf16),
        compiler_params=_cparams("parallel"),
        name="gmlp_mix",
    )(proj, proj, ws_bf, bexp, ln_g.reshape(1, w), ln_b.reshape(1, w))


def _head_expand(v, lane):
    out = v[:, 3:4]
    for r in (2, 1, 0):
        out = jnp.where(lane < SSD_HEADDIM * (r + 1), v[:, r:r + 1], out)
    return out


def _row_expand(v, row):
    out = v[:, 3:4]
    for r in (2, 1, 0):
        out = jnp.where(row < SSD_HEADDIM * (r + 1), v[:, r:r + 1], out)
    return out


def _ssd_kernel(x_ref, b_ref, c_ref, z_ref, dtr_ref, dtc_ref,
                cwx_ref, cwb_ref, cwc_ref, cbx_ref, cbb_ref, cbc_ref,
                alr_ref, alc_ref, dbr_ref, dbc_ref, dsk_ref, ng_ref,
                o_ref,
                xs_s, bs_s, cs_s, dtr_s, dtar_s, dtc_s, dtac_s, suf_s, yacc_s, sf_s, sb_s,
                *, n_ctx, L):
    Q = SSD_CHUNK
    nc = L // Q
    ncc = n_ctx // Q
    H = SSD_HPG

    bias_r = dbr_ref[...]
    a_r = -jnp.exp(alr_ref[...])
    for c in range(nc):
        dt = jax.nn.softplus(dtr_ref[:, c * Q:(c + 1) * Q] + bias_r)
        dtr_s[c] = dt
        dtar_s[c] = dt * a_r
    dtc = jax.nn.softplus(dtc_ref[0] + dbc_ref[0])
    dtc_s[...] = dtc
    dtac_s[...] = dtc * (-jnp.exp(alc_ref[0]))

    ii = lax.broadcasted_iota(i32, (Q, Q), 0)
    jj = lax.broadcasted_iota(i32, (Q, Q), 1)
    lower = ii >= jj
    upper = ii <= jj
    tril = lower.astype(f32)
    triu = upper.astype(f32)
    lane = lax.broadcasted_iota(i32, (Q, SSD_GW), 1)
    row = lax.broadcasted_iota(i32, (SSD_GW, SSD_STATE), 0)

    def conv_act(ref, w_ref, bias_ref, s):
        xv = ref[pl.ds(s, Q), :]
        w = xv.shape[1]
        r0 = lax.broadcasted_iota(i32, (Q, w), 0)
        prev = ref[pl.ds(jnp.maximum(s - 1, 0), 1), :]
        nxt = ref[pl.ds(jnp.minimum(s + Q, L - 1), 1), :]
        has_prev = jnp.logical_and(s != 0, s != n_ctx)
        has_next = jnp.logical_and(s + Q != n_ctx, s + Q != L)
        prev = jnp.where(has_prev, prev, 0.0)
        nxt = jnp.where(has_next, nxt, 0.0)
        xm1 = jnp.where(r0 == 0, prev, pltpu.roll(xv, 1, 0))
        xp1 = jnp.where(r0 == Q - 1, nxt, pltpu.roll(xv, Q - 1, 0))
        y = w_ref[0:1, :] * xm1 + w_ref[1:2, :] * xv + w_ref[2:3, :] * xp1 + bias_ref[...]
        return jax.nn.silu(y)

    sf_s[...] = jnp.zeros_like(sf_s)
    sb_s[...] = jnp.zeros_like(sb_s)

    def fwd_body(c, carry):
        s = pl.multiple_of(c * Q, Q)
        X = conv_act(x_ref, cwx_ref, cbx_ref, s)
        Bc = conv_act(b_ref, cwb_ref, cbb_ref, s)
        Cc = conv_act(c_ref, cwc_ref, cbc_ref, s)
        xs_s[pl.ds(s, Q), :] = X
        bs_s[pl.ds(s, Q), :] = Bc
        cs_s[pl.ds(s, Q), :] = Cc
        Bb = Bc.astype(bf16)
        Cb = Cc.astype(bf16)
        cb = lax.dot_general(Cb, Bb, _NT, preferred_element_type=f32)
        dta_c = dtac_s[pl.ds(s, Q), :]
        dta_r = dtar_s[c]
        dt_r = dtr_s[c]
        p_col = jnp.dot(tril, dta_c, precision=HIGHEST, preferred_element_type=f32)
        s_col = jnp.dot(triu, dta_c, precision=HIGHEST, preferred_element_type=f32)
        p_row = jnp.dot(dta_r, triu, precision=HIGHEST, preferred_element_type=f32)
        s_row = jnp.dot(dta_r, tril, precision=HIGHEST, preferred_element_type=f32)
        suf_s[pl.ds(s, Q), :] = s_col
        y = jnp.zeros((Q, SSD_GW), f32)
        for r in range(H):
            lf = jnp.exp(jnp.where(lower, p_col[:, r:r + 1] - p_row[r:r + 1, :], -jnp.inf))
            lb = jnp.exp(jnp.where(upper, s_col[:, H + r:H + r + 1] - s_row[H + r:H + r + 1, :],
                                   -jnp.inf))
            mr = cb * (lf * dt_r[r:r + 1, :] + lb * dt_r[H + r:H + r + 1, :])
            head = jnp.logical_and(lane >= SSD_HEADDIM * r, lane < SSD_HEADDIM * (r + 1))
            xm = jnp.where(head, X, 0.0).astype(bf16)
            y = y + jnp.dot(mr.astype(bf16), xm, preferred_element_type=f32)
        sf = sf_s[...]
        yi = lax.dot_general(Cb, sf.astype(bf16), _NT, preferred_element_type=f32)
        pf = p_col[:, 0:H]
        y = y + yi * _head_expand(jnp.exp(pf), lane)
        yacc_s[pl.ds(s, Q), :] = y
        tot = p_col[Q - 1:Q, 0:H]
        wf = dtc_s[pl.ds(s, Q), 0:H] * jnp.exp(tot - pf)
        xw = (X * _head_expand(wf, lane)).astype(bf16)
        upd = lax.dot_general(xw, Bb, _TN, preferred_element_type=f32)
        sf_s[...] = sf * _row_expand(jnp.exp(tot), row) + upd
        return carry

    lax.fori_loop(0, nc, fwd_body, 0, unroll=3)

    def bwd_body(k, carry):
        c = jnp.where(k < ncc, ncc - 1 - k, nc - 1 - (k - ncc))
        s = pl.multiple_of(c * Q, Q)
        X = xs_s[pl.ds(s, Q), :]
        Bb = bs_s[pl.ds(s, Q), :].astype(bf16)
        Cb = cs_s[pl.ds(s, Q), :].astype(bf16)
        ab = suf_s[pl.ds(s, Q), H:2 * H]
        tot = ab[0:1, :]
        sb = sb_s[...]
        yi = lax.dot_general(Cb, sb.astype(bf16), _NT, preferred_element_type=f32)
        y = yacc_s[pl.ds(s, Q), :] + yi * _head_expand(jnp.exp(ab), lane) + dsk_ref[...] * X
        wb = dtc_s[pl.ds(s, Q), H:2 * H] * jnp.exp(tot - ab)
        xw = (X * _head_expand(wb, lane)).astype(bf16)
        upd = lax.dot_general(xw, Bb, _TN, preferred_element_type=f32)
        sb_s[...] = sb * _row_expand(jnp.exp(tot), row) + upd
        y = y * jax.nn.silu(z_ref[pl.ds(s, Q), :])
        ms = jnp.mean(y * y, axis=-1, keepdims=True)
        o_ref[pl.ds(s, Q), :] = (y * lax.rsqrt(ms + EPS) * ng_ref[...]).astype(o_ref.dtype)
        return carry

    lax.fori_loop(0, nc, bwd_body, 0, unroll=2)


def ssd_mix(proj, dt_t, conv_w, conv_b, a_log, dt_bias, d_skip, norm_g, geom):
    bsz, L, n_ctx = geom
    m = proj.shape[0]
    G, H, GW, N = SSD_GROUPS, SSD_HPG, SSD_GW, SSD_STATE
    nc = L // SSD_CHUNK
    col0 = proj.shape[1] - (SSD_HEADS * SSD_HEADDIM + 2 * G * N)
    zc0 = col0 - SSD_HEADS * SSD_HEADDIM
    xb, zb = col0 // GW, zc0 // GW
    bb = (col0 + SSD_HEADS * SSD_HEADDIM) // N
    cb = bb + G
    cxb, cbb, ccb = 0, (SSD_HEADS * SSD_HEADDIM) // N, (SSD_HEADS * SSD_HEADDIM) // N + G

    def to_gdr(p):
        return p.reshape(2, G, H).transpose(1, 0, 2).reshape(G, 2 * H)

    al, db = to_gdr(a_log), to_gdr(dt_bias)
    dt_col = dt_t.reshape(G, 2 * H, m).transpose(0, 2, 1)
    dsk = jnp.repeat(d_skip, SSD_HEADDIM).reshape(1, -1)
    conv_b2 = conv_b.reshape(1, -1)
    kern = functools.partial(_ssd_kernel, n_ctx=n_ctx, L=L)
    return pl.pallas_call(
        kern,
        grid=(bsz, G),
        in_specs=[
            pl.BlockSpec((L, GW), lambda b, g: (b, xb + g)),
            pl.BlockSpec((L, N), lambda b, g: (b, bb + g)),
            pl.BlockSpec((L, N), lambda b, g: (b, cb + g)),
            pl.BlockSpec((L, GW), lambda b, g: (b, zb + g)),
            pl.BlockSpec((2 * H, L), lambda b, g: (g, b)),
            pl.BlockSpec((1, L, 2 * H), lambda b, g: (g, b, 0)),
            pl.BlockSpec((3, GW), lambda b, g: (0, cxb + g)),
            pl.BlockSpec((3, N), lambda b, g: (0, cbb + g)),
            pl.BlockSpec((3, N), lambda b, g: (0, ccb + g)),
            pl.BlockSpec((1, GW), lambda b, g: (0, cxb + g)),
            pl.BlockSpec((1, N), lambda b, g: (0, cbb + g)),
            pl.BlockSpec((1, N), lambda b, g: (0, ccb + g)),
            pl.BlockSpec((2 * H, 1), lambda b, g: (g, 0)),
            pl.BlockSpec((1, 1, 2 * H), lambda b, g: (g, 0, 0)),
            pl.BlockSpec((2 * H, 1), lambda b, g: (g, 0)),
            pl.BlockSpec((1, 1, 2 * H), lambda b, g: (g, 0, 0)),
            pl.BlockSpec((1, GW), lambda b, g: (0, g)),
            pl.BlockSpec((1, GW), lambda b, g: (0, g)),
        ],
        out_specs=pl.BlockSpec((L, GW), lambda b, g: (b, g)),
        out_shape=jax.ShapeDtypeStruct((m, G * GW), bf16),
        scratch_shapes=[
            pltpu.VMEM((L, GW), f32), pltpu.VMEM((L, N), f32), pltpu.VMEM((L, N), f32),
            pltpu.VMEM((nc, 2 * H, SSD_CHUNK), f32), pltpu.VMEM((nc, 2 * H, SSD_CHUNK), f32),
            pltpu.VMEM((L, 2 * H), f32), pltpu.VMEM((L, 2 * H), f32), pltpu.VMEM((L, 2 * H), f32),
            pltpu.VMEM((L, GW), f32),
            pltpu.VMEM((GW, N), f32), pltpu.VMEM((GW, N), f32),
        ],
        compiler_params=_cparams("parallel", "parallel"),
        name="ssd_mix",
    )(proj, proj, proj, proj, dt_t, dt_col,
      conv_w, conv_w, conv_w, conv_b2, conv_b2, conv_b2,
      al.reshape(G * 2 * H, 1), al.reshape(G, 1, 2 * H),
      db.reshape(G * 2 * H, 1), db.reshape(G, 1, 2 * H),
      dsk, norm_g.reshape(1, -1))


def _outproj_kernel(*refs, n_in, n_ctx, tpb, tm):
    a_refs = refs[:n_in]
    w_refs = refs[n_in:2 * n_in]
    h_ref, ml_ref, mc_ref, o_ref = refs[2 * n_in:]
    i = pl.program_id(1)
    acc = jnp.dot(a_refs[0][...], w_refs[0][...], preferred_element_type=f32)
    for k in range(1, n_in):
        acc = acc + jnp.dot(a_refs[k][...], w_refs[k][...], preferred_element_type=f32)
    n_ctx_rows = jnp.clip(n_ctx - (i % tpb) * tm, 0, tm)
    row = lax.broadcasted_iota(i32, acc.shape, 0)
    gate = jnp.where(row < n_ctx_rows, mc_ref[0, 2:3, :], ml_ref[0, 2:3, :])
    o_ref[...] = h_ref[...] + gate * acc


def out_project_joint(a_list, w_list, h, mod3, geom, tn=1024):
    bsz, L, n_ctx = geom
    m, d = h.shape
    tm = _pick_tile(L, (BIG_TILE, 512, 256))
    tpb = L // tm
    n_in = len(a_list)
    k = a_list[0].shape[1]
    kern = functools.partial(_outproj_kernel, n_in=n_in, n_ctx=n_ctx, tpb=tpb, tm=tm)
    in_specs = ([pl.BlockSpec((tm, k), lambda j, i: (i, 0)) for _ in range(n_in)]
                + [pl.BlockSpec((k, tn), lambda j, i: (0, j)) for _ in range(n_in)]
                + [pl.BlockSpec((tm, tn), lambda j, i: (i, j)),
                   pl.BlockSpec((1, 6, tn), lambda j, i: (i // tpb, 0, j)),
                   pl.BlockSpec((1, 6, tn), lambda j, i: (bsz, 0, j))])
    return pl.pallas_call(
        kern,
        grid=(d // tn, m // tm),
        in_specs=in_specs,
        out_specs=pl.BlockSpec((tm, tn), lambda j, i: (i, j)),
        out_shape=jax.ShapeDtypeStruct((m, d), f32),
        compiler_params=_cparams("parallel", "parallel"),
        name="out_project_joint",
    )(*a_list, *w_list, h, mod3, mod3)


def _outproj_lat_kernel(a_ref, w_ref, h_ref, ml_ref, o_ref):
    acc = jnp.dot(a_ref[...], w_ref[...], preferred_element_type=f32)
    o_ref[...] = h_ref[...] + ml_ref[0, 2:3, :] * acc


def out_project_lat(a, w, h_joint, mod3, geom, tn=1024):
    bsz, L, n_ctx = geom
    m, k = a.shape
    d = h_joint.shape[1]
    tm = ROW_TILE
    joint = _lat_tile_map(geom, tm)
    lpb = (L - n_ctx) // tm
    return pl.pallas_call(
        _outproj_lat_kernel,
        grid=(d // tn, m // tm),
        in_specs=[pl.BlockSpec((tm, k), lambda j, i: (i, 0)),
                  pl.BlockSpec((k, tn), lambda j, i: (0, j)),
                  pl.BlockSpec((tm, tn), lambda j, i: (joint(i), j)),
                  pl.BlockSpec((1, 6, tn), lambda j, i: (i // lpb, 0, j))],
        out_specs=pl.BlockSpec((tm, tn), lambda j, i: (i, j)),
        out_shape=jax.ShapeDtypeStruct((m, d), f32),
        compiler_params=_cparams("parallel", "parallel"),
        name="out_project_lat",
    )(a, w, h_joint, mod3)


def _qkv_kernel(a_ref, w_ref, gq_ref, gk_ref, cos_ref, sin_ref, o_ref, *, n_qblk):
    j = pl.program_id(1)
    acc = lax.dot_general(w_ref[...], a_ref[...], _NT, preferred_element_type=f32)

    @pl.when(j <= n_qblk)
    def _():
        g = jnp.where(j < n_qblk, gq_ref[...] * ATT_SCALE, gk_ref[...])
        cs = cos_ref[...]
        sn = sin_ref[...]
        hd = ATT_HEADDIM
        q4 = hd // 4
        for hh in range(acc.shape[0] // hd):
            blk = acc[hh * hd:(hh + 1) * hd, :]
            ms = jnp.mean(blk * blk, axis=0, keepdims=True)
            xn = blk * lax.rsqrt(ms + EPS) * g
            sw = jnp.concatenate([xn[q4:2 * q4], xn[0:q4], xn[3 * q4:4 * q4], xn[2 * q4:3 * q4]],
                                 axis=0)
            o_ref[hh * hd:(hh + 1) * hd, :] = xn * cs + sw * sn

    @pl.when(j > n_qblk)
    def _():
        o_ref[...] = acc


def qkv_project_t(a, w_t, gq, gk, cos_t, sin_t, geom):
    bsz, L, n_ctx = geom
    m, k = a.shape
    r = w_t.shape[0]
    rb = 256
    tm = _pick_tile(L, (BIG_TILE, 512, 256))
    tpb = L // tm
    n_qblk = (ATT_HEADS * ATT_HEADDIM) // rb
    return pl.pallas_call(
        functools.partial(_qkv_kernel, n_qblk=n_qblk),
        grid=(m // tm, r // rb),
        in_specs=[pl.BlockSpec((tm, k), lambda i, j: (i, 0)),
                  pl.BlockSpec((rb, k), lambda i, j: (j, 0)),
                  pl.BlockSpec((ATT_HEADDIM, 1), lambda i, j: (0, 0)),
                  pl.BlockSpec((ATT_HEADDIM, 1), lambda i, j: (0, 0)),
                  pl.BlockSpec((ATT_HEADDIM, tm), lambda i, j: (0, i % tpb)),
                  pl.BlockSpec((ATT_HEADDIM, tm), lambda i, j: (0, i % tpb))],
        out_specs=pl.BlockSpec((rb, tm), lambda i, j: (j, i)),
        out_shape=jax.ShapeDtypeStruct((r, m), f32),
        compiler_params=_cparams("parallel", "parallel"),
        name="qkv_project_t",
    )(a, w_t, gq.reshape(-1, 1), gk.reshape(-1, 1), cos_t, sin_t)


def _attn_kernel(sink_ref, q_ref, kc_ref, k0_ref, k1_ref, k2_ref,
                 vc_ref, v0_ref, v1_ref, v2_ref, o_ref, ot_s, *, n_lat, n_ctx):
    qi = pl.program_id(1)
    kvh = pl.program_id(2)
    blk = ATT_BLOCK
    hd = ATT_HEADDIM
    kt = jnp.concatenate([kc_ref[...], k0_ref[...], k1_ref[...], k2_ref[...]], axis=1)
    vt = jnp.concatenate([vc_ref[...], v0_ref[...], v1_ref[...], v2_ref[...]], axis=1)
    kt = kt.astype(bf16)
    vt = vt.astype(bf16)
    R = ATT_Q_PER_KV
    t = lax.broadcasted_iota(i32, (blk, blk), 0)
    c = lax.broadcasted_iota(i32, (blk, blk), 1)
    m_prev = jnp.logical_and(t >= c, qi >= 1)
    m_next = jnp.logical_and(t <= c, (qi + 1) * blk < n_lat)
    q_all = jnp.concatenate([q_ref[r * hd:(r + 1) * hd, :].astype(bf16) for r in range(R)],
                            axis=1)
    s_all = lax.dot_general(kt, q_all, _TN, preferred_element_type=f32)
    p_list, dens = [], []
    for r in range(R):
        s = s_all[:, r * blk:(r + 1) * blk]
        parts = [s[:n_ctx],
                 jnp.where(m_prev, s[n_ctx:n_ctx + blk], -jnp.inf),
                 s[n_ctx + blk:n_ctx + 2 * blk],
                 jnp.where(m_next, s[n_ctx + 2 * blk:], -jnp.inf)]
        sink = sink_ref[kvh * R + r]
        mx = sink
        for x in parts:
            mx = jnp.maximum(jnp.max(x, axis=0, keepdims=True), mx)
        ps = [jnp.exp(x - mx) for x in parts]
        den = jnp.exp(sink - mx)
        for x in ps:
            den = den + jnp.sum(x, axis=0, keepdims=True)
        p_list.append(jnp.concatenate(ps, axis=0).astype(bf16))
        dens.append(den)
    p_all = jnp.concatenate(p_list, axis=1)
    o_all = jnp.dot(vt, p_all, preferred_element_type=f32)
    for r in range(R):
        ot_s[r * hd:(r + 1) * hd, :] = o_all[:, r * blk:(r + 1) * blk] / dens[r]
    o_ref[...] = ot_s[...].T.astype(o_ref.dtype)


def window_attention(qkv_t, sink, geom):
    bsz, L, n_ctx = geom
    n_lat = L - n_ctx
    blk = ATT_BLOCK
    hd = ATT_HEADDIM
    nqb = n_lat // blk
    cpb = L // blk
    cb0 = n_ctx // blk
    q_rows = ATT_Q_PER_KV * hd
    k_rb = (ATT_HEADS * hd) // hd
    v_rb = k_rb + ATT_KV_HEADS

    def band(rb0, off):
        def imap(b, qi, kvh, sink_ref):
            kb = jnp.clip(qi + off, 0, nqb - 1)
            return (rb0 + kvh, b * cpb + cb0 + kb)
        return pl.BlockSpec((hd, blk), imap)

    def ctx(rb0):
        return pl.BlockSpec((hd, n_ctx), lambda b, qi, kvh, s: (rb0 + kvh, b * (L // n_ctx)))

    kern = functools.partial(_attn_kernel, n_lat=n_lat, n_ctx=n_ctx)
    grid_spec = pltpu.PrefetchScalarGridSpec(
        num_scalar_prefetch=1,
        grid=(bsz, nqb, ATT_KV_HEADS),
        in_specs=[pl.BlockSpec((q_rows, blk), lambda b, qi, kvh, s: (kvh, b * cpb + cb0 + qi)),
                  ctx(k_rb), band(k_rb, -1), band(k_rb, 0), band(k_rb, 1),
                  ctx(v_rb), band(v_rb, -1), band(v_rb, 0), band(v_rb, 1)],
        out_specs=pl.BlockSpec((blk, q_rows), lambda b, qi, kvh, s: (b * nqb + qi, kvh)),
        scratch_shapes=[pltpu.VMEM((q_rows, blk), f32)],
    )
    return pl.pallas_call(
        kern,
        grid_spec=grid_spec,
        out_shape=jax.ShapeDtypeStruct((bsz * n_lat, ATT_HEADS * hd), bf16),
        compiler_params=_cparams("parallel", "parallel", "parallel"),
        name="window_attention",
    )(sink, *([qkv_t] * 9))


def _first_max(v, idx, n):
    m = jnp.max(v, axis=0, keepdims=True)
    first = jnp.min(jnp.where(v == m, idx, n), axis=0, keepdims=True)
    return m, first


def _route_kernel(h_ref, g_ref, mod_ref, rw_ref, rb_ref, wsg_ref, wsu_ref, wsd_ref,
                  m_ref, sh_ref, te_ref, wt_ref, pos_ref, cnt_ref, carry_s):
    i = pl.program_id(0)
    T = h_ref.shape[0]
    E, NG, GS = N_EXPERTS, N_EXPERT_GROUPS, GROUP_SIZE

    @pl.when(i == 0)
    def _():
        carry_s[...] = jnp.zeros_like(carry_s)

    x = h_ref[...]
    ms = jnp.mean(x * x, axis=-1, keepdims=True)
    y = x * lax.rsqrt(ms + EPS) * g_ref[...]
    mt = y * (1.0 + mod_ref[0, 4:5, :]) + mod_ref[0, 3:4, :]
    m_ref[...] = _pack_bf16_pair(mt)
    mb = mt.astype(bf16)

    hg = jnp.dot(mb, wsg_ref[...], preferred_element_type=f32)
    hu = jnp.dot(mb, wsu_ref[...], preferred_element_type=f32)
    act = (jax.nn.silu(hg) * hu).astype(bf16)
    sh_ref[...] = jnp.dot(act, wsd_ref[...], preferred_element_type=f32)

    logits = lax.dot_general(rw_ref[...], mt, _NT, precision=HIGHEST,
                             preferred_element_type=f32)
    scores = jax.nn.sigmoid(logits)
    sel = scores + rb_ref[...]
    eidx = lax.broadcasted_iota(i32, (E, T), 0)
    midx = lax.broadcasted_iota(i32, (GS, T), 0)
    gidx = lax.broadcasted_iota(i32, (NG, T), 0)

    gs = jnp.zeros((NG, T), f32)
    for g in range(NG):
        blk = sel[g * GS:(g + 1) * GS, :]
        m1, f1 = _first_max(blk, midx, GS)
        m2 = jnp.max(jnp.where(midx == f1, -jnp.inf, blk), axis=0, keepdims=True)
        gs = jnp.where(gidx == g, m1 + m2, gs)
    gmask = jnp.zeros((NG, T), jnp.bool_)
    work = gs
    for _ in range(TOP_GROUPS):
        _, f = _first_max(work, gidx, NG)
        hit = gidx == f
        gmask = jnp.logical_or(gmask, hit)
        work = jnp.where(hit, -jnp.inf, work)
    cand = jnp.concatenate(
        [jnp.where(gmask[g:g + 1, :], sel[g * GS:(g + 1) * GS, :], -jnp.inf) for g in range(NG)],
        axis=0)
    hits = []
    chosen = jnp.zeros((E, T), jnp.bool_)
    work = cand
    for k in range(TOP_K):
        _, f = _first_max(work, eidx, E)
        hit = eidx == f
        hits.append(hit)
        chosen = jnp.logical_or(chosen, hit)
        work = jnp.where(hit, -jnp.inf, work)
        te_ref[k:k + 1, :] = f
    wts = [jnp.sum(jnp.where(h, scores, 0.0), axis=0, keepdims=True) for h in hits]
    tot = wts[0]
    for w in wts[1:]:
        tot = tot + w
    for k in range(TOP_K):
        wt_ref[k:k + 1, :] = wts[k] / (tot + 1e-20) * ROUTED_SCALE

    t0 = lax.broadcasted_iota(i32, (T, T), 0)
    t1 = lax.broadcasted_iota(i32, (T, T), 1)
    before = (t0 < t1).astype(bf16)
    chosen_f = chosen.astype(f32)
    cnt = jnp.dot(chosen_f.astype(bf16), before, preferred_element_type=f32) + carry_s[:, 0:1]
    for k in range(TOP_K):
        pos_ref[k:k + 1, :] = jnp.sum(jnp.where(hits[k], cnt, 0.0), axis=0,
                                      keepdims=True).astype(i32)
    new_carry = carry_s[...] + jnp.sum(chosen_f, axis=1, keepdims=True)
    carry_s[...] = new_carry
    cnt_ref[...] = new_carry.astype(i32)


def moe_route(h, g, mod3, rw_t, rbias, wsg, wsu, wsd, geom, lat_only):
    m, d = h.shape
    bsz, L, n_ctx = geom
    T = ROW_TILE
    if lat_only:
        lpb = (L - n_ctx) // T
        mod_row = lambda i: i // lpb
    else:
        mod_row = _mod_row_map(geom, T)
    E = N_EXPERTS
    hid = wsg.shape[1]
    const2 = lambda i: (0, 0)
    outs = pl.pallas_call(
        _route_kernel,
        grid=(m // T,),
        in_specs=[pl.BlockSpec((T, d), lambda i: (i, 0)),
                  pl.BlockSpec((1, d), const2),
                  pl.BlockSpec((1, 6, d), lambda i: (mod_row(i), 0, 0)),
                  pl.BlockSpec((E, d), const2),
                  pl.BlockSpec((E, 1), const2),
                  pl.BlockSpec((d, hid), const2),
                  pl.BlockSpec((d, hid), const2),
                  pl.BlockSpec((hid, d), const2)],
        out_specs=[pl.BlockSpec((T, d // 2), lambda i: (i, 0)),
                   pl.BlockSpec((T, d), lambda i: (i, 0)),
                   pl.BlockSpec((TOP_K, T), lambda i: (0, i)),
                   pl.BlockSpec((TOP_K, T), lambda i: (0, i)),
                   pl.BlockSpec((TOP_K, T), lambda i: (0, i)),
                   pl.BlockSpec((E, 128), const2)],
        out_shape=[jax.ShapeDtypeStruct((m, d // 2), jnp.uint32),
                   jax.ShapeDtypeStruct((m, d), f32),
                   jax.ShapeDtypeStruct((TOP_K, m), i32),
                   jax.ShapeDtypeStruct((TOP_K, m), f32),
                   jax.ShapeDtypeStruct((TOP_K, m), i32),
                   jax.ShapeDtypeStruct((E, 128), i32)],
        scratch_shapes=[pltpu.VMEM((E, 128), f32)],
        compiler_params=_cparams("arbitrary"),
        name="moe_route",
    )(h, g.reshape(1, d), mod3, rw_t, rbias.reshape(E, 1), wsg, wsu, wsd)
    return outs


GATHER_AHEAD = 2
GATHER_SLOTS = GATHER_AHEAD + 1


def _gmm_kernel(be_ref, nused_ref, tokc_ref, tokn_ref, tokf_ref, m_hbm, wg_ref, wu_ref,
                wd_ref, o_ref, xbuf, wg_s, wu_s, wd_s, sem):
    i = pl.program_id(0)
    nused = nused_ref[0]
    slot = i % GATHER_SLOTS
    far_slot = (i + GATHER_AHEAD) % GATHER_SLOTS

    def start_gather(tok_ref, dst_slot):
        for r in range(MOE_BLK):
            t = tok_ref[0, 0, r]
            pltpu.make_async_copy(m_hbm.at[pl.ds(t, 1), :],
                                  xbuf.at[dst_slot, pl.ds(r, 1), :],
                                  sem.at[dst_slot]).start()

    def wait_gather(dst_slot):
        pltpu.make_async_copy(m_hbm.at[pl.ds(0, MOE_BLK), :], xbuf.at[dst_slot],
                              sem.at[dst_slot]).wait()

    @pl.when(jnp.logical_and(i == 0, nused > 0))
    def _():
        start_gather(tokc_ref, 0)
        start_gather(tokn_ref, 1)

    prev = be_ref[jnp.maximum(i - 1, 0)]
    new_expert = jnp.logical_or(i == 0, be_ref[i] != prev)

    @pl.when(jnp.logical_and(i < nused, new_expert))
    def _():
        wg_s[...] = wg_ref[...].astype(bf16)
        wu_s[...] = wu_ref[...].astype(bf16)
        wd_s[...] = wd_ref[...].astype(bf16)

    @pl.when(i < nused)
    def _():
        wait_gather(slot)
        x_lo, x_hi = _unpack_bf16_pair(xbuf[slot])
        x_lo = x_lo.astype(bf16)
        x_hi = x_hi.astype(bf16)
        start_gather(tokf_ref, far_slot)
        kh = x_lo.shape[1]
        hg = (jnp.dot(x_lo, wg_s[:kh, :], preferred_element_type=f32)
              + jnp.dot(x_hi, wg_s[kh:, :], preferred_element_type=f32))
        hu = (jnp.dot(x_lo, wu_s[:kh, :], preferred_element_type=f32)
              + jnp.dot(x_hi, wu_s[kh:, :], preferred_element_type=f32))
        act = (jax.nn.silu(hg) * hu).astype(bf16)
        o_ref[...] = _pack_bf16_pair(jnp.dot(act, wd_s[...], preferred_element_type=f32))

    @pl.when(i >= nused)
    def _():
        o_ref[...] = jnp.zeros_like(o_ref)

    @pl.when(jnp.logical_and(jnp.logical_and(i >= nused, i < nused + GATHER_AHEAD), nused > 0))
    def _():
        wait_gather(slot)


def moe_experts(m_rows, slot_tok, block_expert, n_used, wg, wu, wd, layer):
    n_steps = block_expert.shape[0]
    d = wg.shape[2]
    dp = m_rows.shape[1]
    hid = wg.shape[3]
    tok3 = slot_tok.reshape(n_steps, 1, MOE_BLK)
    smem_blk = lambda imap: pl.BlockSpec((1, 1, MOE_BLK), imap, memory_space=pltpu.SMEM)
    grid_spec = pltpu.PrefetchScalarGridSpec(
        num_scalar_prefetch=2,
        grid=(n_steps,),
        in_specs=[smem_blk(lambda i, be, nu: (i, 0, 0)),
                  smem_blk(lambda i, be, nu: (jnp.minimum(i + 1, n_steps - 1), 0, 0)),
                  smem_blk(lambda i, be, nu: (jnp.minimum(i + GATHER_AHEAD, n_steps - 1), 0, 0)),
                  pl.BlockSpec(memory_space=pl.ANY),
                  pl.BlockSpec((None, None, d, hid), lambda i, be, nu: (layer, be[i], 0, 0)),
                  pl.BlockSpec((None, None, d, hid), lambda i, be, nu: (layer, be[i], 0, 0)),
                  pl.BlockSpec((None, None, hid, d), lambda i, be, nu: (layer, be[i], 0, 0))],
        out_specs=pl.BlockSpec((MOE_BLK, dp), lambda i, be, nu: (i, 0)),
        scratch_shapes=[pltpu.VMEM((GATHER_SLOTS, MOE_BLK, dp), jnp.uint32),
                        pltpu.VMEM((d, hid), bf16), pltpu.VMEM((d, hid), bf16),
                        pltpu.VMEM((hid, d), bf16),
                        pltpu.SemaphoreType.DMA((GATHER_SLOTS,))],
    )
    return pl.pallas_call(
        _gmm_kernel,
        grid_spec=grid_spec,
        out_shape=jax.ShapeDtypeStruct((n_steps * MOE_BLK, dp), jnp.uint32),
        compiler_params=_cparams("arbitrary"),
        name="moe_experts",
    )(block_expert, n_used, tok3, tok3, tok3, m_rows, wg, wu, wd)


def _comb_kernel(dc_ref, dn_ref, df_ref, ys_hbm, w_ref, sh_ref, h_ref, mod_ref, o_ref,
                 gbuf0, gbuf1, gbuf2, sem, *, n_tiles):
    i = pl.program_id(0)
    T = COMB_TILE
    gbufs = (gbuf0, gbuf1, gbuf2)
    S = GATHER_SLOTS

    def start_gather(d_ref, p):
        for r in range(T):
            for k in range(TOP_K):
                t = d_ref[0, k, r]
                pltpu.make_async_copy(ys_hbm.at[pl.ds(t, 1), :],
                                      gbufs[p].at[k, pl.ds(r, 1), :],
                                      sem.at[p]).start()

    @pl.when(i == 0)
    def _():
        start_gather(dc_ref, 0)
        start_gather(dn_ref, 1)

    def step(p, issue_far):
        for k in range(TOP_K):
            pltpu.make_async_copy(ys_hbm.at[pl.ds(0, T), :], gbufs[p].at[k],
                                  sem.at[p]).wait()
        if issue_far:
            start_gather(df_ref, (p + GATHER_AHEAD) % S)
        w = w_ref[...]
        half = o_ref.shape[1] // 2
        acc_lo = jnp.zeros((T, half), f32)
        acc_hi = jnp.zeros((T, half), f32)
        for k in range(TOP_K):
            lo, hi = _unpack_bf16_pair(gbufs[p][k])
            acc_lo = acc_lo + lo * w[:, k:k + 1]
            acc_hi = acc_hi + hi * w[:, k:k + 1]
        gate = mod_ref[0, 5:6, :]
        o_ref[:, :half] = h_ref[:, :half] + gate[:, :half] * (acc_lo + sh_ref[:, :half])
        o_ref[:, half:] = h_ref[:, half:] + gate[:, half:] * (acc_hi + sh_ref[:, half:])

    first_tail = n_tiles - GATHER_AHEAD
    for p in range(S):
        @pl.when(jnp.logical_and(i % S == p, i < first_tail))
        def _(p=p):
            step(p, True)

    for tail in range(first_tail, n_tiles):
        @pl.when(i == tail)
        def _(tail=tail):
            step(tail % S, False)


def moe_combine(y_sorted, dest_t, wts_t, shared, h, mod3, geom, lat_only):
    m, d = h.shape
    bsz, L, n_ctx = geom
    T = COMB_TILE
    n_tiles = m // T
    if lat_only:
        lpb = (L - n_ctx) // T
        mod_row = lambda i: i // lpb
    else:
        mod_row = _mod_row_map(geom, T)
    smem_blk = lambda imap: pl.BlockSpec((1, TOP_K, T), imap, memory_space=pltpu.SMEM)
    return pl.pallas_call(
        functools.partial(_comb_kernel, n_tiles=n_tiles),
        grid=(n_tiles,),
        in_specs=[smem_blk(lambda i: (i, 0, 0)),
                  smem_blk(lambda i: (jnp.minimum(i + 1, n_tiles - 1), 0, 0)),
                  smem_blk(lambda i: (jnp.minimum(i + GATHER_AHEAD, n_tiles - 1), 0, 0)),
                  pl.BlockSpec(memory_space=pl.ANY),
                  pl.BlockSpec((T, TOP_K), lambda i: (i, 0)),
                  pl.BlockSpec((T, d), lambda i: (i, 0)),
                  pl.BlockSpec((T, d), lambda i: (i, 0)),
                  pl.BlockSpec((1, 6, d), lambda i: (mod_row(i), 0, 0))],
        out_specs=pl.BlockSpec((T, d), lambda i: (i, 0)),
        out_shape=jax.ShapeDtypeStruct((m, d), f32),
        scratch_shapes=[pltpu.VMEM((TOP_K, T, d // 2), jnp.uint32)] * GATHER_SLOTS
        + [pltpu.SemaphoreType.DMA((GATHER_SLOTS,))],
        compiler_params=_cparams("arbitrary"),
        name="moe_combine",
    )(dest_t, dest_t, dest_t, y_sorted, wts_t, shared, h, mod3)


def moe_layer(h, g2, mod3, layer, rw, rbias, wg, wu, wd, wsg, wsu, wsd, geom, lat_only):
    m, d = h.shape
    E = N_EXPERTS
    mt, shared, top_e, wts, pos, cnt = moe_route(
        h, g2, mod3, rw.T, rbias, wsg.astype(bf16), wsu.astype(bf16), wsd.astype(bf16),
        geom, lat_only)
    counts = cnt[:, 0]
    padded = ((counts + MOE_BLK - 1) // MOE_BLK) * MOE_BLK
    pad_end = jnp.cumsum(padded)
    pad_start = pad_end - padded
    eids = jnp.arange(E, dtype=i32)[:, None, None]
    dest = pos + jnp.sum(jnp.where(top_e[None] == eids, pad_start[:, None, None], 0), axis=0)
    n_steps = (m * TOP_K) // MOE_BLK + E + GATHER_AHEAD
    tok = jnp.broadcast_to(jnp.arange(m, dtype=i32)[None, :], (TOP_K, m))
    slot_tok = jnp.zeros((n_steps * MOE_BLK,), i32).at[dest.reshape(-1)].set(tok.reshape(-1))
    block_start = jnp.arange(n_steps, dtype=i32) * MOE_BLK
    block_expert = jnp.minimum(
        jnp.sum((pad_end[None, :] <= block_start[:, None]).astype(i32), axis=1), E - 1)
    n_used = (pad_end[-1:] // MOE_BLK).astype(i32)
    y_sorted = moe_experts(mt, slot_tok, block_expert, n_used, wg, wu, wd, layer)
    T = COMB_TILE
    dest_t = dest.reshape(TOP_K, m // T, T).transpose(1, 0, 2)
    return moe_combine(y_sorted, dest_t, wts.T, shared, h, mod3, geom, lat_only)


def _rope_tables_t(n_ctx, n_lat):
    n_rows = n_lat // GRID_W
    rows = jnp.repeat(jnp.arange(n_rows, dtype=f32), GRID_W)
    cols = jnp.tile(jnp.arange(GRID_W, dtype=f32), n_rows)
    n_freq = ATT_HEADDIM // 4
    inv_freq = ROPE_BASE ** (-jnp.arange(n_freq, dtype=f32) / n_freq)
    ar = rows[None, :] * inv_freq[:, None]
    ac = cols[None, :] * inv_freq[:, None]
    cos_t = jnp.concatenate([jnp.cos(ar), jnp.cos(ar), jnp.cos(ac), jnp.cos(ac)], axis=0)
    sin_t = jnp.concatenate([-jnp.sin(ar), jnp.sin(ar), -jnp.sin(ac), jnp.sin(ac)], axis=0)
    cos_t = jnp.concatenate([jnp.ones((ATT_HEADDIM, n_ctx), f32), cos_t], axis=1)
    sin_t = jnp.concatenate([jnp.zeros((ATT_HEADDIM, n_ctx), f32), sin_t], axis=1)
    return cos_t, sin_t


def even_layer_mix(h, mod3, g1, w_in, gm_ws, gm_bs, gm_ln_g, gm_ln_b, conv_w, conv_b,
                   a_log, dt_bias, d_skip, ssd_norm_g, w_out, geom):
    a = norm_modulate(h, g1, mod3, geom, 0, 1)
    n_main = w_in.shape[1] - 2 * SSD_HEADS
    proj = matmul_nn(a, w_in[:, :n_main].astype(bf16), f32)
    w_dt = w_in[:, n_main:].reshape(-1, 2, SSD_GROUPS, SSD_HPG).transpose(2, 1, 3, 0)
    dt_t = dt_project(a, w_dt.reshape(2 * SSD_HEADS, -1))
    bexp = jnp.repeat(gm_bs.T, A_CHUNK, axis=1)
    g_out = gmlp_mix(proj, gm_ws.astype(bf16), bexp, gm_ln_g, gm_ln_b)
    y_out = ssd_mix(proj, dt_t, conv_w, conv_b, a_log, dt_bias, d_skip, ssd_norm_g, geom)
    aw = gm_ws.shape[0] * gm_ws.shape[1]
    w_o = w_out.astype(bf16)
    return out_project_joint([g_out, y_out], [w_o[:aw], w_o[aw:]], h, mod3, geom)


def odd_layer_mix_lat(h, mod3, g1, w_q, w_kv, q_norm_g, k_norm_g, sink, w_o, geom):
    bsz, L, n_ctx = geom
    a = norm_modulate(h, g1, mod3, geom, 0, 1)
    w_t = jnp.concatenate([w_q, w_kv], axis=1).T.astype(bf16)
    cos_t, sin_t = _rope_tables_t(n_ctx, L - n_ctx)
    qkv_t = qkv_project_t(a, w_t, q_norm_g, k_norm_g, cos_t, sin_t, geom)
    att = window_attention(qkv_t, sink, geom)
    return out_project_lat(att, w_o.astype(bf16), h, mod3, geom)


def kernel(x, c, ctx, c_ctx, mod_w, mod_b, norm1_g, norm2_g, ev_w_in, ev_gm_ws, ev_gm_bs, ev_gm_ln_g, ev_gm_ln_b, ev_conv_w, ev_conv_b, ev_a_log, ev_dt_bias, ev_d_skip, ev_ssd_norm_g, ev_w_out, od_w_q, od_w_kv, od_q_norm_g, od_k_norm_g, od_sink, od_w_o, moe_router_w, moe_router_bias, moe_w_gate, moe_w_up, moe_w_down, moe_ws_gate, moe_ws_up, moe_ws_down):
    bsz, n_lat, d = x.shape
    n_ctx = ctx.shape[1]
    L = n_ctx + n_lat
    geom = (bsz, L, n_ctx)
    depth = mod_w.shape[0]
    assert depth == 2 and n_ctx == ROW_TILE and n_lat % ROW_TILE == 0 and bsz < 16

    h = jnp.concatenate([ctx, x], axis=1).reshape(bsz * L, d)
    c_all = jnp.zeros((16, d), f32).at[:bsz].set(c).at[bsz].set(c_ctx)

    mod3 = modulation(c_all, mod_w, mod_b, 0).reshape(16, 6, d)
    h = even_layer_mix(h, mod3, norm1_g[0], ev_w_in[0], ev_gm_ws[0], ev_gm_bs[0], ev_gm_ln_g[0],
                       ev_gm_ln_b[0], ev_conv_w[0], ev_conv_b[0], ev_a_log[0], ev_dt_bias[0],
                       ev_d_skip[0], ev_ssd_norm_g[0], ev_w_out[0], geom)
    h = moe_layer(h, norm2_g[0], mod3, 0, moe_router_w[0], moe_router_bias[0], moe_w_gate,
                  moe_w_up, moe_w_down, moe_ws_gate[0], moe_ws_up[0], moe_ws_down[0],
                  geom, lat_only=False)

    mod3 = modulation(c_all, mod_w, mod_b, 1).reshape(16, 6, d)
    h_lat = odd_layer_mix_lat(h, mod3, norm1_g[1], od_w_q[0], od_w_kv[0], od_q_norm_g[0],
                              od_k_norm_g[0], od_sink[0], od_w_o[0], geom)
    h_lat = moe_layer(h_lat, norm2_g[1], mod3, 1, moe_router_w[1], moe_router_bias[1],
                      moe_w_gate, moe_w_up, moe_w_down, moe_ws_gate[1], moe_ws_up[1],
                      moe_ws_down[1], geom, lat_only=True)
    return h_lat.reshape(bsz, n_lat, d)
```

```python
import functools

import jax
import jax.numpy as jnp
from jax import lax
from jax.experimental import pallas as pl
from jax.experimental.pallas import tpu as pltpu

f32 = jnp.float32
bf16 = jnp.bfloat16
i32 = jnp.int32
HIGHEST = lax.Precision.HIGHEST

D_MODEL = 2048
EPS = 1e-6
GRID_W = 64

A_CHUNK = 128
A_GROUPS = 16

SSD_HEADS = 32
SSD_HEADDIM = 64
SSD_GROUPS = 8
SSD_STATE = 128
SSD_CHUNK = 128
SSD_HPG = SSD_HEADS // SSD_GROUPS
SSD_GW = SSD_HPG * SSD_HEADDIM

ATT_HEADDIM = 64
ATT_HEADS = 32
ATT_KV_HEADS = 4
ATT_Q_PER_KV = 8
ATT_WINDOW = 128
ATT_BLOCK = 128
ATT_SCALE = ATT_HEADDIM ** -0.5
ROPE_BASE = 10000.0

N_EXPERTS = 64
N_EXPERT_GROUPS = 8
GROUP_SIZE = N_EXPERTS // N_EXPERT_GROUPS
TOP_GROUPS = 4
TOP_K = 8
EXPERT_HIDDEN = 512
ROUTED_SCALE = 2.5

ROW_TILE = 256
BIG_TILE = 768
MOE_BLK = 512
MOE_BLK_LOG2 = 9
COMB_TILE = 64
VMEM_LIMIT = 56 * 1024 * 1024

_NT = (((1,), (1,)), ((), ()))
_TN = (((0,), (0,)), ((), ()))


def _cparams(*sem):
    return pltpu.CompilerParams(dimension_semantics=sem, vmem_limit_bytes=VMEM_LIMIT)


def _pack_bf16_pair(x):
    c = x.shape[1] // 2
    lo = lax.bitcast_convert_type(x[:, :c].astype(bf16).astype(f32), jnp.uint32)
    hi = lax.bitcast_convert_type(x[:, c:].astype(bf16).astype(f32), jnp.uint32)
    return (lo >> 16) | hi


def _unpack_bf16_pair(p):
    lo = lax.bitcast_convert_type(p << 16, f32)
    hi = lax.bitcast_convert_type(p & jnp.uint32(0xFFFF0000), f32)
    return lo, hi


def _pick_tile(m, options):
    for t in options:
        if m % t == 0:
            return t
    raise ValueError(f"no tile for {m}")


def _mod_kernel(c_ref, w_ref, b_ref, o_ref):
    sc = jax.nn.silu(c_ref[...])
    o_ref[...] = jnp.dot(sc, w_ref[...], precision=HIGHEST,
                         preferred_element_type=f32) + b_ref[...]


def modulation(c_all, mod_w, mod_b, layer):
    rows, d = c_all.shape
    n = mod_w.shape[2]
    tn = 1024
    return pl.pallas_call(
        _mod_kernel,
        grid=(n // tn,),
        in_specs=[pl.BlockSpec((rows, d), lambda j: (0, 0)),
                  pl.BlockSpec((None, d, tn), lambda j: (layer, 0, j)),
                  pl.BlockSpec((None, 1, tn), lambda j: (layer, 0, j))],
        out_specs=pl.BlockSpec((rows, tn), lambda j: (0, j)),
        out_shape=jax.ShapeDtypeStruct((rows, n), f32),
        compiler_params=_cparams("arbitrary"),
        name="modulation",
    )(c_all, mod_w, mod_b.reshape(mod_b.shape[0], 1, n))


def _mod_row_map(geom, tile):
    bsz, L, n_ctx = geom
    tpb = L // tile
    nct = n_ctx // tile

    def mod_row(i):
        return jnp.where((i % tpb) < nct, bsz, i // tpb)
    return mod_row


def _lat_tile_map(geom, tile):
    bsz, L, n_ctx = geom
    lpb = (L - n_ctx) // tile
    tpb = L // tile
    nct = n_ctx // tile

    def joint(t):
        return (t // lpb) * tpb + nct + (t % lpb)
    return joint


def _norm_mod_kernel(x_ref, g_ref, mod_ref, o_ref, *, shift_idx, scale_idx):
    x = x_ref[...]
    ms = jnp.mean(x * x, axis=-1, keepdims=True)
    y = x * lax.rsqrt(ms + EPS) * g_ref[...]
    y = y * (1.0 + mod_ref[0, scale_idx:scale_idx + 1, :]) + mod_ref[0, shift_idx:shift_idx + 1, :]
    o_ref[...] = y.astype(o_ref.dtype)


def norm_modulate(h, g, mod3, geom, shift_idx, scale_idx):
    m, d = h.shape
    mod_row = _mod_row_map(geom, ROW_TILE)
    return pl.pallas_call(
        functools.partial(_norm_mod_kernel, shift_idx=shift_idx, scale_idx=scale_idx),
        grid=(m // ROW_TILE,),
        in_specs=[pl.BlockSpec((ROW_TILE, d), lambda i: (i, 0)),
                  pl.BlockSpec((1, d), lambda i: (0, 0)),
                  pl.BlockSpec((1, 6, d), lambda i: (mod_row(i), 0, 0))],
        out_specs=pl.BlockSpec((ROW_TILE, d), lambda i: (i, 0)),
        out_shape=jax.ShapeDtypeStruct((m, d), bf16),
        compiler_params=_cparams("parallel"),
        name="norm_modulate",
    )(h, g.reshape(1, d), mod3)


def _mm_kernel(a_ref, w_ref, o_ref):
    o_ref[...] = jnp.dot(a_ref[...], w_ref[...],
                         preferred_element_type=f32).astype(o_ref.dtype)


def matmul_nn(a, w, out_dtype, tn=1024):
    m, k = a.shape
    n = w.shape[1]
    tm = _pick_tile(m, (1024, 768, 512, 256))
    return pl.pallas_call(
        _mm_kernel,
        grid=(n // tn, m // tm),
        in_specs=[pl.BlockSpec((tm, k), lambda j, i: (i, 0)),
                  pl.BlockSpec((k, tn), lambda j, i: (0, j))],
        out_specs=pl.BlockSpec((tm, tn), lambda j, i: (i, j)),
        out_shape=jax.ShapeDtypeStruct((m, n), out_dtype),
        compiler_params=_cparams("parallel", "parallel"),
        name="matmul_nn",
    )(a, w)


def _dt_kernel(a_ref, w_ref, o_ref):
    o_ref[...] = lax.dot_general(w_ref[...], a_ref[...].astype(f32), _NT,
                                 precision=HIGHEST, preferred_element_type=f32)


def dt_project(a, w_dt_t):
    m, k = a.shape
    r = w_dt_t.shape[0]
    tm = _pick_tile(m, (1024, 768, 512, 256))
    return pl.pallas_call(
        _dt_kernel,
        grid=(m // tm,),
        in_specs=[pl.BlockSpec((tm, k), lambda i: (i, 0)),
                  pl.BlockSpec((r, k), lambda i: (0, 0))],
        out_specs=pl.BlockSpec((r, tm), lambda i: (0, i)),
        out_shape=jax.ShapeDtypeStruct((r, m), f32),
        compiler_params=_cparams("parallel"),
        name="dt_project",
    )(a, w_dt_t)


def _gelu_exact(x):
    return 0.5 * x * (1.0 + lax.erf(x * (2.0 ** -0.5)))


def _gmlp_kernel(u_ref, v_ref, ws_ref, bexp_ref, lng_ref, lnb_ref, o_ref):
    u = _gelu_exact(u_ref[...])
    v = _gelu_exact(v_ref[...])
    mu = jnp.mean(v, axis=-1, keepdims=True)
    vc = v - mu
    var = jnp.mean(vc * vc, axis=-1, keepdims=True)
    vn = (vc * lax.rsqrt(var + EPS) * lng_ref[...] + lnb_ref[...]).astype(bf16)
    for g in range(A_GROUPS):
        sl = slice(g * A_CHUNK, (g + 1) * A_CHUNK)
        mixed = jnp.dot(ws_ref[g], vn[:, sl], preferred_element_type=f32) + bexp_ref[:, sl]
        o_ref[:, sl] = (u[:, sl] * mixed).astype(o_ref.dtype)


def gmlp_mix(proj, ws_bf, bexp, ln_g, ln_b):
    m = proj.shape[0]
    w = A_GROUPS * A_CHUNK
    return pl.pallas_call(
        _gmlp_kernel,
        grid=(m // A_CHUNK,),
        in_specs=[pl.BlockSpec((A_CHUNK, w), lambda c: (c, 0)),
                  pl.BlockSpec((A_CHUNK, w), lambda c: (c, 1)),
                  pl.BlockSpec((A_GROUPS, A_CHUNK, A_CHUNK), lambda c: (0, 0, 0)),
                  pl.BlockSpec((A_CHUNK, w), lambda c: (0, 0)),
                  pl.BlockSpec((1, w), lambda c: (0, 0)),
                  pl.BlockSpec((1, w), lambda c: (0, 0))],
        out_specs=pl.BlockSpec((A_CHUNK, w), lambda c: (c, 0)),
        out_shape=jax.ShapeDtypeStruct((m, w), bf16),
        compiler_params=_cparams("parallel"),
        name="gmlp_mix",
    )(proj, proj, ws_bf, bexp, ln_g.reshape(1, w), ln_b.reshape(1, w))


def _head_expand(v, lane):
    out = v[:, 3:4]
    for r in (2, 1, 0):
        out = jnp.where(lane < SSD_HEADDIM * (r + 1), v[:, r:r + 1], out)
    return out


def _row_expand(v, row):
    out = v[:, 3:4]
    for r in (2, 1, 0):
        out = jnp.where(row < SSD_HEADDIM * (r + 1), v[:, r:r + 1], out)
    return out


def _ssd_kernel(x_ref, b_ref, c_ref, z_ref, dtr_ref, dtc_ref,
                cwx_ref, cwb_ref, cwc_ref, cbx_ref, cbb_ref, cbc_ref,
                alr_ref, alc_ref, dbr_ref, dbc_ref, dsk_ref, ng_ref,
                o_ref,
                xs_s, bs_s, cs_s, dtr_s, dtar_s, dtc_s, dtac_s, suf_s, yacc_s, sf_s, sb_s,
                *, n_ctx, L):
    Q = SSD_CHUNK
    nc = L // Q
    ncc = n_ctx // Q
    H = SSD_HPG

    bias_r = dbr_ref[...]
    a_r = -jnp.exp(alr_ref[...])
    for c in range(nc):
        dt = jax.nn.softplus(dtr_ref[:, c * Q:(c + 1) * Q] + bias_r)
        dtr_s[c] = dt
        dtar_s[c] = dt * a_r
    dtc = jax.nn.softplus(dtc_ref[0] + dbc_ref[0])
    dtc_s[...] = dtc
    dtac_s[...] = dtc * (-jnp.exp(alc_ref[0]))

    ii = lax.broadcasted_iota(i32, (Q, Q), 0)
    jj = lax.broadcasted_iota(i32, (Q, Q), 1)
    lower = ii >= jj
    upper = ii <= jj
    tril = lower.astype(f32)
    triu = upper.astype(f32)
    lane = lax.broadcasted_iota(i32, (Q, SSD_GW), 1)
    row = lax.broadcasted_iota(i32, (SSD_GW, SSD_STATE), 0)

    def conv_act(ref, w_ref, bias_ref, s):
        xv = ref[pl.ds(s, Q), :]
        w = xv.shape[1]
        r0 = lax.broadcasted_iota(i32, (Q, w), 0)
        prev = ref[pl.ds(jnp.maximum(s - 1, 0), 1), :]
        nxt = ref[pl.ds(jnp.minimum(s + Q, L - 1), 1), :]
        has_prev = jnp.logical_and(s != 0, s != n_ctx)
        has_next = jnp.logical_and(s + Q != n_ctx, s + Q != L)
        prev = jnp.where(has_prev, prev, 0.0)
        nxt = jnp.where(has_next, nxt, 0.0)
        xm1 = jnp.where(r0 == 0, prev, pltpu.roll(xv, 1, 0))
        xp1 = jnp.where(r0 == Q - 1, nxt, pltpu.roll(xv, Q - 1, 0))
        y = w_ref[0:1, :] * xm1 + w_ref[1:2, :] * xv + w_ref[2:3, :] * xp1 + bias_ref[...]
        return jax.nn.silu(y)

    sf_s[...] = jnp.zeros_like(sf_s)
    sb_s[...] = jnp.zeros_like(sb_s)

    def fwd_body(c, carry):
        s = pl.multiple_of(c * Q, Q)
        X = conv_act(x_ref, cwx_ref, cbx_ref, s)
        Bc = conv_act(b_ref, cwb_ref, cbb_ref, s)
        Cc = conv_act(c_ref, cwc_ref, cbc_ref, s)
        xs_s[pl.ds(s, Q), :] = X
        bs_s[pl.ds(s, Q), :] = Bc
        cs_s[pl.ds(s, Q), :] = Cc
        Bb = Bc.astype(bf16)
        Cb = Cc.astype(bf16)
        cb = lax.dot_general(Cb, Bb, _NT, preferred_element_type=f32)
        dta_c = dtac_s[pl.ds(s, Q), :]
        dta_r = dtar_s[c]
        dt_r = dtr_s[c]
        p_col = jnp.dot(tril, dta_c, precision=HIGHEST, preferred_element_type=f32)
        s_col = jnp.dot(triu, dta_c, precision=HIGHEST, preferred_element_type=f32)
        p_row = jnp.dot(dta_r, triu, precision=HIGHEST, preferred_element_type=f32)
        s_row = jnp.dot(dta_r, tril, precision=HIGHEST, preferred_element_type=f32)
        suf_s[pl.ds(s, Q), :] = s_col
        y = jnp.zeros((Q, SSD_GW), f32)
        for r in range(H):
            lf = jnp.exp(jnp.where(lower, p_col[:, r:r + 1] - p_row[r:r + 1, :], -jnp.inf))
            lb = jnp.exp(jnp.where(upper, s_col[:, H + r:H + r + 1] - s_row[H + r:H + r + 1, :],
                                   -jnp.inf))
            mr = cb * (lf * dt_r[r:r + 1, :] + lb * dt_r[H + r:H + r + 1, :])
            head = jnp.logical_and(lane >= SSD_HEADDIM * r, lane < SSD_HEADDIM * (r + 1))
            xm = jnp.where(head, X, 0.0).astype(bf16)
            y = y + jnp.dot(mr.astype(bf16), xm, preferred_element_type=f32)
        sf = sf_s[...]
        yi = lax.dot_general(Cb, sf.astype(bf16), _NT, preferred_element_type=f32)
        pf = p_col[:, 0:H]
        y = y + yi * _head_expand(jnp.exp(pf), lane)
        yacc_s[pl.ds(s, Q), :] = y
        tot = p_col[Q - 1:Q, 0:H]
        wf = dtc_s[pl.ds(s, Q), 0:H] * jnp.exp(tot - pf)
        xw = (X * _head_expand(wf, lane)).astype(bf16)
        upd = lax.dot_general(xw, Bb, _TN, preferred_element_type=f32)
        sf_s[...] = sf * _row_expand(jnp.exp(tot), row) + upd
        return carry

    lax.fori_loop(0, nc, fwd_body, 0, unroll=3)

    def bwd_body(k, carry):
        c = jnp.where(k < ncc, ncc - 1 - k, nc - 1 - (k - ncc))
        s = pl.multiple_of(c * Q, Q)
        X = xs_s[pl.ds(s, Q), :]
        Bb = bs_s[pl.ds(s, Q), :].astype(bf16)
        Cb = cs_s[pl.ds(s, Q), :].astype(bf16)
        ab = suf_s[pl.ds(s, Q), H:2 * H]
        tot = ab[0:1, :]
        sb = sb_s[...]
        yi = lax.dot_general(Cb, sb.astype(bf16), _NT, preferred_element_type=f32)
        y = yacc_s[pl.ds(s, Q), :] + yi * _head_expand(jnp.exp(ab), lane) + dsk_ref[...] * X
        wb = dtc_s[pl.ds(s, Q), H:2 * H] * jnp.exp(tot - ab)
        xw = (X * _head_expand(wb, lane)).astype(bf16)
        upd = lax.dot_general(xw, Bb, _TN, preferred_element_type=f32)
        sb_s[...] = sb * _row_expand(jnp.exp(tot), row) + upd
        y = y * jax.nn.silu(z_ref[pl.ds(s, Q), :])
        ms = jnp.mean(y * y, axis=-1, keepdims=True)
        o_ref[pl.ds(s, Q), :] = (y * lax.rsqrt(ms + EPS) * ng_ref[...]).astype(o_ref.dtype)
        return carry

    lax.fori_loop(0, nc, bwd_body, 0, unroll=2)


def ssd_mix(proj, dt_t, conv_w, conv_b, a_log, dt_bias, d_skip, norm_g, geom):
    bsz, L, n_ctx = geom
    m = proj.shape[0]
    G, H, GW, N = SSD_GROUPS, SSD_HPG, SSD_GW, SSD_STATE
    nc = L // SSD_CHUNK
    col0 = proj.shape[1] - (SSD_HEADS * SSD_HEADDIM + 2 * G * N)
    zc0 = col0 - SSD_HEADS * SSD_HEADDIM
    xb, zb = col0 // GW, zc0 // GW
    bb = (col0 + SSD_HEADS * SSD_HEADDIM) // N
    cb = bb + G
    cxb, cbb, ccb = 0, (SSD_HEADS * SSD_HEADDIM) // N, (SSD_HEADS * SSD_HEADDIM) // N + G

    def to_gdr(p):
        return p.reshape(2, G, H).transpose(1, 0, 2).reshape(G, 2 * H)

    al, db = to_gdr(a_log), to_gdr(dt_bias)
    dt_col = dt_t.reshape(G, 2 * H, m).transpose(0, 2, 1)
    dsk = jnp.repeat(d_skip, SSD_HEADDIM).reshape(1, -1)
    conv_b2 = conv_b.reshape(1, -1)
    kern = functools.partial(_ssd_kernel, n_ctx=n_ctx, L=L)
    return pl.pallas_call(
        kern,
        grid=(bsz, G),
        in_specs=[
            pl.BlockSpec((L, GW), lambda b, g: (b, xb + g)),
            pl.BlockSpec((L, N), lambda b, g: (b, bb + g)),
            pl.BlockSpec((L, N), lambda b, g: (b, cb + g)),
            pl.BlockSpec((L, GW), lambda b, g: (b, zb + g)),
            pl.BlockSpec((2 * H, L), lambda b, g: (g, b)),
            pl.BlockSpec((1, L, 2 * H), lambda b, g: (g, b, 0)),
            pl.BlockSpec((3, GW), lambda b, g: (0, cxb + g)),
            pl.BlockSpec((3, N), lambda b, g: (0, cbb + g)),
            pl.BlockSpec((3, N), lambda b, g: (0, ccb + g)),
            pl.BlockSpec((1, GW), lambda b, g: (0, cxb + g)),
            pl.BlockSpec((1, N), lambda b, g: (0, cbb + g)),
            pl.BlockSpec((1, N), lambda b, g: (0, ccb + g)),
            pl.BlockSpec((2 * H, 1), lambda b, g: (g, 0)),
            pl.BlockSpec((1, 1, 2 * H), lambda b, g: (g, 0, 0)),
            pl.BlockSpec((2 * H, 1), lambda b, g: (g, 0)),
            pl.BlockSpec((1, 1, 2 * H), lambda b, g: (g, 0, 0)),
            pl.BlockSpec((1, GW), lambda b, g: (0, g)),
            pl.BlockSpec((1, GW), lambda b, g: (0, g)),
        ],
        out_specs=pl.BlockSpec((L, GW), lambda b, g: (b, g)),
        out_shape=jax.ShapeDtypeStruct((m, G * GW), bf16),
        scratch_shapes=[
            pltpu.VMEM((L, GW), f32), pltpu.VMEM((L, N), f32), pltpu.VMEM((L, N), f32),
            pltpu.VMEM((nc, 2 * H, SSD_CHUNK), f32), pltpu.VMEM((nc, 2 * H, SSD_CHUNK), f32),
            pltpu.VMEM((L, 2 * H), f32), pltpu.VMEM((L, 2 * H), f32), pltpu.VMEM((L, 2 * H), f32),
            pltpu.VMEM((L, GW), f32),
            pltpu.VMEM((GW, N), f32), pltpu.VMEM((GW, N), f32),
        ],
        compiler_params=_cparams("parallel", "parallel"),
        name="ssd_mix",
    )(proj, proj, proj, proj, dt_t, dt_col,
      conv_w, conv_w, conv_w, conv_b2, conv_b2, conv_b2,
      al.reshape(G * 2 * H, 1), al.reshape(G, 1, 2 * H),
      db.reshape(G * 2 * H, 1), db.reshape(G, 1, 2 * H),
      dsk, norm_g.reshape(1, -1))


def _outproj_kernel(*refs, n_in, n_ctx, tpb, tm):
    a_refs = refs[:n_in]
    w_refs = refs[n_in:2 * n_in]
    h_ref, ml_ref, mc_ref, o_ref = refs[2 * n_in:]
    i = pl.program_id(1)
    acc = jnp.dot(a_refs[0][...], w_refs[0][...], preferred_element_type=f32)
    for k in range(1, n_in):
        acc = acc + jnp.dot(a_refs[k][...], w_refs[k][...], preferred_element_type=f32)
    n_ctx_rows = jnp.clip(n_ctx - (i % tpb) * tm, 0, tm)
    row = lax.broadcasted_iota(i32, acc.shape, 0)
    gate = jnp.where(row < n_ctx_rows, mc_ref[0, 2:3, :], ml_ref[0, 2:3, :])
    o_ref[...] = h_ref[...] + gate * acc


def out_project_joint(a_list, w_list, h, mod3, geom, tn=1024):
    bsz, L, n_ctx = geom
    m, d = h.shape
    tm = _pick_tile(L, (BIG_TILE, 512, 256))
    tpb = L // tm
    n_in = len(a_list)
    k = a_list[0].shape[1]
    kern = functools.partial(_outproj_kernel, n_in=n_in, n_ctx=n_ctx, tpb=tpb, tm=tm)
    in_specs = ([pl.BlockSpec((tm, k), lambda j, i: (i, 0)) for _ in range(n_in)]
                + [pl.BlockSpec((k, tn), lambda j, i: (0, j)) for _ in range(n_in)]
                + [pl.BlockSpec((tm, tn), lambda j, i: (i, j)),
                   pl.BlockSpec((1, 6, tn), lambda j, i: (i // tpb, 0, j)),
                   pl.BlockSpec((1, 6, tn), lambda j, i: (bsz, 0, j))])
    return pl.pallas_call(
        kern,
        grid=(d // tn, m // tm),
        in_specs=in_specs,
        out_specs=pl.BlockSpec((tm, tn), lambda j, i: (i, j)),
        out_shape=jax.ShapeDtypeStruct((m, d), f32),
        compiler_params=_cparams("parallel", "parallel"),
        name="out_project_joint",
    )(*a_list, *w_list, h, mod3, mod3)


def _outproj_lat_kernel(a_ref, w_ref, h_ref, ml_ref, o_ref):
    acc = jnp.dot(a_ref[...], w_ref[...], preferred_element_type=f32)
    o_ref[...] = h_ref[...] + ml_ref[0, 2:3, :] * acc


def out_project_lat(a, w, h_joint, mod3, geom, tn=1024):
    bsz, L, n_ctx = geom
    m, k = a.shape
    d = h_joint.shape[1]
    tm = ROW_TILE
    joint = _lat_tile_map(geom, tm)
    lpb = (L - n_ctx) // tm
    return pl.pallas_call(
        _outproj_lat_kernel,
        grid=(d // tn, m // tm),
        in_specs=[pl.BlockSpec((tm, k), lambda j, i: (i, 0)),
                  pl.BlockSpec((k, tn), lambda j, i: (0, j)),
                  pl.BlockSpec((tm, tn), lambda j, i: (joint(i), j)),
                  pl.BlockSpec((1, 6, tn), lambda j, i: (i // lpb, 0, j))],
        out_specs=pl.BlockSpec((tm, tn), lambda j, i: (i, j)),
        out_shape=jax.ShapeDtypeStruct((m, d), f32),
        compiler_params=_cparams("parallel", "parallel"),
        name="out_project_lat",
    )(a, w, h_joint, mod3)


QKV_ROW_BLOCK = 512


def _qkv_kernel(a_ref, w_ref, gq_ref, gk_ref, cos_ref, sin_ref, o_ref, *, n_qblk):
    j = pl.program_id(1)
    acc = jnp.dot(a_ref[...], w_ref[...], preferred_element_type=f32).T
    hd = ATT_HEADDIM
    q4 = hd // 4
    cs = cos_ref[...]
    sn = sin_ref[...]

    def norm_rope(hh, g):
        blk = acc[hh * hd:(hh + 1) * hd, :]
        ms = jnp.mean(blk * blk, axis=0, keepdims=True)
        xn = blk * lax.rsqrt(ms + EPS) * g
        sw = jnp.concatenate([xn[q4:2 * q4], xn[0:q4], xn[3 * q4:4 * q4], xn[2 * q4:3 * q4]],
                             axis=0)
        o_ref[hh * hd:(hh + 1) * hd, :] = xn * cs + sw * sn

    @pl.when(j < n_qblk)
    def _():
        g = gq_ref[...] * ATT_SCALE
        for hh in range(acc.shape[0] // hd):
            norm_rope(hh, g)

    @pl.when(j == n_qblk)
    def _():
        g = gk_ref[...]
        for hh in range(ATT_KV_HEADS):
            norm_rope(hh, g)
        o_ref[ATT_KV_HEADS * hd:, :] = acc[ATT_KV_HEADS * hd:, :]


def qkv_project_t(a, w, gq, gk, cos_t, sin_t, geom):
    bsz, L, n_ctx = geom
    m, k = a.shape
    r = w.shape[1]
    rb = QKV_ROW_BLOCK
    assert 2 * ATT_KV_HEADS * ATT_HEADDIM == rb and r % rb == 0
    tm = _pick_tile(L, (BIG_TILE, 512, 256))
    tpb = L // tm
    n_qblk = (ATT_HEADS * ATT_HEADDIM) // rb
    return pl.pallas_call(
        functools.partial(_qkv_kernel, n_qblk=n_qblk),
        grid=(m // tm, r // rb),
        in_specs=[pl.BlockSpec((tm, k), lambda i, j: (i, 0)),
                  pl.BlockSpec((k, rb), lambda i, j: (0, j)),
                  pl.BlockSpec((ATT_HEADDIM, 1), lambda i, j: (0, 0)),
                  pl.BlockSpec((ATT_HEADDIM, 1), lambda i, j: (0, 0)),
                  pl.BlockSpec((ATT_HEADDIM, tm), lambda i, j: (0, i % tpb)),
                  pl.BlockSpec((ATT_HEADDIM, tm), lambda i, j: (0, i % tpb))],
        out_specs=pl.BlockSpec((rb, tm), lambda i, j: (j, i)),
        out_shape=jax.ShapeDtypeStruct((r, m), f32),
        compiler_params=_cparams("parallel", "parallel"),
        name="qkv_project_t",
    )(a, w, gq.reshape(-1, 1), gk.reshape(-1, 1), cos_t, sin_t)


def _attn_kernel(sink_ref, q_ref, kc_ref, k0_ref, k1_ref, k2_ref,
                 vc_ref, v0_ref, v1_ref, v2_ref, o_ref, ot_s, *, n_lat, n_ctx):
    qi = pl.program_id(1)
    kvh = pl.program_id(2)
    blk = ATT_BLOCK
    hd = ATT_HEADDIM
    kt = jnp.concatenate([kc_ref[...], k0_ref[...], k1_ref[...], k2_ref[...]], axis=1)
    vt = jnp.concatenate([vc_ref[...], v0_ref[...], v1_ref[...], v2_ref[...]], axis=1)
    kt = kt.astype(bf16)
    vt = vt.astype(bf16)
    R = ATT_Q_PER_KV
    t = lax.broadcasted_iota(i32, (blk, blk), 0)
    c = lax.broadcasted_iota(i32, (blk, blk), 1)
    m_prev = jnp.logical_and(t >= c, qi >= 1)
    m_next = jnp.logical_and(t <= c, (qi + 1) * blk < n_lat)
    q_all = jnp.concatenate([q_ref[r * hd:(r + 1) * hd, :].astype(bf16) for r in range(R)],
                            axis=1)
    s_all = lax.dot_general(kt, q_all, _TN, preferred_element_type=f32)
    p_list, dens = [], []
    for r in range(R):
        s = s_all[:, r * blk:(r + 1) * blk]
        parts = [s[:n_ctx],
                 jnp.where(m_prev, s[n_ctx:n_ctx + blk], -jnp.inf),
                 s[n_ctx + blk:n_ctx + 2 * blk],
                 jnp.where(m_next, s[n_ctx + 2 * blk:], -jnp.inf)]
        sink = sink_ref[kvh * R + r]
        mx = sink
        for x in parts:
            mx = jnp.maximum(jnp.max(x, axis=0, keepdims=True), mx)
        ps = [jnp.exp(x - mx) for x in parts]
        den = jnp.exp(sink - mx)
        for x in ps:
            den = den + jnp.sum(x, axis=0, keepdims=True)
        p_list.append(jnp.concatenate(ps, axis=0).astype(bf16))
        dens.append(den)
    p_all = jnp.concatenate(p_list, axis=1)
    o_all = jnp.dot(vt, p_all, preferred_element_type=f32)
    for r in range(R):
        ot_s[r * hd:(r + 1) * hd, :] = o_all[:, r * blk:(r + 1) * blk] / dens[r]
    o_ref[...] = ot_s[...].T.astype(o_ref.dtype)


def window_attention(qkv_t, sink, geom):
    bsz, L, n_ctx = geom
    n_lat = L - n_ctx
    blk = ATT_BLOCK
    hd = ATT_HEADDIM
    nqb = n_lat // blk
    cpb = L // blk
    cb0 = n_ctx // blk
    q_rows = ATT_Q_PER_KV * hd
    k_rb = (ATT_HEADS * hd) // hd
    v_rb = k_rb + ATT_KV_HEADS

    def band(rb0, off):
        def imap(b, qi, kvh, sink_ref):
            kb = jnp.clip(qi + off, 0, nqb - 1)
            return (rb0 + kvh, b * cpb + cb0 + kb)
        return pl.BlockSpec((hd, blk), imap)

    def ctx(rb0):
        return pl.BlockSpec((hd, n_ctx), lambda b, qi, kvh, s: (rb0 + kvh, b * (L // n_ctx)))

    kern = functools.partial(_attn_kernel, n_lat=n_lat, n_ctx=n_ctx)
    grid_spec = pltpu.PrefetchScalarGridSpec(
        num_scalar_prefetch=1,
        grid=(bsz, nqb, ATT_KV_HEADS),
        in_specs=[pl.BlockSpec((q_rows, blk), lambda b, qi, kvh, s: (kvh, b * cpb + cb0 + qi)),
                  ctx(k_rb), band(k_rb, -1), band(k_rb, 0), band(k_rb, 1),
                  ctx(v_rb), band(v_rb, -1), band(v_rb, 0), band(v_rb, 1)],
        out_specs=pl.BlockSpec((blk, q_rows), lambda b, qi, kvh, s: (b * nqb + qi, kvh)),
        scratch_shapes=[pltpu.VMEM((q_rows, blk), f32)],
    )
    return pl.pallas_call(
        kern,
        grid_spec=grid_spec,
        out_shape=jax.ShapeDtypeStruct((bsz * n_lat, ATT_HEADS * hd), bf16),
        compiler_params=_cparams("parallel", "parallel", "parallel"),
        name="window_attention",
    )(sink, *([qkv_t] * 9))


def _first_max(v, idx, n):
    m = jnp.max(v, axis=0, keepdims=True)
    first = jnp.min(jnp.where(v == m, idx, n), axis=0, keepdims=True)
    return m, first


def _route_kernel(h_ref, g_ref, mod_ref, rw_ref, rb_ref, wsg_ref, wsu_ref, wsd_ref,
                  m_ref, sh_ref, te_ref, wt_ref, pos_ref, cnt_ref, carry_s):
    i = pl.program_id(0)
    T = h_ref.shape[0]
    E, NG, GS = N_EXPERTS, N_EXPERT_GROUPS, GROUP_SIZE

    @pl.when(i == 0)
    def _():
        carry_s[...] = jnp.zeros_like(carry_s)

    x = h_ref[...]
    ms = jnp.mean(x * x, axis=-1, keepdims=True)
    y = x * lax.rsqrt(ms + EPS) * g_ref[...]
    mt = y * (1.0 + mod_ref[0, 4:5, :]) + mod_ref[0, 3:4, :]
    m_ref[...] = _pack_bf16_pair(mt)
    mb = mt.astype(bf16)

    hg = jnp.dot(mb, wsg_ref[...], preferred_element_type=f32)
    hu = jnp.dot(mb, wsu_ref[...], preferred_element_type=f32)
    act = (jax.nn.silu(hg) * hu).astype(bf16)
    sh_ref[...] = jnp.dot(act, wsd_ref[...], preferred_element_type=f32)

    logits = lax.dot_general(rw_ref[...], mt, _NT, precision=HIGHEST,
                             preferred_element_type=f32)
    scores = jax.nn.sigmoid(logits)
    sel = scores + rb_ref[...]
    eidx = lax.broadcasted_iota(i32, (E, T), 0)
    midx = lax.broadcasted_iota(i32, (GS, T), 0)
    gidx = lax.broadcasted_iota(i32, (NG, T), 0)

    gs = jnp.zeros((NG, T), f32)
    for g in range(NG):
        blk = sel[g * GS:(g + 1) * GS, :]
        m1, f1 = _first_max(blk, midx, GS)
        m2 = jnp.max(jnp.where(midx == f1, -jnp.inf, blk), axis=0, keepdims=True)
        gs = jnp.where(gidx == g, m1 + m2, gs)
    gmask = jnp.zeros((NG, T), jnp.bool_)
    work = gs
    for _ in range(TOP_GROUPS):
        _, f = _first_max(work, gidx, NG)
        hit = gidx == f
        gmask = jnp.logical_or(gmask, hit)
        work = jnp.where(hit, -jnp.inf, work)
    cand = jnp.concatenate(
        [jnp.where(gmask[g:g + 1, :], sel[g * GS:(g + 1) * GS, :], -jnp.inf) for g in range(NG)],
        axis=0)
    hits = []
    chosen = jnp.zeros((E, T), jnp.bool_)
    work = cand
    for k in range(TOP_K):
        _, f = _first_max(work, eidx, E)
        hit = eidx == f
        hits.append(hit)
        chosen = jnp.logical_or(chosen, hit)
        work = jnp.where(hit, -jnp.inf, work)
        te_ref[k:k + 1, :] = f
    wts = [jnp.sum(jnp.where(h, scores, 0.0), axis=0, keepdims=True) for h in hits]
    tot = wts[0]
    for w in wts[1:]:
        tot = tot + w
    for k in range(TOP_K):
        wt_ref[k:k + 1, :] = wts[k] / (tot + 1e-20) * ROUTED_SCALE

    t0 = lax.broadcasted_iota(i32, (T, T), 0)
    t1 = lax.broadcasted_iota(i32, (T, T), 1)
    before = (t0 < t1).astype(bf16)
    chosen_f = chosen.astype(f32)
    cnt = jnp.dot(chosen_f.astype(bf16), before, preferred_element_type=f32) + carry_s[:, 0:1]
    for k in range(TOP_K):
        pos_ref[k:k + 1, :] = jnp.sum(jnp.where(hits[k], cnt, 0.0), axis=0,
                                      keepdims=True).astype(i32)
    new_carry = carry_s[...] + jnp.sum(chosen_f, axis=1, keepdims=True)
    carry_s[...] = new_carry
    cnt_ref[...] = new_carry.astype(i32)


def moe_route(h, g, mod3, rw_t, rbias, wsg, wsu, wsd, geom, lat_only):
    m, d = h.shape
    bsz, L, n_ctx = geom
    T = ROW_TILE
    if lat_only:
        lpb = (L - n_ctx) // T
        mod_row = lambda i: i // lpb
    else:
        mod_row = _mod_row_map(geom, T)
    E = N_EXPERTS
    hid = wsg.shape[1]
    const2 = lambda i: (0, 0)
    outs = pl.pallas_call(
        _route_kernel,
        grid=(m // T,),
        in_specs=[pl.BlockSpec((T, d), lambda i: (i, 0)),
                  pl.BlockSpec((1, d), const2),
                  pl.BlockSpec((1, 6, d), lambda i: (mod_row(i), 0, 0)),
                  pl.BlockSpec((E, d), const2),
                  pl.BlockSpec((E, 1), const2),
                  pl.BlockSpec((d, hid), const2),
                  pl.BlockSpec((d, hid), const2),
                  pl.BlockSpec((hid, d), const2)],
        out_specs=[pl.BlockSpec((T, d // 2), lambda i: (i, 0)),
                   pl.BlockSpec((T, d), lambda i: (i, 0)),
                   pl.BlockSpec((TOP_K, T), lambda i: (0, i)),
                   pl.BlockSpec((TOP_K, T), lambda i: (0, i)),
                   pl.BlockSpec((TOP_K, T), lambda i: (0, i)),
                   pl.BlockSpec((E, 128), const2)],
        out_shape=[jax.ShapeDtypeStruct((m, d // 2), jnp.uint32),
                   jax.ShapeDtypeStruct((m, d), f32),
                   jax.ShapeDtypeStruct((TOP_K, m), i32),
                   jax.ShapeDtypeStruct((TOP_K, m), f32),
                   jax.ShapeDtypeStruct((TOP_K, m), i32),
                   jax.ShapeDtypeStruct((E, 128), i32)],
        scratch_shapes=[pltpu.VMEM((E, 128), f32)],
        compiler_params=_cparams("arbitrary"),
        name="moe_route",
    )(h, g.reshape(1, d), mod3, rw_t, rbias.reshape(E, 1), wsg, wsu, wsd)
    return outs


GATHER_AHEAD = 2
GATHER_SLOTS = GATHER_AHEAD + 1


def _gmm_kernel(be_ref, nused_ref, tokc_ref, tokn_ref, tokf_ref, m_hbm, wg_ref, wu_ref,
                wd_ref, o_ref, xbuf, wg_s, wu_s, wd_s, sem):
    i = pl.program_id(0)
    nused = nused_ref[0]
    slot = i % GATHER_SLOTS
    far_slot = (i + GATHER_AHEAD) % GATHER_SLOTS

    def start_gather(tok_ref, dst_slot):
        for r in range(MOE_BLK):
            t = tok_ref[0, 0, r]
            pltpu.make_async_copy(m_hbm.at[pl.ds(t, 1), :],
                                  xbuf.at[dst_slot, pl.ds(r, 1), :],
                                  sem.at[dst_slot]).start()

    def wait_gather(dst_slot):
        pltpu.make_async_copy(m_hbm.at[pl.ds(0, MOE_BLK), :], xbuf.at[dst_slot],
                              sem.at[dst_slot]).wait()

    @pl.when(jnp.logical_and(i == 0, nused > 0))
    def _():
        start_gather(tokc_ref, 0)
        start_gather(tokn_ref, 1)

    prev = be_ref[jnp.maximum(i - 1, 0)]
    new_expert = jnp.logical_or(i == 0, be_ref[i] != prev)

    @pl.when(jnp.logical_and(i < nused, new_expert))
    def _():
        wg_s[...] = wg_ref[...].astype(bf16)
        wu_s[...] = wu_ref[...].astype(bf16)
        wd_s[...] = wd_ref[...].astype(bf16)

    @pl.when(i < nused)
    def _():
        wait_gather(slot)
        x_lo, x_hi = _unpack_bf16_pair(xbuf[slot])
        x_lo = x_lo.astype(bf16)
        x_hi = x_hi.astype(bf16)
        start_gather(tokf_ref, far_slot)
        kh = x_lo.shape[1]
        hg = (jnp.dot(x_lo, wg_s[:kh, :], preferred_element_type=f32)
              + jnp.dot(x_hi, wg_s[kh:, :], preferred_element_type=f32))
        hu = (jnp.dot(x_lo, wu_s[:kh, :], preferred_element_type=f32)
              + jnp.dot(x_hi, wu_s[kh:, :], preferred_element_type=f32))
        act = (jax.nn.silu(hg) * hu).astype(bf16)
        o_ref[...] = _pack_bf16_pair(jnp.dot(act, wd_s[...], preferred_element_type=f32))

    @pl.when(i >= nused)
    def _():
        o_ref[...] = jnp.zeros_like(o_ref)

    @pl.when(jnp.logical_and(jnp.logical_and(i >= nused, i < nused + GATHER_AHEAD), nused > 0))
    def _():
        wait_gather(slot)


def moe_experts(m_rows, slot_tok, block_expert, n_used, wg, wu, wd, layer):
    n_steps = block_expert.shape[0]
    d = wg.shape[2]
    dp = m_rows.shape[1]
    hid = wg.shape[3]
    tok3 = slot_tok.reshape(n_steps, 1, MOE_BLK)
    smem_blk = lambda imap: pl.BlockSpec((1, 1, MOE_BLK), imap, memory_space=pltpu.SMEM)
    grid_spec = pltpu.PrefetchScalarGridSpec(
        num_scalar_prefetch=2,
        grid=(n_steps,),
        in_specs=[smem_blk(lambda i, be, nu: (i, 0, 0)),
                  smem_blk(lambda i, be, nu: (jnp.minimum(i + 1, n_steps - 1), 0, 0)),
                  smem_blk(lambda i, be, nu: (jnp.minimum(i + GATHER_AHEAD, n_steps - 1), 0, 0)),
                  pl.BlockSpec(memory_space=pl.ANY),
                  pl.BlockSpec((None, None, d, hid), lambda i, be, nu: (layer, be[i], 0, 0)),
                  pl.BlockSpec((None, None, d, hid), lambda i, be, nu: (layer, be[i], 0, 0)),
                  pl.BlockSpec((None, None, hid, d), lambda i, be, nu: (layer, be[i], 0, 0))],
        out_specs=pl.BlockSpec((MOE_BLK, dp), lambda i, be, nu: (i, 0)),
        scratch_shapes=[pltpu.VMEM((GATHER_SLOTS, MOE_BLK, dp), jnp.uint32),
                        pltpu.VMEM((d, hid), bf16), pltpu.VMEM((d, hid), bf16),
                        pltpu.VMEM((hid, d), bf16),
                        pltpu.SemaphoreType.DMA((GATHER_SLOTS,))],
    )
    return pl.pallas_call(
        _gmm_kernel,
        grid_spec=grid_spec,
        out_shape=jax.ShapeDtypeStruct((n_steps * MOE_BLK, dp), jnp.uint32),
        compiler_params=_cparams("arbitrary"),
        name="moe_experts",
    )(block_expert, n_used, tok3, tok3, tok3, m_rows, wg, wu, wd)


def _comb_kernel(dc_ref, dn_ref, df_ref, ys_hbm, w_ref, sh_ref, h_ref, mod_ref, o_ref,
                 gbuf0, gbuf1, gbuf2, sem, *, n_tiles):
    i = pl.program_id(0)
    T = COMB_TILE
    gbufs = (gbuf0, gbuf1, gbuf2)
    S = GATHER_SLOTS

    def start_gather(d_ref, p):
        for r in range(T):
            for k in range(TOP_K):
                t = d_ref[0, k, r]
                pltpu.make_async_copy(ys_hbm.at[pl.ds(t, 1), :],
                                      gbufs[p].at[k, pl.ds(r, 1), :],
                                      sem.at[p]).start()

    @pl.when(i == 0)
    def _():
        start_gather(dc_ref, 0)
        start_gather(dn_ref, 1)

    def step(p, issue_far):
        for k in range(TOP_K):
            pltpu.make_async_copy(ys_hbm.at[pl.ds(0, T), :], gbufs[p].at[k],
                                  sem.at[p]).wait()
        if issue_far:
            start_gather(df_ref, (p + GATHER_AHEAD) % S)
        w = w_ref[...]
        half = o_ref.shape[1] // 2
        acc_lo = jnp.zeros((T, half), f32)
        acc_hi = jnp.zeros((T, half), f32)
        for k in range(TOP_K):
            lo, hi = _unpack_bf16_pair(gbufs[p][k])
            acc_lo = acc_lo + lo * w[:, k:k + 1]
            acc_hi = acc_hi + hi * w[:, k:k + 1]
        gate = mod_ref[0, 5:6, :]
        o_ref[:, :half] = h_ref[:, :half] + gate[:, :half] * (acc_lo + sh_ref[:, :half])
        o_ref[:, half:] = h_ref[:, half:] + gate[:, half:] * (acc_hi + sh_ref[:, half:])

    first_tail = n_tiles - GATHER_AHEAD
    for p in range(S):
        @pl.when(jnp.logical_and(i % S == p, i < first_tail))
        def _(p=p):
            step(p, True)

    for tail in range(first_tail, n_tiles):
        @pl.when(i == tail)
        def _(tail=tail):
            step(tail % S, False)


def moe_combine(y_sorted, dest_t, wts_t, shared, h, mod3, geom, lat_only):
    m, d = h.shape
    bsz, L, n_ctx = geom
    T = COMB_TILE
    n_tiles = m // T
    if lat_only:
        lpb = (L - n_ctx) // T
        mod_row = lambda i: i // lpb
    else:
        mod_row = _mod_row_map(geom, T)
    smem_blk = lambda imap: pl.BlockSpec((1, TOP_K, T), imap, memory_space=pltpu.SMEM)
    return pl.pallas_call(
        functools.partial(_comb_kernel, n_tiles=n_tiles),
        grid=(n_tiles,),
        in_specs=[smem_blk(lambda i: (i, 0, 0)),
                  smem_blk(lambda i: (jnp.minimum(i + 1, n_tiles - 1), 0, 0)),
                  smem_blk(lambda i: (jnp.minimum(i + GATHER_AHEAD, n_tiles - 1), 0, 0)),
                  pl.BlockSpec(memory_space=pl.ANY),
                  pl.BlockSpec((T, TOP_K), lambda i: (i, 0)),
                  pl.BlockSpec((T, d), lambda i: (i, 0)),
                  pl.BlockSpec((T, d), lambda i: (i, 0)),
                  pl.BlockSpec((1, 6, d), lambda i: (mod_row(i), 0, 0))],
        out_specs=pl.BlockSpec((T, d), lambda i: (i, 0)),
        out_shape=jax.ShapeDtypeStruct((m, d), f32),
        scratch_shapes=[pltpu.VMEM((TOP_K, T, d // 2), jnp.uint32)] * GATHER_SLOTS
        + [pltpu.SemaphoreType.DMA((GATHER_SLOTS,))],
        compiler_params=_cparams("arbitrary"),
        name="moe_combine",
    )(dest_t, dest_t, dest_t, y_sorted, wts_t, shared, h, mod3)


TOK_SPLIT_LOG2 = 7


def _slot_kernel(dest_ref, hi_ref, lo_ref):
    i = pl.program_id(0)
    T = dest_ref.shape[1]
    nb = hi_ref.shape[0]

    @pl.when(i == 0)
    def _():
        hi_ref[...] = jnp.zeros_like(hi_ref)
        lo_ref[...] = jnp.zeros_like(lo_ref)

    d = dest_ref[...]
    tok = i * T + lax.broadcasted_iota(i32, (1, T), 1)
    tok_hi = (tok >> TOK_SPLIT_LOG2).astype(f32)
    tok_lo = (tok & ((1 << TOK_SPLIT_LOG2) - 1)).astype(f32)
    brow = lax.broadcasted_iota(i32, (nb, T), 0)
    orow = lax.broadcasted_iota(i32, (MOE_BLK, T), 0)
    a_hi, a_lo, b = [], [], []
    for k in range(TOP_K):
        dk = d[k:k + 1, :]
        hit = brow == (dk >> MOE_BLK_LOG2)
        a_hi.append(jnp.where(hit, tok_hi, 0.0).astype(bf16))
        a_lo.append(jnp.where(hit, tok_lo, 0.0).astype(bf16))
        b.append((orow == (dk & (MOE_BLK - 1))).astype(f32).astype(bf16))
    bm = jnp.concatenate(b, axis=1)
    hi_ref[...] += lax.dot_general(jnp.concatenate(a_hi, axis=1), bm, _NT,
                                   preferred_element_type=f32)
    lo_ref[...] += lax.dot_general(jnp.concatenate(a_lo, axis=1), bm, _NT,
                                   preferred_element_type=f32)


def slot_table(dest, n_steps):
    m = dest.shape[1]
    T = ROW_TILE
    nb = -(-n_steps // 8) * 8
    hi, lo = pl.pallas_call(
        _slot_kernel,
        grid=(m // T,),
        in_specs=[pl.BlockSpec((TOP_K, T), lambda i: (0, i))],
        out_specs=[pl.BlockSpec((nb, MOE_BLK), lambda i: (0, 0)),
                   pl.BlockSpec((nb, MOE_BLK), lambda i: (0, 0))],
        out_shape=[jax.ShapeDtypeStruct((nb, MOE_BLK), f32),
                   jax.ShapeDtypeStruct((nb, MOE_BLK), f32)],
        compiler_params=_cparams("arbitrary"),
        name="slot_table",
    )(dest)
    tab = hi.astype(i32) * (1 << TOK_SPLIT_LOG2) + lo.astype(i32)
    return tab[:n_steps]


def moe_layer(h, g2, mod3, layer, rw, rbias, wg, wu, wd, wsg, wsu, wsd, geom, lat_only):
    m, d = h.shape
    E = N_EXPERTS
    mt, shared, top_e, wts, pos, cnt = moe_route(
        h, g2, mod3, rw.T, rbias, wsg.astype(bf16), wsu.astype(bf16), wsd.astype(bf16),
        geom, lat_only)
    counts = cnt[:, 0]
    padded = ((counts + MOE_BLK - 1) // MOE_BLK) * MOE_BLK
    pad_end = jnp.cumsum(padded)
    pad_start = pad_end - padded
    eids = jnp.arange(E, dtype=i32)[:, None, None]
    dest = pos + jnp.sum(jnp.where(top_e[None] == eids, pad_start[:, None, None], 0), axis=0)
    n_steps = (m * TOP_K) // MOE_BLK + E + GATHER_AHEAD
    slot_tok = slot_table(dest, n_steps)
    block_start = jnp.arange(n_steps, dtype=i32) * MOE_BLK
    block_expert = jnp.minimum(
        jnp.sum((pad_end[None, :] <= block_start[:, None]).astype(i32), axis=1), E - 1)
    n_used = (pad_end[-1:] // MOE_BLK).astype(i32)
    y_sorted = moe_experts(mt, slot_tok, block_expert, n_used, wg, wu, wd, layer)
    T = COMB_TILE
    dest_t = dest.reshape(TOP_K, m // T, T).transpose(1, 0, 2)
    return moe_combine(y_sorted, dest_t, wts.T, shared, h, mod3, geom, lat_only)


def _rope_tables_t(n_ctx, n_lat):
    n_rows = n_lat // GRID_W
    rows = jnp.repeat(jnp.arange(n_rows, dtype=f32), GRID_W)
    cols = jnp.tile(jnp.arange(GRID_W, dtype=f32), n_rows)
    n_freq = ATT_HEADDIM // 4
    inv_freq = ROPE_BASE ** (-jnp.arange(n_freq, dtype=f32) / n_freq)
    ar = rows[None, :] * inv_freq[:, None]
    ac = cols[None, :] * inv_freq[:, None]
    cos_t = jnp.concatenate([jnp.cos(ar), jnp.cos(ar), jnp.cos(ac), jnp.cos(ac)], axis=0)
    sin_t = jnp.concatenate([-jnp.sin(ar), jnp.sin(ar), -jnp.sin(ac), jnp.sin(ac)], axis=0)
    cos_t = jnp.concatenate([jnp.ones((ATT_HEADDIM, n_ctx), f32), cos_t], axis=1)
    sin_t = jnp.concatenate([jnp.zeros((ATT_HEADDIM, n_ctx), f32), sin_t], axis=1)
    return cos_t, sin_t


def even_layer_mix(h, mod3, g1, w_in, gm_ws, gm_bs, gm_ln_g, gm_ln_b, conv_w, conv_b,
                   a_log, dt_bias, d_skip, ssd_norm_g, w_out, geom):
    a = norm_modulate(h, g1, mod3, geom, 0, 1)
    n_main = w_in.shape[1] - 2 * SSD_HEADS
    proj = matmul_nn(a, w_in[:, :n_main].astype(bf16), f32)
    w_dt = w_in[:, n_main:].reshape(-1, 2, SSD_GROUPS, SSD_HPG).transpose(2, 1, 3, 0)
    dt_t = dt_project(a, w_dt.reshape(2 * SSD_HEADS, -1))
    bexp = jnp.repeat(gm_bs.T, A_CHUNK, axis=1)
    g_out = gmlp_mix(proj, gm_ws.astype(bf16), bexp, gm_ln_g, gm_ln_b)
    y_out = ssd_mix(proj, dt_t, conv_w, conv_b, a_log, dt_bias, d_skip, ssd_norm_g, geom)
    aw = gm_ws.shape[0] * gm_ws.shape[1]
    w_o = w_out.astype(bf16)
    return out_project_joint([g_out, y_out], [w_o[:aw], w_o[aw:]], h, mod3, geom)


def odd_layer_mix_lat(h, mod3, g1, w_q, w_kv, q_norm_g, k_norm_g, sink, w_o, geom):
    bsz, L, n_ctx = geom
    a = norm_modulate(h, g1, mod3, geom, 0, 1)
    w_t = jnp.concatenate([w_q, w_kv], axis=1).astype(bf16)
    cos_t, sin_t = _rope_tables_t(n_ctx, L - n_ctx)
    qkv_t = qkv_project_t(a, w_t, q_norm_g, k_norm_g, cos_t, sin_t, geom)
    att = window_attention(qkv_t, sink, geom)
    return out_project_lat(att, w_o.astype(bf16), h, mod3, geom)


def kernel(x, c, ctx, c_ctx, mod_w, mod_b, norm1_g, norm2_g, ev_w_in, ev_gm_ws, ev_gm_bs, ev_gm_ln_g, ev_gm_ln_b, ev_conv_w, ev_conv_b, ev_a_log, ev_dt_bias, ev_d_skip, ev_ssd_norm_g, ev_w_out, od_w_q, od_w_kv, od_q_norm_g, od_k_norm_g, od_sink, od_w_o, moe_router_w, moe_router_bias, moe_w_gate, moe_w_up, moe_w_down, moe_ws_gate, moe_ws_up, moe_ws_down):
    bsz, n_lat, d = x.shape
    n_ctx = ctx.shape[1]
    L = n_ctx + n_lat
    geom = (bsz, L, n_ctx)
    depth = mod_w.shape[0]
    assert depth == 2 and n_ctx == ROW_TILE and n_lat % ROW_TILE == 0 and bsz < 16

    h = jnp.concatenate([ctx, x], axis=1).reshape(bsz * L, d)
    c_all = jnp.zeros((16, d), f32).at[:bsz].set(c).at[bsz].set(c_ctx)

    mod3 = modulation(c_all, mod_w, mod_b, 0).reshape(16, 6, d)
    h = even_layer_mix(h, mod3, norm1_g[0], ev_w_in[0], ev_gm_ws[0], ev_gm_bs[0], ev_gm_ln_g[0],
                       ev_gm_ln_b[0], ev_conv_w[0], ev_conv_b[0], ev_a_log[0], ev_dt_bias[0],
                       ev_d_skip[0], ev_ssd_norm_g[0], ev_w_out[0], geom)
    h = moe_layer(h, norm2_g[0], mod3, 0, moe_router_w[0], moe_router_bias[0], moe_w_gate,
                  moe_w_up, moe_w_down, moe_ws_gate[0], moe_ws_up[0], moe_ws_down[0],
                  geom, lat_only=False)

    mod3 = modulation(c_all, mod_w, mod_b, 1).reshape(16, 6, d)
    h_lat = odd_layer_mix_lat(h, mod3, norm1_g[1], od_w_q[0], od_w_kv[0], od_q_norm_g[0],
                              od_k_norm_g[0], od_sink[0], od_w_o[0], geom)
    h_lat = moe_layer(h_lat, norm2_g[1], mod3, 1, moe_router_w[1], moe_router_bias[1],
                      moe_w_gate, moe_w_up, moe_w_down, moe_ws_gate[1], moe_ws_up[1],
                      moe_ws_down[1], geom, lat_only=True)
    return h_lat.reshape(bsz, n_lat, d)
```

```python
import functools

import jax
import jax.numpy as jnp
from jax import lax
from jax.experimental import pallas as pl
from jax.experimental.pallas import tpu as pltpu

f32 = jnp.float32
bf16 = jnp.bfloat16
i32 = jnp.int32
HIGHEST = lax.Precision.HIGHEST

D_MODEL = 2048
EPS = 1e-6
GRID_W = 64

A_CHUNK = 128
A_GROUPS = 16

SSD_HEADS = 32
SSD_HEADDIM = 64
SSD_GROUPS = 8
SSD_STATE = 128
SSD_CHUNK = 128
SSD_HPG = SSD_HEADS // SSD_GROUPS
SSD_GW = SSD_HPG * SSD_HEADDIM

ATT_HEADDIM = 64
ATT_HEADS = 32
ATT_KV_HEADS = 4
ATT_Q_PER_KV = 8
ATT_WINDOW = 128
ATT_BLOCK = 128
ATT_SCALE = ATT_HEADDIM ** -0.5
ROPE_BASE = 10000.0

N_EXPERTS = 64
N_EXPERT_GROUPS = 8
GROUP_SIZE = N_EXPERTS // N_EXPERT_GROUPS
TOP_GROUPS = 4
TOP_K = 8
EXPERT_HIDDEN = 512
ROUTED_SCALE = 2.5

ROW_TILE = 256
BIG_TILE = 768
ROUTE_TILE_JOINT = 384
ROUTE_TILE_LAT = 512
MOE_BLK = 512
MOE_BLK_LOG2 = 9
COMB_TILE = 64
VMEM_LIMIT = 56 * 1024 * 1024

_NT = (((1,), (1,)), ((), ()))
_TN = (((0,), (0,)), ((), ()))


def _cparams(*sem):
    return pltpu.CompilerParams(dimension_semantics=sem, vmem_limit_bytes=VMEM_LIMIT)


def _pack_bf16_pair(x):
    c = x.shape[1] // 2
    lo = lax.bitcast_convert_type(x[:, :c].astype(bf16).astype(f32), jnp.uint32)
    hi = lax.bitcast_convert_type(x[:, c:].astype(bf16).astype(f32), jnp.uint32)
    return (lo >> 16) | hi


def _unpack_bf16_pair(p):
    lo = lax.bitcast_convert_type(p << 16, f32)
    hi = lax.bitcast_convert_type(p & jnp.uint32(0xFFFF0000), f32)
    return lo, hi


def _pick_tile(m, options):
    for t in options:
        if m % t == 0:
            return t
    raise ValueError(f"no tile for {m}")


def _mod_kernel(c_ref, w_ref, b_ref, o_ref):
    sc = jax.nn.silu(c_ref[...])
    o_ref[...] = jnp.dot(sc, w_ref[...], precision=HIGHEST,
                         preferred_element_type=f32) + b_ref[...]


def modulation(c_all, mod_w, mod_b, layer):
    rows, d = c_all.shape
    n = mod_w.shape[2]
    tn = 1024
    return pl.pallas_call(
        _mod_kernel,
        grid=(n // tn,),
        in_specs=[pl.BlockSpec((rows, d), lambda j: (0, 0)),
                  pl.BlockSpec((None, d, tn), lambda j: (layer, 0, j)),
                  pl.BlockSpec((None, 1, tn), lambda j: (layer, 0, j))],
        out_specs=pl.BlockSpec((rows, tn), lambda j: (0, j)),
        out_shape=jax.ShapeDtypeStruct((rows, n), f32),
        compiler_params=_cparams("arbitrary"),
        name="modulation",
    )(c_all, mod_w, mod_b.reshape(mod_b.shape[0], 1, n))


def _mod_row_map(geom, tile):
    bsz, L, n_ctx = geom
    tpb = L // tile
    nct = n_ctx // tile

    def mod_row(i):
        return jnp.where((i % tpb) < nct, bsz, i // tpb)
    return mod_row


def _lat_tile_map(geom, tile):
    bsz, L, n_ctx = geom
    lpb = (L - n_ctx) // tile
    tpb = L // tile
    nct = n_ctx // tile

    def joint(t):
        return (t // lpb) * tpb + nct + (t % lpb)
    return joint


def _norm_mod(x, g, mod_ref, shift_idx, scale_idx):
    ms = jnp.mean(x * x, axis=-1, keepdims=True)
    y = x * lax.rsqrt(ms + EPS) * g
    return y * (1.0 + mod_ref[0, scale_idx:scale_idx + 1, :]) + mod_ref[0, shift_idx:shift_idx + 1, :]


def _join_norm_kernel(ctx_ref, x_ref, g_ref, mod_ref, h_ref, a_ref, *, tpb, nct):
    is_ctx = (pl.program_id(0) % tpb) < nct

    @pl.when(is_ctx)
    def _():
        h_ref[...] = ctx_ref[...]

    @pl.when(jnp.logical_not(is_ctx))
    def _():
        h_ref[...] = x_ref[...]

    a_ref[...] = _norm_mod(h_ref[...], g_ref[...], mod_ref, 0, 1).astype(a_ref.dtype)


def join_norm_modulate(ctx2, x2, g, mod3, geom):
    bsz, L, n_ctx = geom
    d = x2.shape[1]
    T = ROW_TILE
    tpb, nct = L // T, n_ctx // T
    lpb = tpb - nct
    mod_row = _mod_row_map(geom, T)
    return pl.pallas_call(
        functools.partial(_join_norm_kernel, tpb=tpb, nct=nct),
        grid=(bsz * tpb,),
        in_specs=[pl.BlockSpec((T, d), lambda i: ((i // tpb) * nct + jnp.minimum(i % tpb, nct - 1), 0)),
                  pl.BlockSpec((T, d), lambda i: ((i // tpb) * lpb + jnp.maximum(i % tpb - nct, 0), 0)),
                  pl.BlockSpec((1, d), lambda i: (0, 0)),
                  pl.BlockSpec((1, 6, d), lambda i: (mod_row(i), 0, 0))],
        out_specs=[pl.BlockSpec((T, d), lambda i: (i, 0)),
                   pl.BlockSpec((T, d), lambda i: (i, 0))],
        out_shape=[jax.ShapeDtypeStruct((bsz * L, d), f32),
                   jax.ShapeDtypeStruct((bsz * L, d), bf16)],
        compiler_params=_cparams("parallel"),
        name="join_norm_modulate",
    )(ctx2, x2, g.reshape(1, d), mod3)


def _mm_kernel(a_ref, w_ref, o_ref):
    o_ref[...] = jnp.dot(a_ref[...], w_ref[...],
                         preferred_element_type=f32).astype(o_ref.dtype)


def matmul_nn(a, w, out_dtype, tn=1024):
    m, k = a.shape
    n = w.shape[1]
    tm = _pick_tile(m, (1024, 768, 512, 256))
    return pl.pallas_call(
        _mm_kernel,
        grid=(n // tn, m // tm),
        in_specs=[pl.BlockSpec((tm, k), lambda j, i: (i, 0)),
                  pl.BlockSpec((k, tn), lambda j, i: (0, j))],
        out_specs=pl.BlockSpec((tm, tn), lambda j, i: (i, j)),
        out_shape=jax.ShapeDtypeStruct((m, n), out_dtype),
        compiler_params=_cparams("parallel", "parallel"),
        name="matmul_nn",
    )(a, w)


def _dt_kernel(a_ref, w_ref, o_ref):
    o_ref[...] = lax.dot_general(w_ref[...], a_ref[...].astype(f32), _NT,
                                 precision=HIGHEST, preferred_element_type=f32)


def dt_project(a, w_dt_t):
    m, k = a.shape
    r = w_dt_t.shape[0]
    tm = _pick_tile(m, (1024, 768, 512, 256))
    return pl.pallas_call(
        _dt_kernel,
        grid=(m // tm,),
        in_specs=[pl.BlockSpec((tm, k), lambda i: (i, 0)),
                  pl.BlockSpec((r, k), lambda i: (0, 0))],
        out_specs=pl.BlockSpec((r, tm), lambda i: (0, i)),
        out_shape=jax.ShapeDtypeStruct((r, m), f32),
        compiler_params=_cparams("parallel"),
        name="dt_project",
    )(a, w_dt_t)


def _gelu_exact(x):
    return 0.5 * x * (1.0 + lax.erf(x * (2.0 ** -0.5)))


def _gmlp_kernel(u_ref, v_ref, ws_ref, bexp_ref, lng_ref, lnb_ref, o_ref):
    u = _gelu_exact(u_ref[...])
    v = _gelu_exact(v_ref[...])
    mu = jnp.mean(v, axis=-1, keepdims=True)
    vc = v - mu
    var = jnp.mean(vc * vc, axis=-1, keepdims=True)
    vn = (vc * lax.rsqrt(var + EPS) * lng_ref[...] + lnb_ref[...]).astype(bf16)
    for g in range(A_GROUPS):
        sl = slice(g * A_CHUNK, (g + 1) * A_CHUNK)
        mixed = jnp.dot(ws_ref[g], vn[:, sl], preferred_element_type=f32) + bexp_ref[:, sl]
        o_ref[:, sl] = (u[:, sl] * mixed).astype(o_ref.dtype)


def gmlp_mix(proj, ws_bf, bexp, ln_g, ln_b):
    m = proj.shape[0]
    w = A_GROUPS * A_CHUNK
    return pl.pallas_call(
        _gmlp_kernel,
        grid=(m // A_CHUNK,),
        in_specs=[pl.BlockSpec((A_CHUNK, w), lambda c: (c, 0)),
                  pl.BlockSpec((A_CHUNK, w), lambda c: (c, 1)),
                  pl.BlockSpec((A_GROUPS, A_CHUNK, A_CHUNK), lambda c: (0, 0, 0)),
                  pl.BlockSpec((A_CHUNK, w), lambda c: (0, 0)),
                  pl.BlockSpec((1, w), lambda c: (0, 0)),
                  pl.BlockSpec((1, w), lambda c: (0, 0))],
        out_specs=pl.BlockSpec((A_CHUNK, w), lambda c: (c, 0)),
        out_shape=jax.ShapeDtypeStruct((m, w), bf16),
        compiler_params=_cparams("parallel"),
        name="gmlp_mix",
    )(proj, proj, ws_bf, bexp, ln_g.reshape(1, w), ln_b.reshape(1, w))


def _head_expand(v, lane):
    out = v[:, 3:4]
    for r in (2, 1, 0):
        out = jnp.where(lane < SSD_HEADDIM * (r + 1), v[:, r:r + 1], out)
    return out


def _row_expand(v, row):
    out = v[:, 3:4]
    for r in (2, 1, 0):
        out = jnp.where(row < SSD_HEADDIM * (r + 1), v[:, r:r + 1], out)
    return out


def _ssd_kernel(x_ref, b_ref, c_ref, z_ref, dtr_ref, dtc_ref,
                cwx_ref, cwb_ref, cwc_ref, cbx_ref, cbb_ref, cbc_ref,
                alr_ref, alc_ref, dbr_ref, dbc_ref, dsk_ref, ng_ref,
                o_ref,
                xs_s, bs_s, cs_s, dtr_s, dtar_s, dtc_s, dtac_s, suf_s, yacc_s, sf_s, sb_s,
                *, n_ctx, L):
    Q = SSD_CHUNK
    nc = L // Q
    ncc = n_ctx // Q
    H = SSD_HPG

    bias_r = dbr_ref[...]
    a_r = -jnp.exp(alr_ref[...])
    for c in range(nc):
        dt = jax.nn.softplus(dtr_ref[:, c * Q:(c + 1) * Q] + bias_r)
        dtr_s[c] = dt
        dtar_s[c] = dt * a_r
    dtc = jax.nn.softplus(dtc_ref[0] + dbc_ref[0])
    dtc_s[...] = dtc
    dtac_s[...] = dtc * (-jnp.exp(alc_ref[0]))

    ii = lax.broadcasted_iota(i32, (Q, Q), 0)
    jj = lax.broadcasted_iota(i32, (Q, Q), 1)
    lower = ii >= jj
    upper = ii <= jj
    tril = lower.astype(f32)
    triu = upper.astype(f32)
    lane = lax.broadcasted_iota(i32, (Q, SSD_GW), 1)
    row = lax.broadcasted_iota(i32, (SSD_GW, SSD_STATE), 0)

    def conv_act(ref, w_ref, bias_ref, s):
        xv = ref[pl.ds(s, Q), :]
        w = xv.shape[1]
        r0 = lax.broadcasted_iota(i32, (Q, w), 0)
        prev = ref[pl.ds(jnp.maximum(s - 1, 0), 1), :]
        nxt = ref[pl.ds(jnp.minimum(s + Q, L - 1), 1), :]
        has_prev = jnp.logical_and(s != 0, s != n_ctx)
        has_next = jnp.logical_and(s + Q != n_ctx, s + Q != L)
        prev = jnp.where(has_prev, prev, 0.0)
        nxt = jnp.where(has_next, nxt, 0.0)
        xm1 = jnp.where(r0 == 0, prev, pltpu.roll(xv, 1, 0))
        xp1 = jnp.where(r0 == Q - 1, nxt, pltpu.roll(xv, Q - 1, 0))
        y = w_ref[0:1, :] * xm1 + w_ref[1:2, :] * xv + w_ref[2:3, :] * xp1 + bias_ref[...]
        return jax.nn.silu(y)

    sf_s[...] = jnp.zeros_like(sf_s)
    sb_s[...] = jnp.zeros_like(sb_s)

    def fwd_body(c, carry):
        s = pl.multiple_of(c * Q, Q)
        X = conv_act(x_ref, cwx_ref, cbx_ref, s)
        Bc = conv_act(b_ref, cwb_ref, cbb_ref, s)
        Cc = conv_act(c_ref, cwc_ref, cbc_ref, s)
        xs_s[pl.ds(s, Q), :] = X
        bs_s[pl.ds(s, Q), :] = Bc
        cs_s[pl.ds(s, Q), :] = Cc
        Bb = Bc.astype(bf16)
        Cb = Cc.astype(bf16)
        cb = lax.dot_general(Cb, Bb, _NT, preferred_element_type=f32)
        dta_c = dtac_s[pl.ds(s, Q), :]
        dta_r = dtar_s[c]
        dt_r = dtr_s[c]
        p_col = jnp.dot(tril, dta_c, precision=HIGHEST, preferred_element_type=f32)
        s_col = jnp.dot(triu, dta_c, precision=HIGHEST, preferred_element_type=f32)
        p_row = jnp.dot(dta_r, triu, precision=HIGHEST, preferred_element_type=f32)
        s_row = jnp.dot(dta_r, tril, precision=HIGHEST, preferred_element_type=f32)
        suf_s[pl.ds(s, Q), :] = s_col
        y = jnp.zeros((Q, SSD_GW), f32)
        for r in range(H):
            lf = jnp.exp(jnp.where(lower, p_col[:, r:r + 1] - p_row[r:r + 1, :], -jnp.inf))
            lb = jnp.exp(jnp.where(upper, s_col[:, H + r:H + r + 1] - s_row[H + r:H + r + 1, :],
                                   -jnp.inf))
            mr = cb * (lf * dt_r[r:r + 1, :] + lb * dt_r[H + r:H + r + 1, :])
            head = jnp.logical_and(lane >= SSD_HEADDIM * r, lane < SSD_HEADDIM * (r + 1))
            xm = jnp.where(head, X, 0.0).astype(bf16)
            y = y + jnp.dot(mr.astype(bf16), xm, preferred_element_type=f32)
        sf = sf_s[...]
        yi = lax.dot_general(Cb, sf.astype(bf16), _NT, preferred_element_type=f32)
        pf = p_col[:, 0:H]
        y = y + yi * _head_expand(jnp.exp(pf), lane)
        yacc_s[pl.ds(s, Q), :] = y
        tot = p_col[Q - 1:Q, 0:H]
        wf = dtc_s[pl.ds(s, Q), 0:H] * jnp.exp(tot - pf)
        xw = (X * _head_expand(wf, lane)).astype(bf16)
        upd = lax.dot_general(xw, Bb, _TN, preferred_element_type=f32)
        sf_s[...] = sf * _row_expand(jnp.exp(tot), row) + upd
        return carry

    lax.fori_loop(0, nc, fwd_body, 0, unroll=3)

    def bwd_body(k, carry):
        c = jnp.where(k < ncc, ncc - 1 - k, nc - 1 - (k - ncc))
        s = pl.multiple_of(c * Q, Q)
        X = xs_s[pl.ds(s, Q), :]
        Bb = bs_s[pl.ds(s, Q), :].astype(bf16)
        Cb = cs_s[pl.ds(s, Q), :].astype(bf16)
        ab = suf_s[pl.ds(s, Q), H:2 * H]
        tot = ab[0:1, :]
        sb = sb_s[...]
        yi = lax.dot_general(Cb, sb.astype(bf16), _NT, preferred_element_type=f32)
        y = yacc_s[pl.ds(s, Q), :] + yi * _head_expand(jnp.exp(ab), lane) + dsk_ref[...] * X
        wb = dtc_s[pl.ds(s, Q), H:2 * H] * jnp.exp(tot - ab)
        xw = (X * _head_expand(wb, lane)).astype(bf16)
        upd = lax.dot_general(xw, Bb, _TN, preferred_element_type=f32)
        sb_s[...] = sb * _row_expand(jnp.exp(tot), row) + upd
        y = y * jax.nn.silu(z_ref[pl.ds(s, Q), :])
        ms = jnp.mean(y * y, axis=-1, keepdims=True)
        o_ref[pl.ds(s, Q), :] = (y * lax.rsqrt(ms + EPS) * ng_ref[...]).astype(o_ref.dtype)
        return carry

    lax.fori_loop(0, nc, bwd_body, 0, unroll=2)


def ssd_mix(proj, dt_t, conv_w, conv_b, a_log, dt_bias, d_skip, norm_g, geom):
    bsz, L, n_ctx = geom
    m = proj.shape[0]
    G, H, GW, N = SSD_GROUPS, SSD_HPG, SSD_GW, SSD_STATE
    nc = L // SSD_CHUNK
    col0 = proj.shape[1] - (SSD_HEADS * SSD_HEADDIM + 2 * G * N)
    zc0 = col0 - SSD_HEADS * SSD_HEADDIM
    xb, zb = col0 // GW, zc0 // GW
    bb = (col0 + SSD_HEADS * SSD_HEADDIM) // N
    cb = bb + G
    cxb, cbb, ccb = 0, (SSD_HEADS * SSD_HEADDIM) // N, (SSD_HEADS * SSD_HEADDIM) // N + G

    def to_gdr(p):
        return p.reshape(2, G, H).transpose(1, 0, 2).reshape(G, 2 * H)

    al, db = to_gdr(a_log), to_gdr(dt_bias)
    dt_col = dt_t.reshape(G, 2 * H, m).transpose(0, 2, 1)
    dsk = jnp.repeat(d_skip, SSD_HEADDIM).reshape(1, -1)
    conv_b2 = conv_b.reshape(1, -1)
    kern = functools.partial(_ssd_kernel, n_ctx=n_ctx, L=L)
    return pl.pallas_call(
        kern,
        grid=(bsz, G),
        in_specs=[
            pl.BlockSpec((L, GW), lambda b, g: (b, xb + g)),
            pl.BlockSpec((L, N), lambda b, g: (b, bb + g)),
            pl.BlockSpec((L, N), lambda b, g: (b, cb + g)),
            pl.BlockSpec((L, GW), lambda b, g: (b, zb + g)),
            pl.BlockSpec((2 * H, L), lambda b, g: (g, b)),
            pl.BlockSpec((1, L, 2 * H), lambda b, g: (g, b, 0)),
            pl.BlockSpec((3, GW), lambda b, g: (0, cxb + g)),
            pl.BlockSpec((3, N), lambda b, g: (0, cbb + g)),
            pl.BlockSpec((3, N), lambda b, g: (0, ccb + g)),
            pl.BlockSpec((1, GW), lambda b, g: (0, cxb + g)),
            pl.BlockSpec((1, N), lambda b, g: (0, cbb + g)),
            pl.BlockSpec((1, N), lambda b, g: (0, ccb + g)),
            pl.BlockSpec((2 * H, 1), lambda b, g: (g, 0)),
            pl.BlockSpec((1, 1, 2 * H), lambda b, g: (g, 0, 0)),
            pl.BlockSpec((2 * H, 1), lambda b, g: (g, 0)),
            pl.BlockSpec((1, 1, 2 * H), lambda b, g: (g, 0, 0)),
            pl.BlockSpec((1, GW), lambda b, g: (0, g)),
            pl.BlockSpec((1, GW), lambda b, g: (0, g)),
        ],
        out_specs=pl.BlockSpec((L, GW), lambda b, g: (b, g)),
        out_shape=jax.ShapeDtypeStruct((m, G * GW), bf16),
        scratch_shapes=[
            pltpu.VMEM((L, GW), f32), pltpu.VMEM((L, N), f32), pltpu.VMEM((L, N), f32),
            pltpu.VMEM((nc, 2 * H, SSD_CHUNK), f32), pltpu.VMEM((nc, 2 * H, SSD_CHUNK), f32),
            pltpu.VMEM((L, 2 * H), f32), pltpu.VMEM((L, 2 * H), f32), pltpu.VMEM((L, 2 * H), f32),
            pltpu.VMEM((L, GW), f32),
            pltpu.VMEM((GW, N), f32), pltpu.VMEM((GW, N), f32),
        ],
        compiler_params=_cparams("parallel", "parallel"),
        name="ssd_mix",
    )(proj, proj, proj, proj, dt_t, dt_col,
      conv_w, conv_w, conv_w, conv_b2, conv_b2, conv_b2,
      al.reshape(G * 2 * H, 1), al.reshape(G, 1, 2 * H),
      db.reshape(G * 2 * H, 1), db.reshape(G, 1, 2 * H),
      dsk, norm_g.reshape(1, -1))


def _outproj_kernel(*refs, n_in, n_ctx, tpb, tm):
    a_refs = refs[:n_in]
    w_refs = refs[n_in:2 * n_in]
    h_ref, ml_ref, mc_ref, o_ref = refs[2 * n_in:]
    i = pl.program_id(1)
    acc = jnp.dot(a_refs[0][...], w_refs[0][...], preferred_element_type=f32)
    for k in range(1, n_in):
        acc = acc + jnp.dot(a_refs[k][...], w_refs[k][...], preferred_element_type=f32)
    n_ctx_rows = jnp.clip(n_ctx - (i % tpb) * tm, 0, tm)
    row = lax.broadcasted_iota(i32, acc.shape, 0)
    gate = jnp.where(row < n_ctx_rows, mc_ref[0, 2:3, :], ml_ref[0, 2:3, :])
    o_ref[...] = h_ref[...] + gate * acc


def out_project_joint(a_list, w_list, h, mod3, geom, tn=1024):
    bsz, L, n_ctx = geom
    m, d = h.shape
    tm = _pick_tile(L, (BIG_TILE, 512, 256))
    tpb = L // tm
    n_in = len(a_list)
    k = a_list[0].shape[1]
    kern = functools.partial(_outproj_kernel, n_in=n_in, n_ctx=n_ctx, tpb=tpb, tm=tm)
    in_specs = ([pl.BlockSpec((tm, k), lambda j, i: (i, 0)) for _ in range(n_in)]
                + [pl.BlockSpec((k, tn), lambda j, i: (0, j)) for _ in range(n_in)]
                + [pl.BlockSpec((tm, tn), lambda j, i: (i, j)),
                   pl.BlockSpec((1, 6, tn), lambda j, i: (i // tpb, 0, j)),
                   pl.BlockSpec((1, 6, tn), lambda j, i: (bsz, 0, j))])
    return pl.pallas_call(
        kern,
        grid=(d // tn, m // tm),
        in_specs=in_specs,
        out_specs=pl.BlockSpec((tm, tn), lambda j, i: (i, j)),
        out_shape=jax.ShapeDtypeStruct((m, d), f32),
        compiler_params=_cparams("parallel", "parallel"),
        name="out_project_joint",
    )(*a_list, *w_list, h, mod3, mod3)


def _outproj_lat_kernel(a_ref, w_ref, h_ref, ml_ref, o_ref):
    acc = jnp.dot(a_ref[...], w_ref[...], preferred_element_type=f32)
    o_ref[...] = h_ref[...] + ml_ref[0, 2:3, :] * acc


def out_project_lat(a, w, h_joint, mod3, geom, tn=1024):
    bsz, L, n_ctx = geom
    m, k = a.shape
    d = h_joint.shape[1]
    tm = ROW_TILE
    joint = _lat_tile_map(geom, tm)
    lpb = (L - n_ctx) // tm
    return pl.pallas_call(
        _outproj_lat_kernel,
        grid=(d // tn, m // tm),
        in_specs=[pl.BlockSpec((tm, k), lambda j, i: (i, 0)),
                  pl.BlockSpec((k, tn), lambda j, i: (0, j)),
                  pl.BlockSpec((tm, tn), lambda j, i: (joint(i), j)),
                  pl.BlockSpec((1, 6, tn), lambda j, i: (i // lpb, 0, j))],
        out_specs=pl.BlockSpec((tm, tn), lambda j, i: (i, j)),
        out_shape=jax.ShapeDtypeStruct((m, d), f32),
        compiler_params=_cparams("parallel", "parallel"),
        name="out_project_lat",
    )(a, w, h_joint, mod3)


QKV_ROW_BLOCK = 512


def _qkv_kernel(a_ref, w_ref, gq_ref, gk_ref, cos_ref, sin_ref, o_ref, *, n_qblk):
    j = pl.program_id(1)
    acc = jnp.dot(a_ref[...], w_ref[...], preferred_element_type=f32).T
    hd = ATT_HEADDIM
    q4 = hd // 4
    cs = cos_ref[...]
    sn = sin_ref[...]

    def norm_rope(hh, g):
        blk = acc[hh * hd:(hh + 1) * hd, :]
        ms = jnp.mean(blk * blk, axis=0, keepdims=True)
        xn = blk * lax.rsqrt(ms + EPS) * g
        sw = jnp.concatenate([xn[q4:2 * q4], xn[0:q4], xn[3 * q4:4 * q4], xn[2 * q4:3 * q4]],
                             axis=0)
        o_ref[hh * hd:(hh + 1) * hd, :] = xn * cs + sw * sn

    @pl.when(j < n_qblk)
    def _():
        g = gq_ref[...] * ATT_SCALE
        for hh in range(acc.shape[0] // hd):
            norm_rope(hh, g)

    @pl.when(j == n_qblk)
    def _():
        g = gk_ref[...]
        for hh in range(ATT_KV_HEADS):
            norm_rope(hh, g)
        o_ref[ATT_KV_HEADS * hd:, :] = acc[ATT_KV_HEADS * hd:, :]


def qkv_project_t(a, w, gq, gk, cos_t, sin_t, geom):
    bsz, L, n_ctx = geom
    m, k = a.shape
    r = w.shape[1]
    rb = QKV_ROW_BLOCK
    assert 2 * ATT_KV_HEADS * ATT_HEADDIM == rb and r % rb == 0
    tm = _pick_tile(L, (BIG_TILE, 512, 256))
    tpb = L // tm
    n_qblk = (ATT_HEADS * ATT_HEADDIM) // rb
    return pl.pallas_call(
        functools.partial(_qkv_kernel, n_qblk=n_qblk),
        grid=(m // tm, r // rb),
        in_specs=[pl.BlockSpec((tm, k), lambda i, j: (i, 0)),
                  pl.BlockSpec((k, rb), lambda i, j: (0, j)),
                  pl.BlockSpec((ATT_HEADDIM, 1), lambda i, j: (0, 0)),
                  pl.BlockSpec((ATT_HEADDIM, 1), lambda i, j: (0, 0)),
                  pl.BlockSpec((ATT_HEADDIM, tm), lambda i, j: (0, i % tpb)),
                  pl.BlockSpec((ATT_HEADDIM, tm), lambda i, j: (0, i % tpb))],
        out_specs=pl.BlockSpec((rb, tm), lambda i, j: (j, i)),
        out_shape=jax.ShapeDtypeStruct((r, m), f32),
        compiler_params=_cparams("parallel", "parallel"),
        name="qkv_project_t",
    )(a, w, gq.reshape(-1, 1), gk.reshape(-1, 1), cos_t, sin_t)


def _attn_kernel(sink_ref, q_ref, kc_ref, k0_ref, k1_ref, k2_ref,
                 vc_ref, v0_ref, v1_ref, v2_ref, o_ref, ot_s, *, n_lat, n_ctx):
    qi = pl.program_id(1)
    kvh = pl.program_id(2)
    blk = ATT_BLOCK
    hd = ATT_HEADDIM
    kt = jnp.concatenate([kc_ref[...], k0_ref[...], k1_ref[...], k2_ref[...]], axis=1)
    vt = jnp.concatenate([vc_ref[...], v0_ref[...], v1_ref[...], v2_ref[...]], axis=1)
    kt = kt.astype(bf16)
    vt = vt.astype(bf16)
    R = ATT_Q_PER_KV
    t = lax.broadcasted_iota(i32, (blk, blk), 0)
    c = lax.broadcasted_iota(i32, (blk, blk), 1)
    m_prev = jnp.logical_and(t >= c, qi >= 1)
    m_next = jnp.logical_and(t <= c, (qi + 1) * blk < n_lat)
    q_all = jnp.concatenate([q_ref[r * hd:(r + 1) * hd, :].astype(bf16) for r in range(R)],
                            axis=1)
    s_all = lax.dot_general(kt, q_all, _TN, preferred_element_type=f32)
    p_list, dens = [], []
    for r in range(R):
        s = s_all[:, r * blk:(r + 1) * blk]
        parts = [s[:n_ctx],
                 jnp.where(m_prev, s[n_ctx:n_ctx + blk], -jnp.inf),
                 s[n_ctx + blk:n_ctx + 2 * blk],
                 jnp.where(m_next, s[n_ctx + 2 * blk:], -jnp.inf)]
        sink = sink_ref[kvh * R + r]
        mx = sink
        for x in parts:
            mx = jnp.maximum(jnp.max(x, axis=0, keepdims=True), mx)
        ps = [jnp.exp(x - mx) for x in parts]
        den = jnp.exp(sink - mx)
        for x in ps:
            den = den + jnp.sum(x, axis=0, keepdims=True)
        p_list.append(jnp.concatenate(ps, axis=0).astype(bf16))
        dens.append(den)
    p_all = jnp.concatenate(p_list, axis=1)
    o_all = jnp.dot(vt, p_all, preferred_element_type=f32)
    for r in range(R):
        ot_s[r * hd:(r + 1) * hd, :] = o_all[:, r * blk:(r + 1) * blk] / dens[r]
    o_ref[...] = ot_s[...].T.astype(o_ref.dtype)


def window_attention(qkv_t, sink, geom):
    bsz, L, n_ctx = geom
    n_lat = L - n_ctx
    blk = ATT_BLOCK
    hd = ATT_HEADDIM
    nqb = n_lat // blk
    cpb = L // blk
    cb0 = n_ctx // blk
    q_rows = ATT_Q_PER_KV * hd
    k_rb = (ATT_HEADS * hd) // hd
    v_rb = k_rb + ATT_KV_HEADS

    def band(rb0, off):
        def imap(b, qi, kvh, sink_ref):
            kb = jnp.clip(qi + off, 0, nqb - 1)
            return (rb0 + kvh, b * cpb + cb0 + kb)
        return pl.BlockSpec((hd, blk), imap)

    def ctx(rb0):
        return pl.BlockSpec((hd, n_ctx), lambda b, qi, kvh, s: (rb0 + kvh, b * (L // n_ctx)))

    kern = functools.partial(_attn_kernel, n_lat=n_lat, n_ctx=n_ctx)
    grid_spec = pltpu.PrefetchScalarGridSpec(
        num_scalar_prefetch=1,
        grid=(bsz, nqb, ATT_KV_HEADS),
        in_specs=[pl.BlockSpec((q_rows, blk), lambda b, qi, kvh, s: (kvh, b * cpb + cb0 + qi)),
                  ctx(k_rb), band(k_rb, -1), band(k_rb, 0), band(k_rb, 1),
                  ctx(v_rb), band(v_rb, -1), band(v_rb, 0), band(v_rb, 1)],
        out_specs=pl.BlockSpec((blk, q_rows), lambda b, qi, kvh, s: (b * nqb + qi, kvh)),
        scratch_shapes=[pltpu.VMEM((q_rows, blk), f32)],
    )
    return pl.pallas_call(
        kern,
        grid_spec=grid_spec,
        out_shape=jax.ShapeDtypeStruct((bsz * n_lat, ATT_HEADS * hd), bf16),
        compiler_params=_cparams("parallel", "parallel", "parallel"),
        name="window_attention",
    )(sink, *([qkv_t] * 9))


def _first_max(v, idx, n):
    m = jnp.max(v, axis=0, keepdims=True)
    first = jnp.min(jnp.where(v == m, idx, n), axis=0, keepdims=True)
    return m, first


def _route_kernel(h_ref, g_ref, mod_ref, modc_ref, rw_ref, rb_ref, wsg_ref, wsu_ref, wsd_ref,
                  m_ref, sh_ref, te_ref, wt_ref, pos_ref, cnt_ref, carry_s, *, n_ctx, tpb):
    i = pl.program_id(0)
    T = h_ref.shape[0]
    E, NG, GS = N_EXPERTS, N_EXPERT_GROUPS, GROUP_SIZE

    @pl.when(i == 0)
    def _():
        carry_s[...] = jnp.zeros_like(carry_s)

    x = h_ref[...]
    ms = jnp.mean(x * x, axis=-1, keepdims=True)
    y = x * lax.rsqrt(ms + EPS) * g_ref[...]
    scale = mod_ref[0, 4:5, :]
    shift = mod_ref[0, 3:4, :]
    if n_ctx:
        ctx_rows = jnp.clip(n_ctx - (i % tpb) * T, 0, T)
        is_ctx = lax.broadcasted_iota(i32, (T, 1), 0) < ctx_rows
        scale = jnp.where(is_ctx, modc_ref[0, 4:5, :], scale)
        shift = jnp.where(is_ctx, modc_ref[0, 3:4, :], shift)
    mt = y * (1.0 + scale) + shift
    m_ref[...] = _pack_bf16_pair(mt)
    mb = mt.astype(bf16)

    hg = jnp.dot(mb, wsg_ref[...], preferred_element_type=f32)
    hu = jnp.dot(mb, wsu_ref[...], preferred_element_type=f32)
    act = (jax.nn.silu(hg) * hu).astype(bf16)
    sh_ref[...] = jnp.dot(act, wsd_ref[...], preferred_element_type=f32)

    logits = lax.dot_general(rw_ref[...], mt, _NT, precision=HIGHEST,
                             preferred_element_type=f32)
    scores = jax.nn.sigmoid(logits)
    sel = scores + rb_ref[...]
    eidx = lax.broadcasted_iota(i32, (E, T), 0)
    midx = lax.broadcasted_iota(i32, (GS, T), 0)
    gidx = lax.broadcasted_iota(i32, (NG, T), 0)

    gs = jnp.zeros((NG, T), f32)
    for g in range(NG):
        blk = sel[g * GS:(g + 1) * GS, :]
        m1, f1 = _first_max(blk, midx, GS)
        m2 = jnp.max(jnp.where(midx == f1, -jnp.inf, blk), axis=0, keepdims=True)
        gs = jnp.where(gidx == g, m1 + m2, gs)
    gmask = jnp.zeros((NG, T), jnp.bool_)
    work = gs
    for _ in range(TOP_GROUPS):
        _, f = _first_max(work, gidx, NG)
        hit = gidx == f
        gmask = jnp.logical_or(gmask, hit)
        work = jnp.where(hit, -jnp.inf, work)
    cand = jnp.concatenate(
        [jnp.where(gmask[g:g + 1, :], sel[g * GS:(g + 1) * GS, :], -jnp.inf) for g in range(NG)],
        axis=0)
    hits = []
    chosen = jnp.zeros((E, T), jnp.bool_)
    work = cand
    for k in range(TOP_K):
        _, f = _first_max(work, eidx, E)
        hit = eidx == f
        hits.append(hit)
        chosen = jnp.logical_or(chosen, hit)
        work = jnp.where(hit, -jnp.inf, work)
        te_ref[k:k + 1, :] = f
    wts = [jnp.sum(jnp.where(h, scores, 0.0), axis=0, keepdims=True) for h in hits]
    tot = wts[0]
    for w in wts[1:]:
        tot = tot + w
    for k in range(TOP_K):
        wt_ref[k:k + 1, :] = wts[k] / (tot + 1e-20) * ROUTED_SCALE

    t0 = lax.broadcasted_iota(i32, (T, T), 0)
    t1 = lax.broadcasted_iota(i32, (T, T), 1)
    before = (t0 < t1).astype(bf16)
    chosen_f = chosen.astype(f32)
    cnt = jnp.dot(chosen_f.astype(bf16), before, preferred_element_type=f32) + carry_s[:, 0:1]
    for k in range(TOP_K):
        pos_ref[k:k + 1, :] = jnp.sum(jnp.where(hits[k], cnt, 0.0), axis=0,
                                      keepdims=True).astype(i32)
    new_carry = carry_s[...] + jnp.sum(chosen_f, axis=1, keepdims=True)
    carry_s[...] = new_carry
    cnt_ref[...] = new_carry.astype(i32)


def moe_route(h, g, mod3, rw_t, rbias, wsg, wsu, wsd, geom, lat_only):
    m, d = h.shape
    bsz, L, n_ctx = geom
    if lat_only:
        T = _pick_tile(L - n_ctx, (ROUTE_TILE_LAT, ROW_TILE))
        tpb = (L - n_ctx) // T
        kern = functools.partial(_route_kernel, n_ctx=0, tpb=tpb)
    else:
        T = _pick_tile(L, (ROUTE_TILE_JOINT, ROW_TILE))
        tpb = L // T
        kern = functools.partial(_route_kernel, n_ctx=n_ctx, tpb=tpb)
    E = N_EXPERTS
    hid = wsg.shape[1]
    const2 = lambda i: (0, 0)
    outs = pl.pallas_call(
        kern,
        grid=(m // T,),
        in_specs=[pl.BlockSpec((T, d), lambda i: (i, 0)),
                  pl.BlockSpec((1, d), const2),
                  pl.BlockSpec((1, 6, d), lambda i: (i // tpb, 0, 0)),
                  pl.BlockSpec((1, 6, d), lambda i: (bsz, 0, 0)),
                  pl.BlockSpec((E, d), const2),
                  pl.BlockSpec((E, 1), const2),
                  pl.BlockSpec((d, hid), const2),
                  pl.BlockSpec((d, hid), const2),
                  pl.BlockSpec((hid, d), const2)],
        out_specs=[pl.BlockSpec((T, d // 2), lambda i: (i, 0)),
                   pl.BlockSpec((T, d), lambda i: (i, 0)),
                   pl.BlockSpec((TOP_K, T), lambda i: (0, i)),
                   pl.BlockSpec((TOP_K, T), lambda i: (0, i)),
                   pl.BlockSpec((TOP_K, T), lambda i: (0, i)),
                   pl.BlockSpec((E, 128), const2)],
        out_shape=[jax.ShapeDtypeStruct((m, d // 2), jnp.uint32),
                   jax.ShapeDtypeStruct((m, d), f32),
                   jax.ShapeDtypeStruct((TOP_K, m), i32),
                   jax.ShapeDtypeStruct((TOP_K, m), f32),
                   jax.ShapeDtypeStruct((TOP_K, m), i32),
                   jax.ShapeDtypeStruct((E, 128), i32)],
        scratch_shapes=[pltpu.VMEM((E, 128), f32)],
        compiler_params=_cparams("arbitrary"),
        name="moe_route",
    )(h, g.reshape(1, d), mod3, mod3, rw_t, rbias.reshape(E, 1), wsg, wsu, wsd)
    return outs


GATHER_AHEAD = 2
GATHER_SLOTS = GATHER_AHEAD + 1


def _gmm_kernel(be_ref, nused_ref, tokc_ref, tokn_ref, tokf_ref, m_hbm, wg_ref, wu_ref,
                wd_ref, o_ref, xbuf, wg_s, wu_s, wd_s, sem):
    i = pl.program_id(0)
    nused = nused_ref[0]
    slot = i % GATHER_SLOTS
    far_slot = (i + GATHER_AHEAD) % GATHER_SLOTS

    def start_gather(tok_ref, dst_slot):
        for r in range(MOE_BLK):
            t = tok_ref[0, 0, r]
            pltpu.make_async_copy(m_hbm.at[pl.ds(t, 1), :],
                                  xbuf.at[dst_slot, pl.ds(r, 1), :],
                                  sem.at[dst_slot]).start()

    def wait_gather(dst_slot):
        pltpu.make_async_copy(m_hbm.at[pl.ds(0, MOE_BLK), :], xbuf.at[dst_slot],
                              sem.at[dst_slot]).wait()

    @pl.when(jnp.logical_and(i == 0, nused > 0))
    def _():
        start_gather(tokc_ref, 0)
        start_gather(tokn_ref, 1)

    prev = be_ref[jnp.maximum(i - 1, 0)]
    new_expert = jnp.logical_or(i == 0, be_ref[i] != prev)

    @pl.when(jnp.logical_and(i < nused, new_expert))
    def _():
        wg_s[...] = wg_ref[...].astype(bf16)
        wu_s[...] = wu_ref[...].astype(bf16)
        wd_s[...] = wd_ref[...].astype(bf16)

    @pl.when(i < nused)
    def _():
        wait_gather(slot)
        x_lo, x_hi = _unpack_bf16_pair(xbuf[slot])
        x_lo = x_lo.astype(bf16)
        x_hi = x_hi.astype(bf16)
        start_gather(tokf_ref, far_slot)
        kh = x_lo.shape[1]
        hg = (jnp.dot(x_lo, wg_s[:kh, :], preferred_element_type=f32)
              + jnp.dot(x_hi, wg_s[kh:, :], preferred_element_type=f32))
        hu = (jnp.dot(x_lo, wu_s[:kh, :], preferred_element_type=f32)
              + jnp.dot(x_hi, wu_s[kh:, :], preferred_element_type=f32))
        act = (jax.nn.silu(hg) * hu).astype(bf16)
        o_ref[...] = _pack_bf16_pair(jnp.dot(act, wd_s[...], preferred_element_type=f32))

    @pl.when(i >= nused)
    def _():
        o_ref[...] = jnp.zeros_like(o_ref)

    @pl.when(jnp.logical_and(jnp.logical_and(i >= nused, i < nused + GATHER_AHEAD), nused > 0))
    def _():
        wait_gather(slot)


def moe_experts(m_rows, slot_tok, block_expert, n_used, wg, wu, wd, layer):
    n_steps = block_expert.shape[0]
    d = wg.shape[2]
    dp = m_rows.shape[1]
    hid = wg.shape[3]
    tok3 = slot_tok.reshape(n_steps, 1, MOE_BLK)
    smem_blk = lambda imap: pl.BlockSpec((1, 1, MOE_BLK), imap, memory_space=pltpu.SMEM)
    grid_spec = pltpu.PrefetchScalarGridSpec(
        num_scalar_prefetch=2,
        grid=(n_steps,),
        in_specs=[smem_blk(lambda i, be, nu: (i, 0, 0)),
                  smem_blk(lambda i, be, nu: (jnp.minimum(i + 1, n_steps - 1), 0, 0)),
                  smem_blk(lambda i, be, nu: (jnp.minimum(i + GATHER_AHEAD, n_steps - 1), 0, 0)),
                  pl.BlockSpec(memory_space=pl.ANY),
                  pl.BlockSpec((None, None, d, hid), lambda i, be, nu: (layer, be[i], 0, 0)),
                  pl.BlockSpec((None, None, d, hid), lambda i, be, nu: (layer, be[i], 0, 0)),
                  pl.BlockSpec((None, None, hid, d), lambda i, be, nu: (layer, be[i], 0, 0))],
        out_specs=pl.BlockSpec((MOE_BLK, dp), lambda i, be, nu: (i, 0)),
        scratch_shapes=[pltpu.VMEM((GATHER_SLOTS, MOE_BLK, dp), jnp.uint32),
                        pltpu.VMEM((d, hid), bf16), pltpu.VMEM((d, hid), bf16),
                        pltpu.VMEM((hid, d), bf16),
                        pltpu.SemaphoreType.DMA((GATHER_SLOTS,))],
    )
    return pl.pallas_call(
        _gmm_kernel,
        grid_spec=grid_spec,
        out_shape=jax.ShapeDtypeStruct((n_steps * MOE_BLK, dp), jnp.uint32),
        compiler_params=_cparams("arbitrary"),
        name="moe_experts",
    )(block_expert, n_used, tok3, tok3, tok3, m_rows, wg, wu, wd)


def _comb_kernel(*refs, n_tiles, with_next):
    dc_ref, dn_ref, df_ref, ys_hbm, w_ref, sh_ref, h_ref, mod_ref = refs[:8]
    if with_next:
        gn_ref, modn_ref, o_ref, a_ref = refs[8:12]
        gbufs, sem = refs[12:12 + GATHER_SLOTS], refs[12 + GATHER_SLOTS]
    else:
        o_ref = refs[8]
        gbufs, sem = refs[9:9 + GATHER_SLOTS], refs[9 + GATHER_SLOTS]
    i = pl.program_id(0)
    T = COMB_TILE
    S = GATHER_SLOTS

    def start_gather(d_ref, p):
        for r in range(T):
            for k in range(TOP_K):
                t = d_ref[0, k, r]
                pltpu.make_async_copy(ys_hbm.at[pl.ds(t, 1), :],
                                      gbufs[p].at[k, pl.ds(r, 1), :],
                                      sem.at[p]).start()

    @pl.when(i == 0)
    def _():
        start_gather(dc_ref, 0)
        start_gather(dn_ref, 1)

    def step(p, issue_far):
        for k in range(TOP_K):
            pltpu.make_async_copy(ys_hbm.at[pl.ds(0, T), :], gbufs[p].at[k],
                                  sem.at[p]).wait()
        if issue_far:
            start_gather(df_ref, (p + GATHER_AHEAD) % S)
        w = w_ref[...]
        half = o_ref.shape[1] // 2
        acc_lo = jnp.zeros((T, half), f32)
        acc_hi = jnp.zeros((T, half), f32)
        for k in range(TOP_K):
            lo, hi = _unpack_bf16_pair(gbufs[p][k])
            acc_lo = acc_lo + lo * w[:, k:k + 1]
            acc_hi = acc_hi + hi * w[:, k:k + 1]
        gate = mod_ref[0, 5:6, :]
        o_lo = h_ref[:, :half] + gate[:, :half] * (acc_lo + sh_ref[:, :half])
        o_hi = h_ref[:, half:] + gate[:, half:] * (acc_hi + sh_ref[:, half:])
        o_ref[:, :half] = o_lo
        o_ref[:, half:] = o_hi
        if with_next:
            ssq = (jnp.sum(o_lo * o_lo, axis=-1, keepdims=True)
                   + jnp.sum(o_hi * o_hi, axis=-1, keepdims=True))
            inv = lax.rsqrt(ssq / (2 * half) + EPS)
            scale = 1.0 + modn_ref[0, 1:2, :]
            shift = modn_ref[0, 0:1, :]
            gn = gn_ref[...]
            a_ref[:, :half] = (o_lo * inv * gn[:, :half] * scale[:, :half]
                               + shift[:, :half]).astype(a_ref.dtype)
            a_ref[:, half:] = (o_hi * inv * gn[:, half:] * scale[:, half:]
                               + shift[:, half:]).astype(a_ref.dtype)

    first_tail = n_tiles - GATHER_AHEAD
    for p in range(S):
        @pl.when(jnp.logical_and(i % S == p, i < first_tail))
        def _(p=p):
            step(p, True)

    for tail in range(first_tail, n_tiles):
        @pl.when(i == tail)
        def _(tail=tail):
            step(tail % S, False)


def moe_combine(y_sorted, dest_t, wts_t, shared, h, mod3, geom, lat_only, next_norm=None):
    m, d = h.shape
    bsz, L, n_ctx = geom
    T = COMB_TILE
    n_tiles = m // T
    if lat_only:
        lpb = (L - n_ctx) // T
        mod_row = lambda i: i // lpb
    else:
        mod_row = _mod_row_map(geom, T)
    smem_blk = lambda imap: pl.BlockSpec((1, TOP_K, T), imap, memory_space=pltpu.SMEM)
    row_blk = pl.BlockSpec((T, d), lambda i: (i, 0))
    mod_blk = pl.BlockSpec((1, 6, d), lambda i: (mod_row(i), 0, 0))
    in_specs = [smem_blk(lambda i: (i, 0, 0)),
                smem_blk(lambda i: (jnp.minimum(i + 1, n_tiles - 1), 0, 0)),
                smem_blk(lambda i: (jnp.minimum(i + GATHER_AHEAD, n_tiles - 1), 0, 0)),
                pl.BlockSpec(memory_space=pl.ANY),
                pl.BlockSpec((T, TOP_K), lambda i: (i, 0)),
                row_blk, row_blk, mod_blk]
    args = [dest_t, dest_t, dest_t, y_sorted, wts_t, shared, h, mod3]
    out_specs, out_shape = row_blk, jax.ShapeDtypeStruct((m, d), f32)
    if next_norm is not None:
        in_specs += [pl.BlockSpec((1, d), lambda i: (0, 0)), mod_blk]
        args += [next_norm[0].reshape(1, d), next_norm[1]]
        out_specs = [row_blk, row_blk]
        out_shape = [out_shape, jax.ShapeDtypeStruct((m, d), bf16)]
    return pl.pallas_call(
        functools.partial(_comb_kernel, n_tiles=n_tiles, with_next=next_norm is not None),
        grid=(n_tiles,),
        in_specs=in_specs,
        out_specs=out_specs,
        out_shape=out_shape,
        scratch_shapes=[pltpu.VMEM((TOP_K, T, d // 2), jnp.uint32)] * GATHER_SLOTS
        + [pltpu.SemaphoreType.DMA((GATHER_SLOTS,))],
        compiler_params=_cparams("arbitrary"),
        name="moe_combine",
    )(*args)


TOK_SPLIT_LOG2 = 7


def _slot_kernel(dest_ref, hi_ref, lo_ref):
    i = pl.program_id(0)
    T = dest_ref.shape[1]
    nb = hi_ref.shape[0]

    @pl.when(i == 0)
    def _():
        hi_ref[...] = jnp.zeros_like(hi_ref)
        lo_ref[...] = jnp.zeros_like(lo_ref)

    d = dest_ref[...]
    tok = i * T + lax.broadcasted_iota(i32, (1, T), 1)
    tok_hi = (tok >> TOK_SPLIT_LOG2).astype(f32)
    tok_lo = (tok & ((1 << TOK_SPLIT_LOG2) - 1)).astype(f32)
    brow = lax.broadcasted_iota(i32, (nb, T), 0)
    orow = lax.broadcasted_iota(i32, (MOE_BLK, T), 0)
    a_hi, a_lo, b = [], [], []
    for k in range(TOP_K):
        dk = d[k:k + 1, :]
        hit = brow == (dk >> MOE_BLK_LOG2)
        a_hi.append(jnp.where(hit, tok_hi, 0.0).astype(bf16))
        a_lo.append(jnp.where(hit, tok_lo, 0.0).astype(bf16))
        b.append((orow == (dk & (MOE_BLK - 1))).astype(f32).astype(bf16))
    bm = jnp.concatenate(b, axis=1)
    hi_ref[...] += lax.dot_general(jnp.concatenate(a_hi, axis=1), bm, _NT,
                                   preferred_element_type=f32)
    lo_ref[...] += lax.dot_general(jnp.concatenate(a_lo, axis=1), bm, _NT,
                                   preferred_element_type=f32)


def slot_table(dest, n_steps):
    m = dest.shape[1]
    T = ROW_TILE
    nb = -(-n_steps // 8) * 8
    hi, lo = pl.pallas_call(
        _slot_kernel,
        grid=(m // T,),
        in_specs=[pl.BlockSpec((TOP_K, T), lambda i: (0, i))],
        out_specs=[pl.BlockSpec((nb, MOE_BLK), lambda i: (0, 0)),
                   pl.BlockSpec((nb, MOE_BLK), lambda i: (0, 0))],
        out_shape=[jax.ShapeDtypeStruct((nb, MOE_BLK), f32),
                   jax.ShapeDtypeStruct((nb, MOE_BLK), f32)],
        compiler_params=_cparams("arbitrary"),
        name="slot_table",
    )(dest)
    tab = hi.astype(i32) * (1 << TOK_SPLIT_LOG2) + lo.astype(i32)
    return tab[:n_steps]


def moe_layer(h, g2, mod3, layer, rw, rbias, wg, wu, wd, wsg, wsu, wsd, geom, lat_only,
              next_norm=None):
    m, d = h.shape
    E = N_EXPERTS
    mt, shared, top_e, wts, pos, cnt = moe_route(
        h, g2, mod3, rw.T, rbias, wsg.astype(bf16), wsu.astype(bf16), wsd.astype(bf16),
        geom, lat_only)
    counts = cnt[:, 0]
    padded = ((counts + MOE_BLK - 1) // MOE_BLK) * MOE_BLK
    pad_end = jnp.cumsum(padded)
    pad_start = pad_end - padded
    eids = jnp.arange(E, dtype=i32)[:, None, None]
    dest = pos + jnp.sum(jnp.where(top_e[None] == eids, pad_start[:, None, None], 0), axis=0)
    n_steps = (m * TOP_K) // MOE_BLK + E + GATHER_AHEAD
    slot_tok = slot_table(dest, n_steps)
    block_start = jnp.arange(n_steps, dtype=i32) * MOE_BLK
    block_expert = jnp.minimum(
        jnp.sum((pad_end[None, :] <= block_start[:, None]).astype(i32), axis=1), E - 1)
    n_used = (pad_end[-1:] // MOE_BLK).astype(i32)
    y_sorted = moe_experts(mt, slot_tok, block_expert, n_used, wg, wu, wd, layer)
    T = COMB_TILE
    dest_t = dest.reshape(TOP_K, m // T, T).transpose(1, 0, 2)
    return moe_combine(y_sorted, dest_t, wts.T, shared, h, mod3, geom, lat_only, next_norm)


def _rope_tables_t(n_ctx, n_lat):
    n_rows = n_lat // GRID_W
    rows = jnp.repeat(jnp.arange(n_rows, dtype=f32), GRID_W)
    cols = jnp.tile(jnp.arange(GRID_W, dtype=f32), n_rows)
    n_freq = ATT_HEADDIM // 4
    inv_freq = ROPE_BASE ** (-jnp.arange(n_freq, dtype=f32) / n_freq)
    ar = rows[None, :] * inv_freq[:, None]
    ac = cols[None, :] * inv_freq[:, None]
    cos_t = jnp.concatenate([jnp.cos(ar), jnp.cos(ar), jnp.cos(ac), jnp.cos(ac)], axis=0)
    sin_t = jnp.concatenate([-jnp.sin(ar), jnp.sin(ar), -jnp.sin(ac), jnp.sin(ac)], axis=0)
    cos_t = jnp.concatenate([jnp.ones((ATT_HEADDIM, n_ctx), f32), cos_t], axis=1)
    sin_t = jnp.concatenate([jnp.zeros((ATT_HEADDIM, n_ctx), f32), sin_t], axis=1)
    return cos_t, sin_t


def even_layer_mix(h, a, mod3, w_in, gm_ws, gm_bs, gm_ln_g, gm_ln_b, conv_w, conv_b,
                   a_log, dt_bias, d_skip, ssd_norm_g, w_out, geom):
    n_main = w_in.shape[1] - 2 * SSD_HEADS
    proj = matmul_nn(a, w_in[:, :n_main].astype(bf16), f32)
    w_dt = w_in[:, n_main:].reshape(-1, 2, SSD_GROUPS, SSD_HPG).transpose(2, 1, 3, 0)
    dt_t = dt_project(a, w_dt.reshape(2 * SSD_HEADS, -1))
    bexp = jnp.repeat(gm_bs.T, A_CHUNK, axis=1)
    g_out = gmlp_mix(proj, gm_ws.astype(bf16), bexp, gm_ln_g, gm_ln_b)
    y_out = ssd_mix(proj, dt_t, conv_w, conv_b, a_log, dt_bias, d_skip, ssd_norm_g, geom)
    aw = gm_ws.shape[0] * gm_ws.shape[1]
    w_o = w_out.astype(bf16)
    return out_project_joint([g_out, y_out], [w_o[:aw], w_o[aw:]], h, mod3, geom)


def odd_layer_mix_lat(h, a, mod3, w_q, w_kv, q_norm_g, k_norm_g, sink, w_o, geom):
    bsz, L, n_ctx = geom
    w_t = jnp.concatenate([w_q, w_kv], axis=1).astype(bf16)
    cos_t, sin_t = _rope_tables_t(n_ctx, L - n_ctx)
    qkv_t = qkv_project_t(a, w_t, q_norm_g, k_norm_g, cos_t, sin_t, geom)
    att = window_attention(qkv_t, sink, geom)
    return out_project_lat(att, w_o.astype(bf16), h, mod3, geom)


def kernel(x, c, ctx, c_ctx, mod_w, mod_b, norm1_g, norm2_g, ev_w_in, ev_gm_ws, ev_gm_bs, ev_gm_ln_g, ev_gm_ln_b, ev_conv_w, ev_conv_b, ev_a_log, ev_dt_bias, ev_d_skip, ev_ssd_norm_g, ev_w_out, od_w_q, od_w_kv, od_q_norm_g, od_k_norm_g, od_sink, od_w_o, moe_router_w, moe_router_bias, moe_w_gate, moe_w_up, moe_w_down, moe_ws_gate, moe_ws_up, moe_ws_down):
    bsz, n_lat, d = x.shape
    n_ctx = ctx.shape[1]
    L = n_ctx + n_lat
    geom = (bsz, L, n_ctx)
    depth = mod_w.shape[0]
    assert depth == 2 and n_ctx == ROW_TILE and n_lat % ROW_TILE == 0 and bsz < 16

    c_all = jnp.zeros((16, d), f32).at[:bsz].set(c).at[bsz].set(c_ctx)
    mod3 = modulation(c_all, mod_w, mod_b, 0).reshape(16, 6, d)
    mod3_1 = modulation(c_all, mod_w, mod_b, 1).reshape(16, 6, d)

    h, a = join_norm_modulate(ctx.reshape(bsz * n_ctx, d), x.reshape(bsz * n_lat, d),
                              norm1_g[0], mod3, geom)
    h = even_layer_mix(h, a, mod3, ev_w_in[0], ev_gm_ws[0], ev_gm_bs[0], ev_gm_ln_g[0],
                       ev_gm_ln_b[0], ev_conv_w[0], ev_conv_b[0], ev_a_log[0], ev_dt_bias[0],
                       ev_d_skip[0], ev_ssd_norm_g[0], ev_w_out[0], geom)
    h, a = moe_layer(h, norm2_g[0], mod3, 0, moe_router_w[0], moe_router_bias[0], moe_w_gate,
                     moe_w_up, moe_w_down, moe_ws_gate[0], moe_ws_up[0], moe_ws_down[0],
                     geom, lat_only=False, next_norm=(norm1_g[1], mod3_1))

    h_lat = odd_layer_mix_lat(h, a, mod3_1, od_w_q[0], od_w_kv[0], od_q_norm_g[0],
                              od_k_norm_g[0], od_sink[0], od_w_o[0], geom)
    h_lat = moe_layer(h_lat, norm2_g[1], mod3_1, 1, moe_router_w[1], moe_router_bias[1],
                      moe_w_gate, moe_w_up, moe_w_down, moe_ws_gate[1], moe_ws_up[1],
                      moe_ws_down[1], geom, lat_only=True)
    return h_lat.reshape(bsz, n_lat, d)
```

```python
import functools

import jax
import jax.numpy as jnp
from jax import lax
from jax.experimental import pallas as pl
from jax.experimental.pallas import tpu as pltpu

f32 = jnp.float32
bf16 = jnp.bfloat16
i32 = jnp.int32
HIGHEST = lax.Precision.HIGHEST

D_MODEL = 2048
EPS = 1e-6
GRID_W = 64

A_CHUNK = 128
A_GROUPS = 16

SSD_HEADS = 32
SSD_HEADDIM = 64
SSD_GROUPS = 8
SSD_STATE = 128
SSD_CHUNK = 128
SSD_HPG = SSD_HEADS // SSD_GROUPS
SSD_GW = SSD_HPG * SSD_HEADDIM

ATT_HEADDIM = 64
ATT_HEADS = 32
ATT_KV_HEADS = 4
ATT_Q_PER_KV = 8
ATT_WINDOW = 128
ATT_BLOCK = 128
ATT_SCALE = ATT_HEADDIM ** -0.5
ROPE_BASE = 10000.0

N_EXPERTS = 64
N_EXPERT_GROUPS = 8
GROUP_SIZE = N_EXPERTS // N_EXPERT_GROUPS
TOP_GROUPS = 4
TOP_K = 8
EXPERT_HIDDEN = 512
ROUTED_SCALE = 2.5

ROW_TILE = 256
BIG_TILE = 768
ROUTE_TILE_JOINT = 384
ROUTE_TILE_LAT = 512
MOE_BLK = 512
MOE_BLK_LOG2 = 9
COMB_TILE = 64
VMEM_LIMIT = 56 * 1024 * 1024

_NT = (((1,), (1,)), ((), ()))
_TN = (((0,), (0,)), ((), ()))


def _cparams(*sem):
    return pltpu.CompilerParams(dimension_semantics=sem, vmem_limit_bytes=VMEM_LIMIT)


def _pack_bf16_pair(x):
    c = x.shape[1] // 2
    lo = lax.bitcast_convert_type(x[:, :c].astype(bf16).astype(f32), jnp.uint32)
    hi = lax.bitcast_convert_type(x[:, c:].astype(bf16).astype(f32), jnp.uint32)
    return (lo >> 16) | hi


def _unpack_bf16_pair(p):
    lo = lax.bitcast_convert_type(p << 16, f32)
    hi = lax.bitcast_convert_type(p & jnp.uint32(0xFFFF0000), f32)
    return lo, hi


def _pick_tile(m, options):
    for t in options:
        if m % t == 0:
            return t
    raise ValueError(f"no tile for {m}")


def _mod_kernel(c_ref, w_ref, b_ref, o_ref):
    sc = jax.nn.silu(c_ref[...])
    o_ref[...] = jnp.dot(sc, w_ref[...], precision=HIGHEST,
                         preferred_element_type=f32) + b_ref[...]


def modulation(c_all, mod_w, mod_b, layer):
    rows, d = c_all.shape
    n = mod_w.shape[2]
    tn = 1024
    return pl.pallas_call(
        _mod_kernel,
        grid=(n // tn,),
        in_specs=[pl.BlockSpec((rows, d), lambda j: (0, 0)),
                  pl.BlockSpec((None, d, tn), lambda j: (layer, 0, j)),
                  pl.BlockSpec((None, 1, tn), lambda j: (layer, 0, j))],
        out_specs=pl.BlockSpec((rows, tn), lambda j: (0, j)),
        out_shape=jax.ShapeDtypeStruct((rows, n), f32),
        compiler_params=_cparams("arbitrary"),
        name="modulation",
    )(c_all, mod_w, mod_b.reshape(mod_b.shape[0], 1, n))


def _mod_row_map(geom, tile):
    bsz, L, n_ctx = geom
    tpb = L // tile
    nct = n_ctx // tile

    def mod_row(i):
        return jnp.where((i % tpb) < nct, bsz, i // tpb)
    return mod_row


def _lat_tile_map(geom, tile):
    bsz, L, n_ctx = geom
    lpb = (L - n_ctx) // tile
    tpb = L // tile
    nct = n_ctx // tile

    def joint(t):
        return (t // lpb) * tpb + nct + (t % lpb)
    return joint


def _norm_mod(x, g, mod_ref, shift_idx, scale_idx):
    ms = jnp.mean(x * x, axis=-1, keepdims=True)
    y = x * lax.rsqrt(ms + EPS) * g
    return y * (1.0 + mod_ref[0, scale_idx:scale_idx + 1, :]) + mod_ref[0, shift_idx:shift_idx + 1, :]


def _join_norm_kernel(ctx_ref, x_ref, g_ref, mod_ref, h_ref, a_ref, *, tpb, nct):
    is_ctx = (pl.program_id(0) % tpb) < nct

    @pl.when(is_ctx)
    def _():
        h_ref[...] = ctx_ref[...]

    @pl.when(jnp.logical_not(is_ctx))
    def _():
        h_ref[...] = x_ref[...]

    a_ref[...] = _norm_mod(h_ref[...], g_ref[...], mod_ref, 0, 1).astype(a_ref.dtype)


def join_norm_modulate(ctx2, x2, g, mod3, geom):
    bsz, L, n_ctx = geom
    d = x2.shape[1]
    T = ROW_TILE
    tpb, nct = L // T, n_ctx // T
    lpb = tpb - nct
    mod_row = _mod_row_map(geom, T)
    return pl.pallas_call(
        functools.partial(_join_norm_kernel, tpb=tpb, nct=nct),
        grid=(bsz * tpb,),
        in_specs=[pl.BlockSpec((T, d), lambda i: ((i // tpb) * nct + jnp.minimum(i % tpb, nct - 1), 0)),
                  pl.BlockSpec((T, d), lambda i: ((i // tpb) * lpb + jnp.maximum(i % tpb - nct, 0), 0)),
                  pl.BlockSpec((1, d), lambda i: (0, 0)),
                  pl.BlockSpec((1, 6, d), lambda i: (mod_row(i), 0, 0))],
        out_specs=[pl.BlockSpec((T, d), lambda i: (i, 0)),
                   pl.BlockSpec((T, d), lambda i: (i, 0))],
        out_shape=[jax.ShapeDtypeStruct((bsz * L, d), f32),
                   jax.ShapeDtypeStruct((bsz * L, d), bf16)],
        compiler_params=_cparams("parallel"),
        name="join_norm_modulate",
    )(ctx2, x2, g.reshape(1, d), mod3)


def _mm_kernel(a_ref, w_ref, o_ref):
    o_ref[...] = jnp.dot(a_ref[...], w_ref[...],
                         preferred_element_type=f32).astype(o_ref.dtype)


def matmul_nn(a, w, out_dtype, n=None, tn=1024):
    m, k = a.shape
    n = w.shape[1] if n is None else n
    assert n % tn == 0
    tm = _pick_tile(m, (1024, 768, 512, 256))
    return pl.pallas_call(
        _mm_kernel,
        grid=(n // tn, m // tm),
        in_specs=[pl.BlockSpec((tm, k), lambda j, i: (i, 0)),
                  pl.BlockSpec((k, tn), lambda j, i: (0, j))],
        out_specs=pl.BlockSpec((tm, tn), lambda j, i: (i, j)),
        out_shape=jax.ShapeDtypeStruct((m, n), out_dtype),
        compiler_params=_cparams("parallel", "parallel"),
        name="matmul_nn",
    )(a, w)


def _dt_kernel(a_ref, w_ref, o_ref):
    a = a_ref[...]
    acc = lax.dot_general(w_ref[0], a, _NT, preferred_element_type=f32)
    for p in range(1, w_ref.shape[0]):
        acc = acc + lax.dot_general(w_ref[p], a, _NT, preferred_element_type=f32)
    o_ref[...] = acc


def dt_project(a, w_dt_t):
    m, k = a.shape
    r = w_dt_t.shape[0]
    hi = w_dt_t.astype(bf16)
    r1 = w_dt_t - hi.astype(f32)
    mid = r1.astype(bf16)
    lo = (r1 - mid.astype(f32)).astype(bf16)
    w3 = jnp.stack([hi, mid, lo])
    tm = _pick_tile(m, (1024, 768, 512, 256))
    return pl.pallas_call(
        _dt_kernel,
        grid=(m // tm,),
        in_specs=[pl.BlockSpec((tm, k), lambda i: (i, 0)),
                  pl.BlockSpec((3, r, k), lambda i: (0, 0, 0))],
        out_specs=pl.BlockSpec((r, tm), lambda i: (0, i)),
        out_shape=jax.ShapeDtypeStruct((r, m), f32),
        compiler_params=_cparams("parallel"),
        name="dt_project",
    )(a, w3)


def _gelu_exact(x):
    return 0.5 * x * (1.0 + lax.erf(x * (2.0 ** -0.5)))


def _gmlp_kernel(u_ref, v_ref, ws_ref, bexp_ref, lng_ref, lnb_ref, o_ref):
    u = _gelu_exact(u_ref[...])
    v = _gelu_exact(v_ref[...])
    mu = jnp.mean(v, axis=-1, keepdims=True)
    vc = v - mu
    var = jnp.mean(vc * vc, axis=-1, keepdims=True)
    vn = (vc * lax.rsqrt(var + EPS) * lng_ref[...] + lnb_ref[...]).astype(bf16)
    for g in range(A_GROUPS):
        sl = slice(g * A_CHUNK, (g + 1) * A_CHUNK)
        mixed = jnp.dot(ws_ref[g], vn[:, sl], preferred_element_type=f32) + bexp_ref[:, sl]
        o_ref[:, sl] = (u[:, sl] * mixed).astype(o_ref.dtype)


def gmlp_mix(proj, ws_bf, bexp, ln_g, ln_b):
    m = proj.shape[0]
    w = A_GROUPS * A_CHUNK
    return pl.pallas_call(
        _gmlp_kernel,
        grid=(m // A_CHUNK,),
        in_specs=[pl.BlockSpec((A_CHUNK, w), lambda c: (c, 0)),
                  pl.BlockSpec((A_CHUNK, w), lambda c: (c, 1)),
                  pl.BlockSpec((A_GROUPS, A_CHUNK, A_CHUNK), lambda c: (0, 0, 0)),
                  pl.BlockSpec((A_CHUNK, w), lambda c: (0, 0)),
                  pl.BlockSpec((1, w), lambda c: (0, 0)),
                  pl.BlockSpec((1, w), lambda c: (0, 0))],
        out_specs=pl.BlockSpec((A_CHUNK, w), lambda c: (c, 0)),
        out_shape=jax.ShapeDtypeStruct((m, w), bf16),
        compiler_params=_cparams("parallel"),
        name="gmlp_mix",
    )(proj, proj, ws_bf, bexp, ln_g.reshape(1, w), ln_b.reshape(1, w))


def _head_expand(v, lane):
    out = v[:, 3:4]
    for r in (2, 1, 0):
        out = jnp.where(lane < SSD_HEADDIM * (r + 1), v[:, r:r + 1], out)
    return out


def _row_expand(v, row):
    out = v[:, 3:4]
    for r in (2, 1, 0):
        out = jnp.where(row < SSD_HEADDIM * (r + 1), v[:, r:r + 1], out)
    return out


def _ssd_kernel(x_ref, b_ref, c_ref, z_ref, dtr_ref, dtc_ref,
                cwx_ref, cwb_ref, cwc_ref, cbx_ref, cbb_ref, cbc_ref,
                alr_ref, alc_ref, dbr_ref, dbc_ref, dsk_ref, ng_ref,
                o_ref,
                xs_s, bs_s, cs_s, dtr_s, dtar_s, dtc_s, dtac_s, suf_s, yacc_s, sf_s, sb_s,
                *, n_ctx, L):
    Q = SSD_CHUNK
    nc = L // Q
    ncc = n_ctx // Q
    H = SSD_HPG

    bias_r = dbr_ref[...]
    a_r = -jnp.exp(alr_ref[...])
    for c in range(nc):
        dt = jax.nn.softplus(dtr_ref[:, c * Q:(c + 1) * Q] + bias_r)
        dtr_s[c] = dt
        dtar_s[c] = dt * a_r
    dtc = jax.nn.softplus(dtc_ref[0] + dbc_ref[0])
    dtc_s[...] = dtc
    dtac_s[...] = dtc * (-jnp.exp(alc_ref[0]))

    ii = lax.broadcasted_iota(i32, (Q, Q), 0)
    jj = lax.broadcasted_iota(i32, (Q, Q), 1)
    lower = ii >= jj
    upper = ii <= jj
    tril = lower.astype(f32)
    triu = upper.astype(f32)
    lane = lax.broadcasted_iota(i32, (Q, SSD_GW), 1)
    row = lax.broadcasted_iota(i32, (SSD_GW, SSD_STATE), 0)

    def conv_act(ref, w_ref, bias_ref, s):
        xv = ref[pl.ds(s, Q), :]
        w = xv.shape[1]
        r0 = lax.broadcasted_iota(i32, (Q, w), 0)
        prev = ref[pl.ds(jnp.maximum(s - 1, 0), 1), :]
        nxt = ref[pl.ds(jnp.minimum(s + Q, L - 1), 1), :]
        has_prev = jnp.logical_and(s != 0, s != n_ctx)
        has_next = jnp.logical_and(s + Q != n_ctx, s + Q != L)
        prev = jnp.where(has_prev, prev, 0.0)
        nxt = jnp.where(has_next, nxt, 0.0)
        xm1 = jnp.where(r0 == 0, prev, pltpu.roll(xv, 1, 0))
        xp1 = jnp.where(r0 == Q - 1, nxt, pltpu.roll(xv, Q - 1, 0))
        y = w_ref[0:1, :] * xm1 + w_ref[1:2, :] * xv + w_ref[2:3, :] * xp1 + bias_ref[...]
        return jax.nn.silu(y)

    sf_s[...] = jnp.zeros_like(sf_s)
    sb_s[...] = jnp.zeros_like(sb_s)

    def fwd_body(c, carry):
        s = pl.multiple_of(c * Q, Q)
        X = conv_act(x_ref, cwx_ref, cbx_ref, s)
        Bc = conv_act(b_ref, cwb_ref, cbb_ref, s)
        Cc = conv_act(c_ref, cwc_ref, cbc_ref, s)
        xs_s[pl.ds(s, Q), :] = X
        bs_s[pl.ds(s, Q), :] = Bc
        cs_s[pl.ds(s, Q), :] = Cc
        Bb = Bc.astype(bf16)
        Cb = Cc.astype(bf16)
        cb = lax.dot_general(Cb, Bb, _NT, preferred_element_type=f32)
        dta_c = dtac_s[pl.ds(s, Q), :]
        dta_r = dtar_s[c]
        dt_r = dtr_s[c]
        p_col = jnp.dot(tril, dta_c, precision=HIGHEST, preferred_element_type=f32)
        s_col = jnp.dot(triu, dta_c, precision=HIGHEST, preferred_element_type=f32)
        p_row = jnp.dot(dta_r, triu, precision=HIGHEST, preferred_element_type=f32)
        s_row = jnp.dot(dta_r, tril, precision=HIGHEST, preferred_element_type=f32)
        suf_s[pl.ds(s, Q), :] = s_col
        y = jnp.zeros((Q, SSD_GW), f32)
        for r in range(H):
            lf = jnp.exp(jnp.where(lower, p_col[:, r:r + 1] - p_row[r:r + 1, :], -jnp.inf))
            lb = jnp.exp(jnp.where(upper, s_col[:, H + r:H + r + 1] - s_row[H + r:H + r + 1, :],
                                   -jnp.inf))
            mr = cb * (lf * dt_r[r:r + 1, :] + lb * dt_r[H + r:H + r + 1, :])
            head = jnp.logical_and(lane >= SSD_HEADDIM * r, lane < SSD_HEADDIM * (r + 1))
            xm = jnp.where(head, X, 0.0).astype(bf16)
            y = y + jnp.dot(mr.astype(bf16), xm, preferred_element_type=f32)
        sf = sf_s[...]
        yi = lax.dot_general(Cb, sf.astype(bf16), _NT, preferred_element_type=f32)
        pf = p_col[:, 0:H]
        y = y + yi * _head_expand(jnp.exp(pf), lane)
        yacc_s[pl.ds(s, Q), :] = y
        tot = p_col[Q - 1:Q, 0:H]
        wf = dtc_s[pl.ds(s, Q), 0:H] * jnp.exp(tot - pf)
        xw = (X * _head_expand(wf, lane)).astype(bf16)
        upd = lax.dot_general(xw, Bb, _TN, preferred_element_type=f32)
        sf_s[...] = sf * _row_expand(jnp.exp(tot), row) + upd
        return carry

    lax.fori_loop(0, nc, fwd_body, 0, unroll=3)

    def bwd_body(k, carry):
        c = jnp.where(k < ncc, ncc - 1 - k, nc - 1 - (k - ncc))
        s = pl.multiple_of(c * Q, Q)
        X = xs_s[pl.ds(s, Q), :]
        Bb = bs_s[pl.ds(s, Q), :].astype(bf16)
        Cb = cs_s[pl.ds(s, Q), :].astype(bf16)
        ab = suf_s[pl.ds(s, Q), H:2 * H]
        tot = ab[0:1, :]
        sb = sb_s[...]
        yi = lax.dot_general(Cb, sb.astype(bf16), _NT, preferred_element_type=f32)
        y = yacc_s[pl.ds(s, Q), :] + yi * _head_expand(jnp.exp(ab), lane) + dsk_ref[...] * X
        wb = dtc_s[pl.ds(s, Q), H:2 * H] * jnp.exp(tot - ab)
        xw = (X * _head_expand(wb, lane)).astype(bf16)
        upd = lax.dot_general(xw, Bb, _TN, preferred_element_type=f32)
        sb_s[...] = sb * _row_expand(jnp.exp(tot), row) + upd
        y = y * jax.nn.silu(z_ref[pl.ds(s, Q), :])
        ms = jnp.mean(y * y, axis=-1, keepdims=True)
        o_ref[pl.ds(s, Q), :] = (y * lax.rsqrt(ms + EPS) * ng_ref[...]).astype(o_ref.dtype)
        return carry

    lax.fori_loop(0, nc, bwd_body, 0, unroll=2)


def ssd_mix(proj, dt_t, conv_w, conv_b, a_log, dt_bias, d_skip, norm_g, geom):
    bsz, L, n_ctx = geom
    m = proj.shape[0]
    G, H, GW, N = SSD_GROUPS, SSD_HPG, SSD_GW, SSD_STATE
    nc = L // SSD_CHUNK
    col0 = proj.shape[1] - (SSD_HEADS * SSD_HEADDIM + 2 * G * N)
    zc0 = col0 - SSD_HEADS * SSD_HEADDIM
    xb, zb = col0 // GW, zc0 // GW
    bb = (col0 + SSD_HEADS * SSD_HEADDIM) // N
    cb = bb + G
    cxb, cbb, ccb = 0, (SSD_HEADS * SSD_HEADDIM) // N, (SSD_HEADS * SSD_HEADDIM) // N + G

    def to_gdr(p):
        return p.reshape(2, G, H).transpose(1, 0, 2).reshape(G, 2 * H)

    al, db = to_gdr(a_log), to_gdr(dt_bias)
    dt_col = dt_t.reshape(G, 2 * H, m).transpose(0, 2, 1)
    dsk = jnp.repeat(d_skip, SSD_HEADDIM).reshape(1, -1)
    conv_b2 = conv_b.reshape(1, -1)
    kern = functools.partial(_ssd_kernel, n_ctx=n_ctx, L=L)
    return pl.pallas_call(
        kern,
        grid=(bsz, G),
        in_specs=[
            pl.BlockSpec((L, GW), lambda b, g: (b, xb + g)),
            pl.BlockSpec((L, N), lambda b, g: (b, bb + g)),
            pl.BlockSpec((L, N), lambda b, g: (b, cb + g)),
            pl.BlockSpec((L, GW), lambda b, g: (b, zb + g)),
            pl.BlockSpec((2 * H, L), lambda b, g: (g, b)),
            pl.BlockSpec((1, L, 2 * H), lambda b, g: (g, b, 0)),
            pl.BlockSpec((3, GW), lambda b, g: (0, cxb + g)),
            pl.BlockSpec((3, N), lambda b, g: (0, cbb + g)),
            pl.BlockSpec((3, N), lambda b, g: (0, ccb + g)),
            pl.BlockSpec((1, GW), lambda b, g: (0, cxb + g)),
            pl.BlockSpec((1, N), lambda b, g: (0, cbb + g)),
            pl.BlockSpec((1, N), lambda b, g: (0, ccb + g)),
            pl.BlockSpec((2 * H, 1), lambda b, g: (g, 0)),
            pl.BlockSpec((1, 1, 2 * H), lambda b, g: (g, 0, 0)),
            pl.BlockSpec((2 * H, 1), lambda b, g: (g, 0)),
            pl.BlockSpec((1, 1, 2 * H), lambda b, g: (g, 0, 0)),
            pl.BlockSpec((1, GW), lambda b, g: (0, g)),
            pl.BlockSpec((1, GW), lambda b, g: (0, g)),
        ],
        out_specs=pl.BlockSpec((L, GW), lambda b, g: (b, g)),
        out_shape=jax.ShapeDtypeStruct((m, G * GW), bf16),
        scratch_shapes=[
            pltpu.VMEM((L, GW), f32), pltpu.VMEM((L, N), f32), pltpu.VMEM((L, N), f32),
            pltpu.VMEM((nc, 2 * H, SSD_CHUNK), f32), pltpu.VMEM((nc, 2 * H, SSD_CHUNK), f32),
            pltpu.VMEM((L, 2 * H), f32), pltpu.VMEM((L, 2 * H), f32), pltpu.VMEM((L, 2 * H), f32),
            pltpu.VMEM((L, GW), f32),
            pltpu.VMEM((GW, N), f32), pltpu.VMEM((GW, N), f32),
        ],
        compiler_params=_cparams("parallel", "parallel"),
        name="ssd_mix",
    )(proj, proj, proj, proj, dt_t, dt_col,
      conv_w, conv_w, conv_w, conv_b2, conv_b2, conv_b2,
      al.reshape(G * 2 * H, 1), al.reshape(G, 1, 2 * H),
      db.reshape(G * 2 * H, 1), db.reshape(G, 1, 2 * H),
      dsk, norm_g.reshape(1, -1))


def _outproj_kernel(*refs, n_in, n_ctx, tpb, tm):
    a_refs = refs[:n_in]
    w_refs = refs[n_in:2 * n_in]
    h_ref, ml_ref, mc_ref, o_ref = refs[2 * n_in:]
    i = pl.program_id(1)
    acc = jnp.dot(a_refs[0][...], w_refs[0][...], preferred_element_type=f32)
    for k in range(1, n_in):
        acc = acc + jnp.dot(a_refs[k][...], w_refs[k][...], preferred_element_type=f32)
    n_ctx_rows = jnp.clip(n_ctx - (i % tpb) * tm, 0, tm)
    row = lax.broadcasted_iota(i32, acc.shape, 0)
    gate = jnp.where(row < n_ctx_rows, mc_ref[0, 2:3, :], ml_ref[0, 2:3, :])
    o_ref[...] = h_ref[...] + gate * acc


def out_project_joint(a_list, w_list, h, mod3, geom, tn=1024):
    bsz, L, n_ctx = geom
    m, d = h.shape
    tm = _pick_tile(L, (BIG_TILE, 512, 256))
    tpb = L // tm
    n_in = len(a_list)
    k = a_list[0].shape[1]
    kern = functools.partial(_outproj_kernel, n_in=n_in, n_ctx=n_ctx, tpb=tpb, tm=tm)
    in_specs = ([pl.BlockSpec((tm, k), lambda j, i: (i, 0)) for _ in range(n_in)]
                + [pl.BlockSpec((k, tn), lambda j, i: (0, j)) for _ in range(n_in)]
                + [pl.BlockSpec((tm, tn), lambda j, i: (i, j)),
                   pl.BlockSpec((1, 6, tn), lambda j, i: (i // tpb, 0, j)),
                   pl.BlockSpec((1, 6, tn), lambda j, i: (bsz, 0, j))])
    return pl.pallas_call(
        kern,
        grid=(d // tn, m // tm),
        in_specs=in_specs,
        out_specs=pl.BlockSpec((tm, tn), lambda j, i: (i, j)),
        out_shape=jax.ShapeDtypeStruct((m, d), f32),
        compiler_params=_cparams("parallel", "parallel"),
        name="out_project_joint",
    )(*a_list, *w_list, h, mod3, mod3)


def _outproj_lat_kernel(a_ref, w_ref, h_ref, ml_ref, o_ref):
    acc = jnp.dot(a_ref[...], w_ref[...], preferred_element_type=f32)
    o_ref[...] = h_ref[...] + ml_ref[0, 2:3, :] * acc


def out_project_lat(a, w, h_joint, mod3, geom, tn=1024):
    bsz, L, n_ctx = geom
    m, k = a.shape
    d = h_joint.shape[1]
    tm = ROW_TILE
    joint = _lat_tile_map(geom, tm)
    lpb = (L - n_ctx) // tm
    return pl.pallas_call(
        _outproj_lat_kernel,
        grid=(d // tn, m // tm),
        in_specs=[pl.BlockSpec((tm, k), lambda j, i: (i, 0)),
                  pl.BlockSpec((k, tn), lambda j, i: (0, j)),
                  pl.BlockSpec((tm, tn), lambda j, i: (joint(i), j)),
                  pl.BlockSpec((1, 6, tn), lambda j, i: (i // lpb, 0, j))],
        out_specs=pl.BlockSpec((tm, tn), lambda j, i: (i, j)),
        out_shape=jax.ShapeDtypeStruct((m, d), f32),
        compiler_params=_cparams("parallel", "parallel"),
        name="out_project_lat",
    )(a, w, h_joint, mod3)


QKV_ROW_BLOCK = 512


def _qkv_kernel(a_ref, w_ref, gq_ref, gk_ref, cos_ref, sin_ref, o_ref, *, n_qblk):
    j = pl.program_id(1)
    acc = jnp.dot(a_ref[...], w_ref[...], preferred_element_type=f32).T
    hd = ATT_HEADDIM
    q4 = hd // 4
    cs = cos_ref[...]
    sn = sin_ref[...]

    def norm_rope(hh, g):
        blk = acc[hh * hd:(hh + 1) * hd, :]
        ms = jnp.mean(blk * blk, axis=0, keepdims=True)
        xn = blk * lax.rsqrt(ms + EPS) * g
        sw = jnp.concatenate([xn[q4:2 * q4], xn[0:q4], xn[3 * q4:4 * q4], xn[2 * q4:3 * q4]],
                             axis=0)
        o_ref[hh * hd:(hh + 1) * hd, :] = xn * cs + sw * sn

    @pl.when(j < n_qblk)
    def _():
        g = gq_ref[...] * ATT_SCALE
        for hh in range(acc.shape[0] // hd):
            norm_rope(hh, g)

    @pl.when(j == n_qblk)
    def _():
        g = gk_ref[...]
        for hh in range(ATT_KV_HEADS):
            norm_rope(hh, g)
        o_ref[ATT_KV_HEADS * hd:, :] = acc[ATT_KV_HEADS * hd:, :]


def qkv_project_t(a, w, gq, gk, cos_t, sin_t, geom):
    bsz, L, n_ctx = geom
    m, k = a.shape
    r = w.shape[1]
    rb = QKV_ROW_BLOCK
    assert 2 * ATT_KV_HEADS * ATT_HEADDIM == rb and r % rb == 0
    tm = _pick_tile(L, (BIG_TILE, 512, 256))
    tpb = L // tm
    n_qblk = (ATT_HEADS * ATT_HEADDIM) // rb
    return pl.pallas_call(
        functools.partial(_qkv_kernel, n_qblk=n_qblk),
        grid=(m // tm, r // rb),
        in_specs=[pl.BlockSpec((tm, k), lambda i, j: (i, 0)),
                  pl.BlockSpec((k, rb), lambda i, j: (0, j)),
                  pl.BlockSpec((ATT_HEADDIM, 1), lambda i, j: (0, 0)),
                  pl.BlockSpec((ATT_HEADDIM, 1), lambda i, j: (0, 0)),
                  pl.BlockSpec((ATT_HEADDIM, tm), lambda i, j: (0, i % tpb)),
                  pl.BlockSpec((ATT_HEADDIM, tm), lambda i, j: (0, i % tpb))],
        out_specs=pl.BlockSpec((rb, tm), lambda i, j: (j, i)),
        out_shape=jax.ShapeDtypeStruct((r, m), f32),
        compiler_params=_cparams("parallel", "parallel"),
        name="qkv_project_t",
    )(a, w, gq.reshape(-1, 1), gk.reshape(-1, 1), cos_t, sin_t)


def _attn_kernel(sink_ref, q_ref, kc_ref, k0_ref, k1_ref, k2_ref,
                 vc_ref, v0_ref, v1_ref, v2_ref, o_ref, ot_s, *, n_lat, n_ctx):
    qi = pl.program_id(1)
    kvh = pl.program_id(2)
    blk = ATT_BLOCK
    hd = ATT_HEADDIM
    kt = jnp.concatenate([kc_ref[...], k0_ref[...], k1_ref[...], k2_ref[...]], axis=1)
    vt = jnp.concatenate([vc_ref[...], v0_ref[...], v1_ref[...], v2_ref[...]], axis=1)
    kt = kt.astype(bf16)
    vt = vt.astype(bf16)
    R = ATT_Q_PER_KV
    t = lax.broadcasted_iota(i32, (blk, blk), 0)
    c = lax.broadcasted_iota(i32, (blk, blk), 1)
    m_prev = jnp.logical_and(t >= c, qi >= 1)
    m_next = jnp.logical_and(t <= c, (qi + 1) * blk < n_lat)
    q_all = jnp.concatenate([q_ref[r * hd:(r + 1) * hd, :].astype(bf16) for r in range(R)],
                            axis=1)
    s_all = lax.dot_general(kt, q_all, _TN, preferred_element_type=f32)
    p_list, dens = [], []
    for r in range(R):
        s = s_all[:, r * blk:(r + 1) * blk]
        parts = [s[:n_ctx],
                 jnp.where(m_prev, s[n_ctx:n_ctx + blk], -jnp.inf),
                 s[n_ctx + blk:n_ctx + 2 * blk],
                 jnp.where(m_next, s[n_ctx + 2 * blk:], -jnp.inf)]
        sink = sink_ref[kvh * R + r]
        mx = sink
        for x in parts:
            mx = jnp.maximum(jnp.max(x, axis=0, keepdims=True), mx)
        ps = [jnp.exp(x - mx) for x in parts]
        den = jnp.exp(sink - mx)
        for x in ps:
            den = den + jnp.sum(x, axis=0, keepdims=True)
        p_list.append(jnp.concatenate(ps, axis=0).astype(bf16))
        dens.append(den)
    p_all = jnp.concatenate(p_list, axis=1)
    o_all = jnp.dot(vt, p_all, preferred_element_type=f32)
    for r in range(R):
        ot_s[r * hd:(r + 1) * hd, :] = o_all[:, r * blk:(r + 1) * blk] / dens[r]
    o_ref[...] = ot_s[...].T.astype(o_ref.dtype)


def window_attention(qkv_t, sink, geom):
    bsz, L, n_ctx = geom
    n_lat = L - n_ctx
    blk = ATT_BLOCK
    hd = ATT_HEADDIM
    nqb = n_lat // blk
    cpb = L // blk
    cb0 = n_ctx // blk
    q_rows = ATT_Q_PER_KV * hd
    k_rb = (ATT_HEADS * hd) // hd
    v_rb = k_rb + ATT_KV_HEADS

    def band(rb0, off):
        def imap(b, qi, kvh, sink_ref):
            kb = jnp.clip(qi + off, 0, nqb - 1)
            return (rb0 + kvh, b * cpb + cb0 + kb)
        return pl.BlockSpec((hd, blk), imap)

    def ctx(rb0):
        return pl.BlockSpec((hd, n_ctx), lambda b, qi, kvh, s: (rb0 + kvh, b * (L // n_ctx)))

    kern = functools.partial(_attn_kernel, n_lat=n_lat, n_ctx=n_ctx)
    grid_spec = pltpu.PrefetchScalarGridSpec(
        num_scalar_prefetch=1,
        grid=(bsz, nqb, ATT_KV_HEADS),
        in_specs=[pl.BlockSpec((q_rows, blk), lambda b, qi, kvh, s: (kvh, b * cpb + cb0 + qi)),
                  ctx(k_rb), band(k_rb, -1), band(k_rb, 0), band(k_rb, 1),
                  ctx(v_rb), band(v_rb, -1), band(v_rb, 0), band(v_rb, 1)],
        out_specs=pl.BlockSpec((blk, q_rows), lambda b, qi, kvh, s: (b * nqb + qi, kvh)),
        scratch_shapes=[pltpu.VMEM((q_rows, blk), f32)],
    )
    return pl.pallas_call(
        kern,
        grid_spec=grid_spec,
        out_shape=jax.ShapeDtypeStruct((bsz * n_lat, ATT_HEADS * hd), bf16),
        compiler_params=_cparams("parallel", "parallel", "parallel"),
        name="window_attention",
    )(sink, *([qkv_t] * 9))


def _first_max(v, idx, n):
    m = jnp.max(v, axis=0, keepdims=True)
    first = jnp.min(jnp.where(v == m, idx, n), axis=0, keepdims=True)
    return m, first


def _route_kernel(h_ref, g_ref, mod_ref, modc_ref, rw_ref, rb_ref, wsg_ref, wsu_ref, wsd_ref,
                  m_ref, sh_ref, te_ref, wt_ref, pos_ref, cnt_ref, carry_s, *, n_ctx, tpb):
    i = pl.program_id(0)
    T = h_ref.shape[0]
    E, NG, GS = N_EXPERTS, N_EXPERT_GROUPS, GROUP_SIZE

    @pl.when(i == 0)
    def _():
        carry_s[...] = jnp.zeros_like(carry_s)

    x = h_ref[...]
    ms = jnp.mean(x * x, axis=-1, keepdims=True)
    y = x * lax.rsqrt(ms + EPS) * g_ref[...]
    scale = mod_ref[0, 4:5, :]
    shift = mod_ref[0, 3:4, :]
    if n_ctx:
        ctx_rows = jnp.clip(n_ctx - (i % tpb) * T, 0, T)
        is_ctx = lax.broadcasted_iota(i32, (T, 1), 0) < ctx_rows
        scale = jnp.where(is_ctx, modc_ref[0, 4:5, :], scale)
        shift = jnp.where(is_ctx, modc_ref[0, 3:4, :], shift)
    mt = y * (1.0 + scale) + shift
    m_ref[...] = _pack_bf16_pair(mt)
    mb = mt.astype(bf16)

    hg = jnp.dot(mb, wsg_ref[...], preferred_element_type=f32)
    hu = jnp.dot(mb, wsu_ref[...], preferred_element_type=f32)
    act = (jax.nn.silu(hg) * hu).astype(bf16)
    sh_ref[...] = jnp.dot(act, wsd_ref[...], preferred_element_type=f32)

    logits = lax.dot_general(rw_ref[...], mt, _NT, precision=HIGHEST,
                             preferred_element_type=f32)
    scores = jax.nn.sigmoid(logits)
    sel = scores + rb_ref[...]
    eidx = lax.broadcasted_iota(i32, (E, T), 0)
    midx = lax.broadcasted_iota(i32, (GS, T), 0)
    gidx = lax.broadcasted_iota(i32, (NG, T), 0)

    gs = jnp.zeros((NG, T), f32)
    for g in range(NG):
        blk = sel[g * GS:(g + 1) * GS, :]
        m1, f1 = _first_max(blk, midx, GS)
        m2 = jnp.max(jnp.where(midx == f1, -jnp.inf, blk), axis=0, keepdims=True)
        gs = jnp.where(gidx == g, m1 + m2, gs)
    gmask = jnp.zeros((NG, T), jnp.bool_)
    work = gs
    for _ in range(TOP_GROUPS):
        _, f = _first_max(work, gidx, NG)
        hit = gidx == f
        gmask = jnp.logical_or(gmask, hit)
        work = jnp.where(hit, -jnp.inf, work)
    cand = jnp.concatenate(
        [jnp.where(gmask[g:g + 1, :], sel[g * GS:(g + 1) * GS, :], -jnp.inf) for g in range(NG)],
        axis=0)
    hits = []
    chosen = jnp.zeros((E, T), jnp.bool_)
    work = cand
    for k in range(TOP_K):
        _, f = _first_max(work, eidx, E)
        hit = eidx == f
        hits.append(hit)
        chosen = jnp.logical_or(chosen, hit)
        work = jnp.where(hit, -jnp.inf, work)
        te_ref[k:k + 1, :] = f
    wts = [jnp.sum(jnp.where(h, scores, 0.0), axis=0, keepdims=True) for h in hits]
    tot = wts[0]
    for w in wts[1:]:
        tot = tot + w
    for k in range(TOP_K):
        wt_ref[k:k + 1, :] = wts[k] / (tot + 1e-20) * ROUTED_SCALE

    t0 = lax.broadcasted_iota(i32, (T, T), 0)
    t1 = lax.broadcasted_iota(i32, (T, T), 1)
    before = (t0 < t1).astype(bf16)
    chosen_f = chosen.astype(f32)
    cnt = jnp.dot(chosen_f.astype(bf16), before, preferred_element_type=f32) + carry_s[:, 0:1]
    for k in range(TOP_K):
        pos_ref[k:k + 1, :] = jnp.sum(jnp.where(hits[k], cnt, 0.0), axis=0,
                                      keepdims=True).astype(i32)
    new_carry = carry_s[...] + jnp.sum(chosen_f, axis=1, keepdims=True)
    carry_s[...] = new_carry
    cnt_ref[...] = new_carry.astype(i32)


def moe_route(h, g, mod3, rw_t, rbias, wsg, wsu, wsd, geom, lat_only):
    m, d = h.shape
    bsz, L, n_ctx = geom
    if lat_only:
        T = _pick_tile(L - n_ctx, (ROUTE_TILE_LAT, ROW_TILE))
        tpb = (L - n_ctx) // T
        kern = functools.partial(_route_kernel, n_ctx=0, tpb=tpb)
    else:
        T = _pick_tile(L, (ROUTE_TILE_JOINT, ROW_TILE))
        tpb = L // T
        kern = functools.partial(_route_kernel, n_ctx=n_ctx, tpb=tpb)
    E = N_EXPERTS
    hid = wsg.shape[1]
    const2 = lambda i: (0, 0)
    outs = pl.pallas_call(
        kern,
        grid=(m // T,),
        in_specs=[pl.BlockSpec((T, d), lambda i: (i, 0)),
                  pl.BlockSpec((1, d), const2),
                  pl.BlockSpec((1, 6, d), lambda i: (i // tpb, 0, 0)),
                  pl.BlockSpec((1, 6, d), lambda i: (bsz, 0, 0)),
                  pl.BlockSpec((E, d), const2),
                  pl.BlockSpec((E, 1), const2),
                  pl.BlockSpec((d, hid), const2),
                  pl.BlockSpec((d, hid), const2),
                  pl.BlockSpec((hid, d), const2)],
        out_specs=[pl.BlockSpec((T, d // 2), lambda i: (i, 0)),
                   pl.BlockSpec((T, d), lambda i: (i, 0)),
                   pl.BlockSpec((TOP_K, T), lambda i: (0, i)),
                   pl.BlockSpec((TOP_K, T), lambda i: (0, i)),
                   pl.BlockSpec((TOP_K, T), lambda i: (0, i)),
                   pl.BlockSpec((E, 128), const2)],
        out_shape=[jax.ShapeDtypeStruct((m, d // 2), jnp.uint32),
                   jax.ShapeDtypeStruct((m, d), f32),
                   jax.ShapeDtypeStruct((TOP_K, m), i32),
                   jax.ShapeDtypeStruct((TOP_K, m), f32),
                   jax.ShapeDtypeStruct((TOP_K, m), i32),
                   jax.ShapeDtypeStruct((E, 128), i32)],
        scratch_shapes=[pltpu.VMEM((E, 128), f32)],
        compiler_params=_cparams("arbitrary"),
        name="moe_route",
    )(h, g.reshape(1, d), mod3, mod3, rw_t, rbias.reshape(E, 1), wsg, wsu, wsd)
    return outs


GATHER_AHEAD = 2
GATHER_SLOTS = GATHER_AHEAD + 1


def _gmm_kernel(be_ref, nused_ref, tokc_ref, tokn_ref, tokf_ref, m_hbm, wg_ref, wu_ref,
                wd_ref, o_ref, xbuf, wg_s, wu_s, wd_s, sem):
    i = pl.program_id(0)
    nused = nused_ref[0]
    slot = i % GATHER_SLOTS
    far_slot = (i + GATHER_AHEAD) % GATHER_SLOTS

    def start_gather(tok_ref, dst_slot):
        for r in range(MOE_BLK):
            t = tok_ref[0, 0, r]
            pltpu.make_async_copy(m_hbm.at[pl.ds(t, 1), :],
                                  xbuf.at[dst_slot, pl.ds(r, 1), :],
                                  sem.at[dst_slot]).start()

    def wait_gather(dst_slot):
        pltpu.make_async_copy(m_hbm.at[pl.ds(0, MOE_BLK), :], xbuf.at[dst_slot],
                              sem.at[dst_slot]).wait()

    @pl.when(jnp.logical_and(i == 0, nused > 0))
    def _():
        start_gather(tokc_ref, 0)
        start_gather(tokn_ref, 1)

    prev = be_ref[jnp.maximum(i - 1, 0)]
    new_expert = jnp.logical_or(i == 0, be_ref[i] != prev)

    @pl.when(jnp.logical_and(i < nused, new_expert))
    def _():
        wg_s[...] = wg_ref[...].astype(bf16)
        wu_s[...] = wu_ref[...].astype(bf16)
        wd_s[...] = wd_ref[...].astype(bf16)

    @pl.when(i < nused)
    def _():
        wait_gather(slot)
        x_lo, x_hi = _unpack_bf16_pair(xbuf[slot])
        x_lo = x_lo.astype(bf16)
        x_hi = x_hi.astype(bf16)
        start_gather(tokf_ref, far_slot)
        kh = x_lo.shape[1]
        hg = (jnp.dot(x_lo, wg_s[:kh, :], preferred_element_type=f32)
              + jnp.dot(x_hi, wg_s[kh:, :], preferred_element_type=f32))
        hu = (jnp.dot(x_lo, wu_s[:kh, :], preferred_element_type=f32)
              + jnp.dot(x_hi, wu_s[kh:, :], preferred_element_type=f32))
        act = (jax.nn.silu(hg) * hu).astype(bf16)
        o_ref[...] = _pack_bf16_pair(jnp.dot(act, wd_s[...], preferred_element_type=f32))

    @pl.when(i >= nused)
    def _():
        o_ref[...] = jnp.zeros_like(o_ref)

    @pl.when(jnp.logical_and(jnp.logical_and(i >= nused, i < nused + GATHER_AHEAD), nused > 0))
    def _():
        wait_gather(slot)


def moe_experts(m_rows, slot_tok, block_expert, n_used, wg, wu, wd, layer):
    n_steps = block_expert.shape[0]
    d = wg.shape[2]
    dp = m_rows.shape[1]
    hid = wg.shape[3]
    tok3 = slot_tok.reshape(n_steps, 1, MOE_BLK)
    smem_blk = lambda imap: pl.BlockSpec((1, 1, MOE_BLK), imap, memory_space=pltpu.SMEM)
    grid_spec = pltpu.PrefetchScalarGridSpec(
        num_scalar_prefetch=2,
        grid=(n_steps,),
        in_specs=[smem_blk(lambda i, be, nu: (i, 0, 0)),
                  smem_blk(lambda i, be, nu: (jnp.minimum(i + 1, n_steps - 1), 0, 0)),
                  smem_blk(lambda i, be, nu: (jnp.minimum(i + GATHER_AHEAD, n_steps - 1), 0, 0)),
                  pl.BlockSpec(memory_space=pl.ANY),
                  pl.BlockSpec((None, None, d, hid), lambda i, be, nu: (layer, be[i], 0, 0)),
                  pl.BlockSpec((None, None, d, hid), lambda i, be, nu: (layer, be[i], 0, 0)),
                  pl.BlockSpec((None, None, hid, d), lambda i, be, nu: (layer, be[i], 0, 0))],
        out_specs=pl.BlockSpec((MOE_BLK, dp), lambda i, be, nu: (i, 0)),
        scratch_shapes=[pltpu.VMEM((GATHER_SLOTS, MOE_BLK, dp), jnp.uint32),
                        pltpu.VMEM((d, hid), bf16), pltpu.VMEM((d, hid), bf16),
                        pltpu.VMEM((hid, d), bf16),
                        pltpu.SemaphoreType.DMA((GATHER_SLOTS,))],
    )
    return pl.pallas_call(
        _gmm_kernel,
        grid_spec=grid_spec,
        out_shape=jax.ShapeDtypeStruct((n_steps * MOE_BLK, dp), jnp.uint32),
        compiler_params=_cparams("arbitrary"),
        name="moe_experts",
    )(block_expert, n_used, tok3, tok3, tok3, m_rows, wg, wu, wd)


def _comb_kernel(*refs, n_tiles, with_next):
    dc_ref, dn_ref, df_ref, ys_hbm, w_ref, sh_ref, h_ref, mod_ref = refs[:8]
    if with_next:
        gn_ref, modn_ref, o_ref, a_ref = refs[8:12]
        gbufs, sem = refs[12:12 + GATHER_SLOTS], refs[12 + GATHER_SLOTS]
    else:
        o_ref = refs[8]
        gbufs, sem = refs[9:9 + GATHER_SLOTS], refs[9 + GATHER_SLOTS]
    i = pl.program_id(0)
    T = COMB_TILE
    S = GATHER_SLOTS

    def start_gather(d_ref, p):
        for r in range(T):
            for k in range(TOP_K):
                t = d_ref[0, k, r]
                pltpu.make_async_copy(ys_hbm.at[pl.ds(t, 1), :],
                                      gbufs[p].at[k, pl.ds(r, 1), :],
                                      sem.at[p]).start()

    @pl.when(i == 0)
    def _():
        start_gather(dc_ref, 0)
        start_gather(dn_ref, 1)

    def step(p, issue_far):
        for k in range(TOP_K):
            pltpu.make_async_copy(ys_hbm.at[pl.ds(0, T), :], gbufs[p].at[k],
                                  sem.at[p]).wait()
        if issue_far:
            start_gather(df_ref, (p + GATHER_AHEAD) % S)
        w = w_ref[...]
        half = o_ref.shape[1] // 2
        acc_lo = jnp.zeros((T, half), f32)
        acc_hi = jnp.zeros((T, half), f32)
        for k in range(TOP_K):
            lo, hi = _unpack_bf16_pair(gbufs[p][k])
            acc_lo = acc_lo + lo * w[:, k:k + 1]
            acc_hi = acc_hi + hi * w[:, k:k + 1]
        gate = mod_ref[0, 5:6, :]
        o_lo = h_ref[:, :half] + gate[:, :half] * (acc_lo + sh_ref[:, :half])
        o_hi = h_ref[:, half:] + gate[:, half:] * (acc_hi + sh_ref[:, half:])
        o_ref[:, :half] = o_lo
        o_ref[:, half:] = o_hi
        if with_next:
            ssq = (jnp.sum(o_lo * o_lo, axis=-1, keepdims=True)
                   + jnp.sum(o_hi * o_hi, axis=-1, keepdims=True))
            inv = lax.rsqrt(ssq / (2 * half) + EPS)
            scale = 1.0 + modn_ref[0, 1:2, :]
            shift = modn_ref[0, 0:1, :]
            gn = gn_ref[...]
            a_ref[:, :half] = (o_lo * inv * gn[:, :half] * scale[:, :half]
                               + shift[:, :half]).astype(a_ref.dtype)
            a_ref[:, half:] = (o_hi * inv * gn[:, half:] * scale[:, half:]
                               + shift[:, half:]).astype(a_ref.dtype)

    first_tail = n_tiles - GATHER_AHEAD
    for p in range(S):
        @pl.when(jnp.logical_and(i % S == p, i < first_tail))
        def _(p=p):
            step(p, True)

    for tail in range(first_tail, n_tiles):
        @pl.when(i == tail)
        def _(tail=tail):
            step(tail % S, False)


def moe_combine(y_sorted, dest_t, wts_t, shared, h, mod3, geom, lat_only, next_norm=None):
    m, d = h.shape
    bsz, L, n_ctx = geom
    T = COMB_TILE
    n_tiles = m // T
    if lat_only:
        lpb = (L - n_ctx) // T
        mod_row = lambda i: i // lpb
    else:
        mod_row = _mod_row_map(geom, T)
    smem_blk = lambda imap: pl.BlockSpec((1, TOP_K, T), imap, memory_space=pltpu.SMEM)
    row_blk = pl.BlockSpec((T, d), lambda i: (i, 0))
    mod_blk = pl.BlockSpec((1, 6, d), lambda i: (mod_row(i), 0, 0))
    in_specs = [smem_blk(lambda i: (i, 0, 0)),
                smem_blk(lambda i: (jnp.minimum(i + 1, n_tiles - 1), 0, 0)),
                smem_blk(lambda i: (jnp.minimum(i + GATHER_AHEAD, n_tiles - 1), 0, 0)),
                pl.BlockSpec(memory_space=pl.ANY),
                pl.BlockSpec((T, TOP_K), lambda i: (i, 0)),
                row_blk, row_blk, mod_blk]
    args = [dest_t, dest_t, dest_t, y_sorted, wts_t, shared, h, mod3]
    out_specs, out_shape = row_blk, jax.ShapeDtypeStruct((m, d), f32)
    if next_norm is not None:
        in_specs += [pl.BlockSpec((1, d), lambda i: (0, 0)), mod_blk]
        args += [next_norm[0].reshape(1, d), next_norm[1]]
        out_specs = [row_blk, row_blk]
        out_shape = [out_shape, jax.ShapeDtypeStruct((m, d), bf16)]
    return pl.pallas_call(
        functools.partial(_comb_kernel, n_tiles=n_tiles, with_next=next_norm is not None),
        grid=(n_tiles,),
        in_specs=in_specs,
        out_specs=out_specs,
        out_shape=out_shape,
        scratch_shapes=[pltpu.VMEM((TOP_K, T, d // 2), jnp.uint32)] * GATHER_SLOTS
        + [pltpu.SemaphoreType.DMA((GATHER_SLOTS,))],
        compiler_params=_cparams("arbitrary"),
        name="moe_combine",
    )(*args)


TOK_SPLIT_LOG2 = 7


def _slot_kernel(dest_ref, hi_ref, lo_ref):
    i = pl.program_id(0)
    T = dest_ref.shape[1]
    nb = hi_ref.shape[0]

    @pl.when(i == 0)
    def _():
        hi_ref[...] = jnp.zeros_like(hi_ref)
        lo_ref[...] = jnp.zeros_like(lo_ref)

    d = dest_ref[...]
    tok = i * T + lax.broadcasted_iota(i32, (1, T), 1)
    tok_hi = (tok >> TOK_SPLIT_LOG2).astype(f32)
    tok_lo = (tok & ((1 << TOK_SPLIT_LOG2) - 1)).astype(f32)
    brow = lax.broadcasted_iota(i32, (nb, T), 0)
    orow = lax.broadcasted_iota(i32, (MOE_BLK, T), 0)
    a_hi, a_lo, b = [], [], []
    for k in range(TOP_K):
        dk = d[k:k + 1, :]
        hit = brow == (dk >> MOE_BLK_LOG2)
        a_hi.append(jnp.where(hit, tok_hi, 0.0).astype(bf16))
        a_lo.append(jnp.where(hit, tok_lo, 0.0).astype(bf16))
        b.append((orow == (dk & (MOE_BLK - 1))).astype(f32).astype(bf16))
    bm = jnp.concatenate(b, axis=1)
    hi_ref[...] += lax.dot_general(jnp.concatenate(a_hi, axis=1), bm, _NT,
                                   preferred_element_type=f32)
    lo_ref[...] += lax.dot_general(jnp.concatenate(a_lo, axis=1), bm, _NT,
                                   preferred_element_type=f32)


def slot_table(dest, n_steps):
    m = dest.shape[1]
    T = ROW_TILE
    nb = -(-n_steps // 8) * 8
    hi, lo = pl.pallas_call(
        _slot_kernel,
        grid=(m // T,),
        in_specs=[pl.BlockSpec((TOP_K, T), lambda i: (0, i))],
        out_specs=[pl.BlockSpec((nb, MOE_BLK), lambda i: (0, 0)),
                   pl.BlockSpec((nb, MOE_BLK), lambda i: (0, 0))],
        out_shape=[jax.ShapeDtypeStruct((nb, MOE_BLK), f32),
                   jax.ShapeDtypeStruct((nb, MOE_BLK), f32)],
        compiler_params=_cparams("arbitrary"),
        name="slot_table",
    )(dest)
    tab = hi.astype(i32) * (1 << TOK_SPLIT_LOG2) + lo.astype(i32)
    return tab[:n_steps]


def moe_layer(h, g2, mod3, layer, rw, rbias, wg, wu, wd, wsg, wsu, wsd, geom, lat_only,
              next_norm=None):
    m, d = h.shape
    E = N_EXPERTS
    mt, shared, top_e, wts, pos, cnt = moe_route(
        h, g2, mod3, rw.T, rbias, wsg.astype(bf16), wsu.astype(bf16), wsd.astype(bf16),
        geom, lat_only)
    counts = cnt[:, 0]
    padded = ((counts + MOE_BLK - 1) // MOE_BLK) * MOE_BLK
    pad_end = jnp.cumsum(padded)
    pad_start = pad_end - padded
    eids = jnp.arange(E, dtype=i32)[:, None, None]
    dest = pos + jnp.sum(jnp.where(top_e[None] == eids, pad_start[:, None, None], 0), axis=0)
    n_steps = (m * TOP_K) // MOE_BLK + E + GATHER_AHEAD
    slot_tok = slot_table(dest, n_steps)
    block_start = jnp.arange(n_steps, dtype=i32) * MOE_BLK
    block_expert = jnp.minimum(
        jnp.sum((pad_end[None, :] <= block_start[:, None]).astype(i32), axis=1), E - 1)
    n_used = (pad_end[-1:] // MOE_BLK).astype(i32)
    y_sorted = moe_experts(mt, slot_tok, block_expert, n_used, wg, wu, wd, layer)
    T = COMB_TILE
    dest_t = dest.reshape(TOP_K, m // T, T).transpose(1, 0, 2)
    return moe_combine(y_sorted, dest_t, wts.T, shared, h, mod3, geom, lat_only, next_norm)


def _rope_tables_t(n_ctx, n_lat):
    n_rows = n_lat // GRID_W
    rows = jnp.repeat(jnp.arange(n_rows, dtype=f32), GRID_W)
    cols = jnp.tile(jnp.arange(GRID_W, dtype=f32), n_rows)
    n_freq = ATT_HEADDIM // 4
    inv_freq = ROPE_BASE ** (-jnp.arange(n_freq, dtype=f32) / n_freq)
    ar = rows[None, :] * inv_freq[:, None]
    ac = cols[None, :] * inv_freq[:, None]
    cos_t = jnp.concatenate([jnp.cos(ar), jnp.cos(ar), jnp.cos(ac), jnp.cos(ac)], axis=0)
    sin_t = jnp.concatenate([-jnp.sin(ar), jnp.sin(ar), -jnp.sin(ac), jnp.sin(ac)], axis=0)
    cos_t = jnp.concatenate([jnp.ones((ATT_HEADDIM, n_ctx), f32), cos_t], axis=1)
    sin_t = jnp.concatenate([jnp.zeros((ATT_HEADDIM, n_ctx), f32), sin_t], axis=1)
    return cos_t, sin_t


def even_layer_mix(h, a, mod3, w_in, gm_ws, gm_bs, gm_ln_g, gm_ln_b, conv_w, conv_b,
                   a_log, dt_bias, d_skip, ssd_norm_g, w_out, geom):
    n_main = w_in.shape[1] - 2 * SSD_HEADS
    proj = matmul_nn(a, w_in.astype(bf16), f32, n=n_main)
    w_dt = w_in[:, n_main:].reshape(-1, 2, SSD_GROUPS, SSD_HPG).transpose(2, 1, 3, 0)
    dt_t = dt_project(a, w_dt.reshape(2 * SSD_HEADS, -1))
    bexp = jnp.repeat(gm_bs.T, A_CHUNK, axis=1)
    g_out = gmlp_mix(proj, gm_ws.astype(bf16), bexp, gm_ln_g, gm_ln_b)
    y_out = ssd_mix(proj, dt_t, conv_w, conv_b, a_log, dt_bias, d_skip, ssd_norm_g, geom)
    aw = gm_ws.shape[0] * gm_ws.shape[1]
    w_o = w_out.astype(bf16)
    return out_project_joint([g_out, y_out], [w_o[:aw], w_o[aw:]], h, mod3, geom)


def odd_layer_mix_lat(h, a, mod3, w_q, w_kv, q_norm_g, k_norm_g, sink, w_o, geom):
    bsz, L, n_ctx = geom
    w_t = jnp.concatenate([w_q, w_kv], axis=1).astype(bf16)
    cos_t, sin_t = _rope_tables_t(n_ctx, L - n_ctx)
    qkv_t = qkv_project_t(a, w_t, q_norm_g, k_norm_g, cos_t, sin_t, geom)
    att = window_attention(qkv_t, sink, geom)
    return out_project_lat(att, w_o.astype(bf16), h, mod3, geom)


def kernel(x, c, ctx, c_ctx, mod_w, mod_b, norm1_g, norm2_g, ev_w_in, ev_gm_ws, ev_gm_bs, ev_gm_ln_g, ev_gm_ln_b, ev_conv_w, ev_conv_b, ev_a_log, ev_dt_bias, ev_d_skip, ev_ssd_norm_g, ev_w_out, od_w_q, od_w_kv, od_q_norm_g, od_k_norm_g, od_sink, od_w_o, moe_router_w, moe_router_bias, moe_w_gate, moe_w_up, moe_w_down, moe_ws_gate, moe_ws_up, moe_ws_down):
    bsz, n_lat, d = x.shape
    n_ctx = ctx.shape[1]
    L = n_ctx + n_lat
    geom = (bsz, L, n_ctx)
    depth = mod_w.shape[0]
    assert depth == 2 and n_ctx == ROW_TILE and n_lat % ROW_TILE == 0 and bsz < 16

    c_all = jnp.zeros((16, d), f32).at[:bsz].set(c).at[bsz].set(c_ctx)
    mod3 = modulation(c_all, mod_w, mod_b, 0).reshape(16, 6, d)
    mod3_1 = modulation(c_all, mod_w, mod_b, 1).reshape(16, 6, d)

    h, a = join_norm_modulate(ctx.reshape(bsz * n_ctx, d), x.reshape(bsz * n_lat, d),
                              norm1_g[0], mod3, geom)
    h = even_layer_mix(h, a, mod3, ev_w_in[0], ev_gm_ws[0], ev_gm_bs[0], ev_gm_ln_g[0],
                       ev_gm_ln_b[0], ev_conv_w[0], ev_conv_b[0], ev_a_log[0], ev_dt_bias[0],
                       ev_d_skip[0], ev_ssd_norm_g[0], ev_w_out[0], geom)
    h, a = moe_layer(h, norm2_g[0], mod3, 0, moe_router_w[0], moe_router_bias[0], moe_w_gate,
                     moe_w_up, moe_w_down, moe_ws_gate[0], moe_ws_up[0], moe_ws_down[0],
                     geom, lat_only=False, next_norm=(norm1_g[1], mod3_1))

    h_lat = odd_layer_mix_lat(h, a, mod3_1, od_w_q[0], od_w_kv[0], od_q_norm_g[0],
                              od_k_norm_g[0], od_sink[0], od_w_o[0], geom)
    h_lat = moe_layer(h_lat, norm2_g[1], mod3_1, 1, moe_router_w[1], moe_router_bias[1],
                      moe_w_gate, moe_w_up, moe_w_down, moe_ws_gate[1], moe_ws_up[1],
                      moe_ws_down[1], geom, lat_only=True)
    return h_lat.reshape(bsz, n_lat, d)
```

```python
import functools

import jax
import jax.numpy as jnp
from jax import lax
from jax.experimental import pallas as pl
from jax.experimental.pallas import tpu as pltpu

f32 = jnp.float32
bf16 = jnp.bfloat16
i32 = jnp.int32
HIGHEST = lax.Precision.HIGHEST

D_MODEL = 2048
EPS = 1e-6
GRID_W = 64

A_CHUNK = 128
A_GROUPS = 16

SSD_HEADS = 32
SSD_HEADDIM = 64
SSD_GROUPS = 8
SSD_STATE = 128
SSD_CHUNK = 128
SSD_HPG = SSD_HEADS // SSD_GROUPS
SSD_GW = SSD_HPG * SSD_HEADDIM

ATT_HEADDIM = 64
ATT_HEADS = 32
ATT_KV_HEADS = 4
ATT_Q_PER_KV = 8
ATT_WINDOW = 128
ATT_BLOCK = 128
ATT_SCALE = ATT_HEADDIM ** -0.5
ROPE_BASE = 10000.0

N_EXPERTS = 64
N_EXPERT_GROUPS = 8
GROUP_SIZE = N_EXPERTS // N_EXPERT_GROUPS
TOP_GROUPS = 4
TOP_K = 8
EXPERT_HIDDEN = 512
ROUTED_SCALE = 2.5

ROW_TILE = 256
BIG_TILE = 768
ROUTE_TILE_JOINT = 384
ROUTE_TILE_LAT = 512
MOE_BLK = 512
MOE_BLK_LOG2 = 9
COMB_TILE = 64
VMEM_LIMIT = 56 * 1024 * 1024

_NT = (((1,), (1,)), ((), ()))
_TN = (((0,), (0,)), ((), ()))


def _cparams(*sem):
    return pltpu.CompilerParams(dimension_semantics=sem, vmem_limit_bytes=VMEM_LIMIT)


def _pack_bf16_pair(x):
    c = x.shape[1] // 2
    lo = lax.bitcast_convert_type(x[:, :c].astype(bf16).astype(f32), jnp.uint32)
    hi = lax.bitcast_convert_type(x[:, c:].astype(bf16).astype(f32), jnp.uint32)
    return (lo >> 16) | hi


def _unpack_bf16_pair(p):
    lo = lax.bitcast_convert_type(p << 16, f32)
    hi = lax.bitcast_convert_type(p & jnp.uint32(0xFFFF0000), f32)
    return lo, hi


def _pick_tile(m, options):
    for t in options:
        if m % t == 0:
            return t
    raise ValueError(f"no tile for {m}")


def _mod_kernel(c_ref, w_ref, b_ref, o_ref):
    sc = jax.nn.silu(c_ref[...])
    o_ref[...] = jnp.dot(sc, w_ref[...], precision=HIGHEST,
                         preferred_element_type=f32) + b_ref[...]


def modulation(c_all, mod_w, mod_b, layer):
    rows, d = c_all.shape
    n = mod_w.shape[2]
    tn = 1024
    return pl.pallas_call(
        _mod_kernel,
        grid=(n // tn,),
        in_specs=[pl.BlockSpec((rows, d), lambda j: (0, 0)),
                  pl.BlockSpec((None, d, tn), lambda j: (layer, 0, j)),
                  pl.BlockSpec((None, 1, tn), lambda j: (layer, 0, j))],
        out_specs=pl.BlockSpec((rows, tn), lambda j: (0, j)),
        out_shape=jax.ShapeDtypeStruct((rows, n), f32),
        compiler_params=_cparams("arbitrary"),
        name="modulation",
    )(c_all, mod_w, mod_b.reshape(mod_b.shape[0], 1, n))


def _mod_row_map(geom, tile):
    bsz, L, n_ctx = geom
    tpb = L // tile
    nct = n_ctx // tile

    def mod_row(i):
        return jnp.where((i % tpb) < nct, bsz, i // tpb)
    return mod_row


def _lat_tile_map(geom, tile):
    bsz, L, n_ctx = geom
    lpb = (L - n_ctx) // tile
    tpb = L // tile
    nct = n_ctx // tile

    def joint(t):
        return (t // lpb) * tpb + nct + (t % lpb)
    return joint


def _norm_mod(x, g, mod_ref, shift_idx, scale_idx):
    ms = jnp.mean(x * x, axis=-1, keepdims=True)
    y = x * lax.rsqrt(ms + EPS) * g
    return y * (1.0 + mod_ref[0, scale_idx:scale_idx + 1, :]) + mod_ref[0, shift_idx:shift_idx + 1, :]


def _join_norm_kernel(ctx_ref, x_ref, g_ref, mod_ref, h_ref, a_ref, *, tpb, nct):
    is_ctx = (pl.program_id(0) % tpb) < nct

    @pl.when(is_ctx)
    def _():
        h_ref[...] = ctx_ref[...]

    @pl.when(jnp.logical_not(is_ctx))
    def _():
        h_ref[...] = x_ref[...]

    a_ref[...] = _norm_mod(h_ref[...], g_ref[...], mod_ref, 0, 1).astype(a_ref.dtype)


def join_norm_modulate(ctx2, x2, g, mod3, geom):
    bsz, L, n_ctx = geom
    d = x2.shape[1]
    T = ROW_TILE
    tpb, nct = L // T, n_ctx // T
    lpb = tpb - nct
    mod_row = _mod_row_map(geom, T)
    return pl.pallas_call(
        functools.partial(_join_norm_kernel, tpb=tpb, nct=nct),
        grid=(bsz * tpb,),
        in_specs=[pl.BlockSpec((T, d), lambda i: ((i // tpb) * nct + jnp.minimum(i % tpb, nct - 1), 0)),
                  pl.BlockSpec((T, d), lambda i: ((i // tpb) * lpb + jnp.maximum(i % tpb - nct, 0), 0)),
                  pl.BlockSpec((1, d), lambda i: (0, 0)),
                  pl.BlockSpec((1, 6, d), lambda i: (mod_row(i), 0, 0))],
        out_specs=[pl.BlockSpec((T, d), lambda i: (i, 0)),
                   pl.BlockSpec((T, d), lambda i: (i, 0))],
        out_shape=[jax.ShapeDtypeStruct((bsz * L, d), f32),
                   jax.ShapeDtypeStruct((bsz * L, d), bf16)],
        compiler_params=_cparams("parallel"),
        name="join_norm_modulate",
    )(ctx2, x2, g.reshape(1, d), mod3)


def _mm_kernel(a_ref, w_ref, o_ref):
    o_ref[...] = jnp.dot(a_ref[...], w_ref[...],
                         preferred_element_type=f32).astype(o_ref.dtype)


def matmul_nn(a, w, out_dtype, n=None, tn=1024):
    m, k = a.shape
    n = w.shape[1] if n is None else n
    assert n % tn == 0
    tm = _pick_tile(m, (1024, 768, 512, 256))
    return pl.pallas_call(
        _mm_kernel,
        grid=(n // tn, m // tm),
        in_specs=[pl.BlockSpec((tm, k), lambda j, i: (i, 0)),
                  pl.BlockSpec((k, tn), lambda j, i: (0, j))],
        out_specs=pl.BlockSpec((tm, tn), lambda j, i: (i, j)),
        out_shape=jax.ShapeDtypeStruct((m, n), out_dtype),
        compiler_params=_cparams("parallel", "parallel"),
        name="matmul_nn",
    )(a, w)


def _dt_kernel(a_ref, w_ref, o_ref):
    a = a_ref[...]
    acc = lax.dot_general(w_ref[0], a, _NT, preferred_element_type=f32)
    for p in range(1, w_ref.shape[0]):
        acc = acc + lax.dot_general(w_ref[p], a, _NT, preferred_element_type=f32)
    o_ref[...] = acc


def dt_project(a, w_dt_t):
    m, k = a.shape
    r = w_dt_t.shape[0]
    hi = w_dt_t.astype(bf16)
    r1 = w_dt_t - hi.astype(f32)
    mid = r1.astype(bf16)
    lo = (r1 - mid.astype(f32)).astype(bf16)
    w3 = jnp.stack([hi, mid, lo])
    tm = _pick_tile(m, (1024, 768, 512, 256))
    return pl.pallas_call(
        _dt_kernel,
        grid=(m // tm,),
        in_specs=[pl.BlockSpec((tm, k), lambda i: (i, 0)),
                  pl.BlockSpec((3, r, k), lambda i: (0, 0, 0))],
        out_specs=pl.BlockSpec((r, tm), lambda i: (0, i)),
        out_shape=jax.ShapeDtypeStruct((r, m), f32),
        compiler_params=_cparams("parallel"),
        name="dt_project",
    )(a, w3)


def _gelu_exact(x):
    return 0.5 * x * (1.0 + lax.erf(x * (2.0 ** -0.5)))


def _gmlp_kernel(u_ref, v_ref, ws_ref, bexp_ref, lng_ref, lnb_ref, o_ref):
    u = _gelu_exact(u_ref[...])
    v = _gelu_exact(v_ref[...])
    mu = jnp.mean(v, axis=-1, keepdims=True)
    vc = v - mu
    var = jnp.mean(vc * vc, axis=-1, keepdims=True)
    vn = (vc * lax.rsqrt(var + EPS) * lng_ref[...] + lnb_ref[...]).astype(bf16)
    for g in range(A_GROUPS):
        sl = slice(g * A_CHUNK, (g + 1) * A_CHUNK)
        mixed = jnp.dot(ws_ref[g], vn[:, sl], preferred_element_type=f32) + bexp_ref[:, sl]
        o_ref[:, sl] = (u[:, sl] * mixed).astype(o_ref.dtype)


def gmlp_mix(proj, ws_bf, bexp, ln_g, ln_b):
    m = proj.shape[0]
    w = A_GROUPS * A_CHUNK
    return pl.pallas_call(
        _gmlp_kernel,
        grid=(m // A_CHUNK,),
        in_specs=[pl.BlockSpec((A_CHUNK, w), lambda c: (c, 0)),
                  pl.BlockSpec((A_CHUNK, w), lambda c: (c, 1)),
                  pl.BlockSpec((A_GROUPS, A_CHUNK, A_CHUNK), lambda c: (0, 0, 0)),
                  pl.BlockSpec((A_CHUNK, w), lambda c: (0, 0)),
                  pl.BlockSpec((1, w), lambda c: (0, 0)),
                  pl.BlockSpec((1, w), lambda c: (0, 0))],
        out_specs=pl.BlockSpec((A_CHUNK, w), lambda c: (c, 0)),
        out_shape=jax.ShapeDtypeStruct((m, w), bf16),
        compiler_params=_cparams("parallel"),
        name="gmlp_mix",
    )(proj, proj, ws_bf, bexp, ln_g.reshape(1, w), ln_b.reshape(1, w))


def _head_expand(v, lane):
    out = v[:, 3:4]
    for r in (2, 1, 0):
        out = jnp.where(lane < SSD_HEADDIM * (r + 1), v[:, r:r + 1], out)
    return out


def _row_expand(v, row):
    out = v[:, 3:4]
    for r in (2, 1, 0):
        out = jnp.where(row < SSD_HEADDIM * (r + 1), v[:, r:r + 1], out)
    return out


def _ssd_kernel(x_ref, b_ref, c_ref, z_ref, dtr_ref, dtc_ref,
                cwx_ref, cwb_ref, cwc_ref, cbx_ref, cbb_ref, cbc_ref,
                alr_ref, alc_ref, dbr_ref, dbc_ref, dsk_ref, ng_ref,
                o_ref,
                xs_s, bs_s, cs_s, dtr_s, dtar_s, dtc_s, dtac_s, suf_s, yacc_s, sf_s, sb_s,
                *, n_ctx, L):
    Q = SSD_CHUNK
    nc = L // Q
    ncc = n_ctx // Q
    H = SSD_HPG

    bias_r = dbr_ref[...]
    a_r = -jnp.exp(alr_ref[...])
    for c in range(nc):
        dt = jax.nn.softplus(dtr_ref[:, c * Q:(c + 1) * Q] + bias_r)
        dtr_s[c] = dt
        dtar_s[c] = dt * a_r
    dtc = jax.nn.softplus(dtc_ref[0] + dbc_ref[0])
    dtc_s[...] = dtc
    dtac_s[...] = dtc * (-jnp.exp(alc_ref[0]))

    ii = lax.broadcasted_iota(i32, (Q, Q), 0)
    jj = lax.broadcasted_iota(i32, (Q, Q), 1)
    lower = ii >= jj
    upper = ii <= jj
    tril = lower.astype(f32)
    triu = upper.astype(f32)
    lane = lax.broadcasted_iota(i32, (Q, SSD_GW), 1)
    row = lax.broadcasted_iota(i32, (SSD_GW, SSD_STATE), 0)

    def conv_act(ref, w_ref, bias_ref, s):
        xv = ref[pl.ds(s, Q), :]
        w = xv.shape[1]
        r0 = lax.broadcasted_iota(i32, (Q, w), 0)
        prev = ref[pl.ds(jnp.maximum(s - 1, 0), 1), :]
        nxt = ref[pl.ds(jnp.minimum(s + Q, L - 1), 1), :]
        has_prev = jnp.logical_and(s != 0, s != n_ctx)
        has_next = jnp.logical_and(s + Q != n_ctx, s + Q != L)
        prev = jnp.where(has_prev, prev, 0.0)
        nxt = jnp.where(has_next, nxt, 0.0)
        xm1 = jnp.where(r0 == 0, prev, pltpu.roll(xv, 1, 0))
        xp1 = jnp.where(r0 == Q - 1, nxt, pltpu.roll(xv, Q - 1, 0))
        y = w_ref[0:1, :] * xm1 + w_ref[1:2, :] * xv + w_ref[2:3, :] * xp1 + bias_ref[...]
        return jax.nn.silu(y)

    sf_s[...] = jnp.zeros_like(sf_s)
    sb_s[...] = jnp.zeros_like(sb_s)

    def fwd_body(c, carry):
        s = pl.multiple_of(c * Q, Q)
        X = conv_act(x_ref, cwx_ref, cbx_ref, s)
        Bc = conv_act(b_ref, cwb_ref, cbb_ref, s)
        Cc = conv_act(c_ref, cwc_ref, cbc_ref, s)
        xs_s[pl.ds(s, Q), :] = X
        bs_s[pl.ds(s, Q), :] = Bc
        cs_s[pl.ds(s, Q), :] = Cc
        Bb = Bc.astype(bf16)
        Cb = Cc.astype(bf16)
        cb = lax.dot_general(Cb, Bb, _NT, preferred_element_type=f32)
        dta_c = dtac_s[pl.ds(s, Q), :]
        dta_r = dtar_s[c]
        dt_r = dtr_s[c]
        p_col = jnp.dot(tril, dta_c, precision=HIGHEST, preferred_element_type=f32)
        s_col = jnp.dot(triu, dta_c, precision=HIGHEST, preferred_element_type=f32)
        p_row = jnp.dot(dta_r, triu, precision=HIGHEST, preferred_element_type=f32)
        s_row = jnp.dot(dta_r, tril, precision=HIGHEST, preferred_element_type=f32)
        suf_s[pl.ds(s, Q), :] = s_col
        y = jnp.zeros((Q, SSD_GW), f32)
        for r in range(H):
            lf = jnp.exp(jnp.where(lower, p_col[:, r:r + 1] - p_row[r:r + 1, :], -jnp.inf))
            lb = jnp.exp(jnp.where(upper, s_col[:, H + r:H + r + 1] - s_row[H + r:H + r + 1, :],
                                   -jnp.inf))
            mr = cb * (lf * dt_r[r:r + 1, :] + lb * dt_r[H + r:H + r + 1, :])
            head = jnp.logical_and(lane >= SSD_HEADDIM * r, lane < SSD_HEADDIM * (r + 1))
            xm = jnp.where(head, X, 0.0).astype(bf16)
            y = y + jnp.dot(mr.astype(bf16), xm, preferred_element_type=f32)
        sf = sf_s[...]
        yi = lax.dot_general(Cb, sf.astype(bf16), _NT, preferred_element_type=f32)
        pf = p_col[:, 0:H]
        y = y + yi * _head_expand(jnp.exp(pf), lane)
        yacc_s[pl.ds(s, Q), :] = y
        tot = p_col[Q - 1:Q, 0:H]
        wf = dtc_s[pl.ds(s, Q), 0:H] * jnp.exp(tot - pf)
        xw = (X * _head_expand(wf, lane)).astype(bf16)
        upd = lax.dot_general(xw, Bb, _TN, preferred_element_type=f32)
        sf_s[...] = sf * _row_expand(jnp.exp(tot), row) + upd
        return carry

    lax.fori_loop(0, nc, fwd_body, 0, unroll=3)

    def bwd_body(k, carry):
        c = jnp.where(k < ncc, ncc - 1 - k, nc - 1 - (k - ncc))
        s = pl.multiple_of(c * Q, Q)
        X = xs_s[pl.ds(s, Q), :]
        Bb = bs_s[pl.ds(s, Q), :].astype(bf16)
        Cb = cs_s[pl.ds(s, Q), :].astype(bf16)
        ab = suf_s[pl.ds(s, Q), H:2 * H]
        tot = ab[0:1, :]
        sb = sb_s[...]
        yi = lax.dot_general(Cb, sb.astype(bf16), _NT, preferred_element_type=f32)
        y = yacc_s[pl.ds(s, Q), :] + yi * _head_expand(jnp.exp(ab), lane) + dsk_ref[...] * X
        wb = dtc_s[pl.ds(s, Q), H:2 * H] * jnp.exp(tot - ab)
        xw = (X * _head_expand(wb, lane)).astype(bf16)
        upd = lax.dot_general(xw, Bb, _TN, preferred_element_type=f32)
        sb_s[...] = sb * _row_expand(jnp.exp(tot), row) + upd
        y = y * jax.nn.silu(z_ref[pl.ds(s, Q), :])
        ms = jnp.mean(y * y, axis=-1, keepdims=True)
        o_ref[pl.ds(s, Q), :] = (y * lax.rsqrt(ms + EPS) * ng_ref[...]).astype(o_ref.dtype)
        return carry

    lax.fori_loop(0, nc, bwd_body, 0, unroll=2)


def ssd_mix(proj, dt_t, conv_w, conv_b, a_log, dt_bias, d_skip, norm_g, geom):
    bsz, L, n_ctx = geom
    m = proj.shape[0]
    G, H, GW, N = SSD_GROUPS, SSD_HPG, SSD_GW, SSD_STATE
    nc = L // SSD_CHUNK
    col0 = proj.shape[1] - (SSD_HEADS * SSD_HEADDIM + 2 * G * N)
    zc0 = col0 - SSD_HEADS * SSD_HEADDIM
    xb, zb = col0 // GW, zc0 // GW
    bb = (col0 + SSD_HEADS * SSD_HEADDIM) // N
    cb = bb + G
    cxb, cbb, ccb = 0, (SSD_HEADS * SSD_HEADDIM) // N, (SSD_HEADS * SSD_HEADDIM) // N + G

    def to_gdr(p):
        return p.reshape(2, G, H).transpose(1, 0, 2).reshape(G, 2 * H)

    al, db = to_gdr(a_log), to_gdr(dt_bias)
    dt_col = dt_t.reshape(G, 2 * H, m).transpose(0, 2, 1)
    dsk = jnp.repeat(d_skip, SSD_HEADDIM).reshape(1, -1)
    conv_b2 = conv_b.reshape(1, -1)
    kern = functools.partial(_ssd_kernel, n_ctx=n_ctx, L=L)
    return pl.pallas_call(
        kern,
        grid=(bsz, G),
        in_specs=[
            pl.BlockSpec((L, GW), lambda b, g: (b, xb + g)),
            pl.BlockSpec((L, N), lambda b, g: (b, bb + g)),
            pl.BlockSpec((L, N), lambda b, g: (b, cb + g)),
            pl.BlockSpec((L, GW), lambda b, g: (b, zb + g)),
            pl.BlockSpec((2 * H, L), lambda b, g: (g, b)),
            pl.BlockSpec((1, L, 2 * H), lambda b, g: (g, b, 0)),
            pl.BlockSpec((3, GW), lambda b, g: (0, cxb + g)),
            pl.BlockSpec((3, N), lambda b, g: (0, cbb + g)),
            pl.BlockSpec((3, N), lambda b, g: (0, ccb + g)),
            pl.BlockSpec((1, GW), lambda b, g: (0, cxb + g)),
            pl.BlockSpec((1, N), lambda b, g: (0, cbb + g)),
            pl.BlockSpec((1, N), lambda b, g: (0, ccb + g)),
            pl.BlockSpec((2 * H, 1), lambda b, g: (g, 0)),
            pl.BlockSpec((1, 1, 2 * H), lambda b, g: (g, 0, 0)),
            pl.BlockSpec((2 * H, 1), lambda b, g: (g, 0)),
            pl.BlockSpec((1, 1, 2 * H), lambda b, g: (g, 0, 0)),
            pl.BlockSpec((1, GW), lambda b, g: (0, g)),
            pl.BlockSpec((1, GW), lambda b, g: (0, g)),
        ],
        out_specs=pl.BlockSpec((L, GW), lambda b, g: (b, g)),
        out_shape=jax.ShapeDtypeStruct((m, G * GW), bf16),
        scratch_shapes=[
            pltpu.VMEM((L, GW), f32), pltpu.VMEM((L, N), f32), pltpu.VMEM((L, N), f32),
            pltpu.VMEM((nc, 2 * H, SSD_CHUNK), f32), pltpu.VMEM((nc, 2 * H, SSD_CHUNK), f32),
            pltpu.VMEM((L, 2 * H), f32), pltpu.VMEM((L, 2 * H), f32), pltpu.VMEM((L, 2 * H), f32),
            pltpu.VMEM((L, GW), f32),
            pltpu.VMEM((GW, N), f32), pltpu.VMEM((GW, N), f32),
        ],
        compiler_params=_cparams("parallel", "parallel"),
        name="ssd_mix",
    )(proj, proj, proj, proj, dt_t, dt_col,
      conv_w, conv_w, conv_w, conv_b2, conv_b2, conv_b2,
      al.reshape(G * 2 * H, 1), al.reshape(G, 1, 2 * H),
      db.reshape(G * 2 * H, 1), db.reshape(G, 1, 2 * H),
      dsk, norm_g.reshape(1, -1))


def _outproj_kernel(*refs, n_in, n_ctx, tpb, tm):
    a_refs = refs[:n_in]
    w_refs = refs[n_in:2 * n_in]
    h_ref, ml_ref, mc_ref, o_ref = refs[2 * n_in:]
    i = pl.program_id(1)
    acc = jnp.dot(a_refs[0][...], w_refs[0][...], preferred_element_type=f32)
    for k in range(1, n_in):
        acc = acc + jnp.dot(a_refs[k][...], w_refs[k][...], preferred_element_type=f32)
    n_ctx_rows = jnp.clip(n_ctx - (i % tpb) * tm, 0, tm)
    row = lax.broadcasted_iota(i32, acc.shape, 0)
    gate = jnp.where(row < n_ctx_rows, mc_ref[0, 2:3, :], ml_ref[0, 2:3, :])
    o_ref[...] = h_ref[...] + gate * acc


def out_project_joint(a_list, w_list, h, mod3, geom, tn=1024):
    bsz, L, n_ctx = geom
    m, d = h.shape
    tm = _pick_tile(L, (BIG_TILE, 512, 256))
    tpb = L // tm
    n_in = len(a_list)
    k = a_list[0].shape[1]
    kern = functools.partial(_outproj_kernel, n_in=n_in, n_ctx=n_ctx, tpb=tpb, tm=tm)
    in_specs = ([pl.BlockSpec((tm, k), lambda j, i: (i, 0)) for _ in range(n_in)]
                + [pl.BlockSpec((k, tn), lambda j, i: (0, j)) for _ in range(n_in)]
                + [pl.BlockSpec((tm, tn), lambda j, i: (i, j)),
                   pl.BlockSpec((1, 6, tn), lambda j, i: (i // tpb, 0, j)),
                   pl.BlockSpec((1, 6, tn), lambda j, i: (bsz, 0, j))])
    return pl.pallas_call(
        kern,
        grid=(d // tn, m // tm),
        in_specs=in_specs,
        out_specs=pl.BlockSpec((tm, tn), lambda j, i: (i, j)),
        out_shape=jax.ShapeDtypeStruct((m, d), f32),
        compiler_params=_cparams("parallel", "parallel"),
        name="out_project_joint",
    )(*a_list, *w_list, h, mod3, mod3)


def _outproj_lat_kernel(a_ref, w_ref, h_ref, ml_ref, o_ref):
    acc = jnp.dot(a_ref[...], w_ref[...], preferred_element_type=f32)
    o_ref[...] = h_ref[...] + ml_ref[0, 2:3, :] * acc


def out_project_lat(a, w, h_joint, mod3, geom, tn=1024):
    bsz, L, n_ctx = geom
    m, k = a.shape
    d = h_joint.shape[1]
    tm = ROW_TILE
    joint = _lat_tile_map(geom, tm)
    lpb = (L - n_ctx) // tm
    return pl.pallas_call(
        _outproj_lat_kernel,
        grid=(d // tn, m // tm),
        in_specs=[pl.BlockSpec((tm, k), lambda j, i: (i, 0)),
                  pl.BlockSpec((k, tn), lambda j, i: (0, j)),
                  pl.BlockSpec((tm, tn), lambda j, i: (joint(i), j)),
                  pl.BlockSpec((1, 6, tn), lambda j, i: (i // lpb, 0, j))],
        out_specs=pl.BlockSpec((tm, tn), lambda j, i: (i, j)),
        out_shape=jax.ShapeDtypeStruct((m, d), f32),
        compiler_params=_cparams("parallel", "parallel"),
        name="out_project_lat",
    )(a, w, h_joint, mod3)


QKV_ROW_BLOCK = 512


def _qkv_kernel(a_ref, w_ref, gq_ref, gk_ref, cos_ref, sin_ref, o_ref, *, n_qblk):
    j = pl.program_id(1)
    acc = jnp.dot(a_ref[...], w_ref[...], preferred_element_type=f32).T
    hd = ATT_HEADDIM
    q4 = hd // 4
    cs = cos_ref[...]
    sn = sin_ref[...]

    def norm_rope(hh, g):
        blk = acc[hh * hd:(hh + 1) * hd, :]
        ms = jnp.mean(blk * blk, axis=0, keepdims=True)
        xn = blk * lax.rsqrt(ms + EPS) * g
        sw = jnp.concatenate([xn[q4:2 * q4], xn[0:q4], xn[3 * q4:4 * q4], xn[2 * q4:3 * q4]],
                             axis=0)
        o_ref[hh * hd:(hh + 1) * hd, :] = xn * cs + sw * sn

    @pl.when(j < n_qblk)
    def _():
        g = gq_ref[...] * ATT_SCALE
        for hh in range(acc.shape[0] // hd):
            norm_rope(hh, g)

    @pl.when(j == n_qblk)
    def _():
        g = gk_ref[...]
        for hh in range(ATT_KV_HEADS):
            norm_rope(hh, g)
        o_ref[ATT_KV_HEADS * hd:, :] = acc[ATT_KV_HEADS * hd:, :]


def qkv_project_t(a, w, gq, gk, cos_t, sin_t, geom):
    bsz, L, n_ctx = geom
    m, k = a.shape
    r = w.shape[1]
    rb = QKV_ROW_BLOCK
    assert 2 * ATT_KV_HEADS * ATT_HEADDIM == rb and r % rb == 0
    tm = _pick_tile(L, (BIG_TILE, 512, 256))
    tpb = L // tm
    n_qblk = (ATT_HEADS * ATT_HEADDIM) // rb
    return pl.pallas_call(
        functools.partial(_qkv_kernel, n_qblk=n_qblk),
        grid=(m // tm, r // rb),
        in_specs=[pl.BlockSpec((tm, k), lambda i, j: (i, 0)),
                  pl.BlockSpec((k, rb), lambda i, j: (0, j)),
                  pl.BlockSpec((ATT_HEADDIM, 1), lambda i, j: (0, 0)),
                  pl.BlockSpec((ATT_HEADDIM, 1), lambda i, j: (0, 0)),
                  pl.BlockSpec((ATT_HEADDIM, tm), lambda i, j: (0, i % tpb)),
                  pl.BlockSpec((ATT_HEADDIM, tm), lambda i, j: (0, i % tpb))],
        out_specs=pl.BlockSpec((rb, tm), lambda i, j: (j, i)),
        out_shape=jax.ShapeDtypeStruct((r, m), f32),
        compiler_params=_cparams("parallel", "parallel"),
        name="qkv_project_t",
    )(a, w, gq.reshape(-1, 1), gk.reshape(-1, 1), cos_t, sin_t)


def _attn_kernel(sink_ref, q_ref, kc_ref, k0_ref, k1_ref, k2_ref,
                 vc_ref, v0_ref, v1_ref, v2_ref, o_ref, ot_s, *, n_lat, n_ctx):
    qi = pl.program_id(1)
    kvh = pl.program_id(2)
    blk = ATT_BLOCK
    hd = ATT_HEADDIM
    kt = jnp.concatenate([kc_ref[...], k0_ref[...], k1_ref[...], k2_ref[...]], axis=1)
    vt = jnp.concatenate([vc_ref[...], v0_ref[...], v1_ref[...], v2_ref[...]], axis=1)
    kt = kt.astype(bf16)
    vt = vt.astype(bf16)
    R = ATT_Q_PER_KV
    t = lax.broadcasted_iota(i32, (blk, blk), 0)
    c = lax.broadcasted_iota(i32, (blk, blk), 1)
    m_prev = jnp.logical_and(t >= c, qi >= 1)
    m_next = jnp.logical_and(t <= c, (qi + 1) * blk < n_lat)
    q_all = jnp.concatenate([q_ref[r * hd:(r + 1) * hd, :].astype(bf16) for r in range(R)],
                            axis=1)
    s_all = lax.dot_general(kt, q_all, _TN, preferred_element_type=f32)
    p_list, dens = [], []
    for r in range(R):
        s = s_all[:, r * blk:(r + 1) * blk]
        parts = [s[:n_ctx],
                 jnp.where(m_prev, s[n_ctx:n_ctx + blk], -jnp.inf),
                 s[n_ctx + blk:n_ctx + 2 * blk],
                 jnp.where(m_next, s[n_ctx + 2 * blk:], -jnp.inf)]
        sink = sink_ref[kvh * R + r]
        mx = sink
        for x in parts:
            mx = jnp.maximum(jnp.max(x, axis=0, keepdims=True), mx)
        ps = [jnp.exp(x - mx) for x in parts]
        den = jnp.exp(sink - mx)
        for x in ps:
            den = den + jnp.sum(x, axis=0, keepdims=True)
        p_list.append(jnp.concatenate(ps, axis=0).astype(bf16))
        dens.append(den)
    p_all = jnp.concatenate(p_list, axis=1)
    o_all = jnp.dot(vt, p_all, preferred_element_type=f32)
    for r in range(R):
        ot_s[r * hd:(r + 1) * hd, :] = o_all[:, r * blk:(r + 1) * blk] / dens[r]
    o_ref[...] = ot_s[...].T.astype(o_ref.dtype)


def window_attention(qkv_t, sink, geom):
    bsz, L, n_ctx = geom
    n_lat = L - n_ctx
    blk = ATT_BLOCK
    hd = ATT_HEADDIM
    nqb = n_lat // blk
    cpb = L // blk
    cb0 = n_ctx // blk
    q_rows = ATT_Q_PER_KV * hd
    k_rb = (ATT_HEADS * hd) // hd
    v_rb = k_rb + ATT_KV_HEADS

    def band(rb0, off):
        def imap(b, qi, kvh, sink_ref):
            kb = jnp.clip(qi + off, 0, nqb - 1)
            return (rb0 + kvh, b * cpb + cb0 + kb)
        return pl.BlockSpec((hd, blk), imap)

    def ctx(rb0):
        return pl.BlockSpec((hd, n_ctx), lambda b, qi, kvh, s: (rb0 + kvh, b * (L // n_ctx)))

    kern = functools.partial(_attn_kernel, n_lat=n_lat, n_ctx=n_ctx)
    grid_spec = pltpu.PrefetchScalarGridSpec(
        num_scalar_prefetch=1,
        grid=(bsz, nqb, ATT_KV_HEADS),
        in_specs=[pl.BlockSpec((q_rows, blk), lambda b, qi, kvh, s: (kvh, b * cpb + cb0 + qi)),
                  ctx(k_rb), band(k_rb, -1), band(k_rb, 0), band(k_rb, 1),
                  ctx(v_rb), band(v_rb, -1), band(v_rb, 0), band(v_rb, 1)],
        out_specs=pl.BlockSpec((blk, q_rows), lambda b, qi, kvh, s: (b * nqb + qi, kvh)),
        scratch_shapes=[pltpu.VMEM((q_rows, blk), f32)],
    )
    return pl.pallas_call(
        kern,
        grid_spec=grid_spec,
        out_shape=jax.ShapeDtypeStruct((bsz * n_lat, ATT_HEADS * hd), bf16),
        compiler_params=_cparams("parallel", "parallel", "parallel"),
        name="window_attention",
    )(sink, *([qkv_t] * 9))


def _first_max(v, idx, n):
    m = jnp.max(v, axis=0, keepdims=True)
    first = jnp.min(jnp.where(v == m, idx, n), axis=0, keepdims=True)
    return m, first


def _route_kernel(h_ref, g_ref, mod_ref, modc_ref, rw_ref, rb_ref, wsg_ref, wsu_ref, wsd_ref,
                  m_ref, sh_ref, te_ref, wt_ref, pos_ref, cnt_ref, carry_s, *, n_ctx, tpb):
    i = pl.program_id(0)
    T = h_ref.shape[0]
    E, NG, GS = N_EXPERTS, N_EXPERT_GROUPS, GROUP_SIZE

    @pl.when(i == 0)
    def _():
        carry_s[...] = jnp.zeros_like(carry_s)

    x = h_ref[...]
    ms = jnp.mean(x * x, axis=-1, keepdims=True)
    y = x * lax.rsqrt(ms + EPS) * g_ref[...]
    scale = mod_ref[0, 4:5, :]
    shift = mod_ref[0, 3:4, :]
    if n_ctx:
        ctx_rows = jnp.clip(n_ctx - (i % tpb) * T, 0, T)
        is_ctx = lax.broadcasted_iota(i32, (T, 1), 0) < ctx_rows
        scale = jnp.where(is_ctx, modc_ref[0, 4:5, :], scale)
        shift = jnp.where(is_ctx, modc_ref[0, 3:4, :], shift)
    mt = y * (1.0 + scale) + shift
    m_ref[...] = _pack_bf16_pair(mt)
    mb = mt.astype(bf16)

    hg = jnp.dot(mb, wsg_ref[...], preferred_element_type=f32)
    hu = jnp.dot(mb, wsu_ref[...], preferred_element_type=f32)
    act = (jax.nn.silu(hg) * hu).astype(bf16)
    sh_ref[...] = jnp.dot(act, wsd_ref[...], preferred_element_type=f32)

    logits = lax.dot_general(rw_ref[...], mt, _NT, precision=HIGHEST,
                             preferred_element_type=f32)
    scores = jax.nn.sigmoid(logits)
    sel = scores + rb_ref[...]
    eidx = lax.broadcasted_iota(i32, (E, T), 0)
    midx = lax.broadcasted_iota(i32, (GS, T), 0)
    gidx = lax.broadcasted_iota(i32, (NG, T), 0)

    gs = jnp.zeros((NG, T), f32)
    for g in range(NG):
        blk = sel[g * GS:(g + 1) * GS, :]
        m1, f1 = _first_max(blk, midx, GS)
        m2 = jnp.max(jnp.where(midx == f1, -jnp.inf, blk), axis=0, keepdims=True)
        gs = jnp.where(gidx == g, m1 + m2, gs)
    gmask = jnp.zeros((NG, T), jnp.bool_)
    work = gs
    for _ in range(TOP_GROUPS):
        _, f = _first_max(work, gidx, NG)
        hit = gidx == f
        gmask = jnp.logical_or(gmask, hit)
        work = jnp.where(hit, -jnp.inf, work)
    cand = jnp.concatenate(
        [jnp.where(gmask[g:g + 1, :], sel[g * GS:(g + 1) * GS, :], -jnp.inf) for g in range(NG)],
        axis=0)
    hits = []
    chosen = jnp.zeros((E, T), jnp.bool_)
    work = cand
    for k in range(TOP_K):
        _, f = _first_max(work, eidx, E)
        hit = eidx == f
        hits.append(hit)
        chosen = jnp.logical_or(chosen, hit)
        work = jnp.where(hit, -jnp.inf, work)
        te_ref[k:k + 1, :] = f
    wts = [jnp.sum(jnp.where(h, scores, 0.0), axis=0, keepdims=True) for h in hits]
    tot = wts[0]
    for w in wts[1:]:
        tot = tot + w
    for k in range(TOP_K):
        wt_ref[k:k + 1, :] = wts[k] / (tot + 1e-20) * ROUTED_SCALE

    t0 = lax.broadcasted_iota(i32, (T, T), 0)
    t1 = lax.broadcasted_iota(i32, (T, T), 1)
    before = (t0 < t1).astype(bf16)
    chosen_f = chosen.astype(f32)
    cnt = jnp.dot(chosen_f.astype(bf16), before, preferred_element_type=f32) + carry_s[:, 0:1]
    for k in range(TOP_K):
        pos_ref[k:k + 1, :] = jnp.sum(jnp.where(hits[k], cnt, 0.0), axis=0,
                                      keepdims=True).astype(i32)
    new_carry = carry_s[...] + jnp.sum(chosen_f, axis=1, keepdims=True)
    carry_s[...] = new_carry
    cnt_ref[...] = new_carry.astype(i32)


def moe_route(h, g, mod3, rw_t, rbias, wsg, wsu, wsd, geom, lat_only):
    m, d = h.shape
    bsz, L, n_ctx = geom
    if lat_only:
        T = _pick_tile(L - n_ctx, (ROUTE_TILE_LAT, ROW_TILE))
        tpb = (L - n_ctx) // T
        kern = functools.partial(_route_kernel, n_ctx=0, tpb=tpb)
    else:
        T = _pick_tile(L, (ROUTE_TILE_JOINT, ROW_TILE))
        tpb = L // T
        kern = functools.partial(_route_kernel, n_ctx=n_ctx, tpb=tpb)
    E = N_EXPERTS
    hid = wsg.shape[1]
    const2 = lambda i: (0, 0)
    outs = pl.pallas_call(
        kern,
        grid=(m // T,),
        in_specs=[pl.BlockSpec((T, d), lambda i: (i, 0)),
                  pl.BlockSpec((1, d), const2),
                  pl.BlockSpec((1, 6, d), lambda i: (i // tpb, 0, 0)),
                  pl.BlockSpec((1, 6, d), lambda i: (bsz, 0, 0)),
                  pl.BlockSpec((E, d), const2),
                  pl.BlockSpec((E, 1), const2),
                  pl.BlockSpec((d, hid), const2),
                  pl.BlockSpec((d, hid), const2),
                  pl.BlockSpec((hid, d), const2)],
        out_specs=[pl.BlockSpec((T, d // 2), lambda i: (i, 0)),
                   pl.BlockSpec((T, d), lambda i: (i, 0)),
                   pl.BlockSpec((TOP_K, T), lambda i: (0, i)),
                   pl.BlockSpec((TOP_K, T), lambda i: (0, i)),
                   pl.BlockSpec((TOP_K, T), lambda i: (0, i)),
                   pl.BlockSpec((E, 128), const2)],
        out_shape=[jax.ShapeDtypeStruct((m, d // 2), jnp.uint32),
                   jax.ShapeDtypeStruct((m, d), f32),
                   jax.ShapeDtypeStruct((TOP_K, m), i32),
                   jax.ShapeDtypeStruct((TOP_K, m), f32),
                   jax.ShapeDtypeStruct((TOP_K, m), i32),
                   jax.ShapeDtypeStruct((E, 128), i32)],
        scratch_shapes=[pltpu.VMEM((E, 128), f32)],
        compiler_params=_cparams("arbitrary"),
        name="moe_route",
    )(h, g.reshape(1, d), mod3, mod3, rw_t, rbias.reshape(E, 1), wsg, wsu, wsd)
    return outs


GATHER_AHEAD = 2
GATHER_SLOTS = GATHER_AHEAD + 1


def _gmm_kernel(be_ref, nused_ref, tokc_ref, tokn_ref, tokf_ref, m_hbm, wg_ref, wu_ref,
                wd_ref, o_ref, xbuf, wg_s, wu_s, wd_s, sem):
    i = pl.program_id(0)
    nused = nused_ref[0]
    slot = i % GATHER_SLOTS
    far_slot = (i + GATHER_AHEAD) % GATHER_SLOTS

    def start_gather(tok_ref, dst_slot):
        for r in range(MOE_BLK):
            t = tok_ref[0, 0, r]
            pltpu.make_async_copy(m_hbm.at[pl.ds(t, 1), :],
                                  xbuf.at[dst_slot, pl.ds(r, 1), :],
                                  sem.at[dst_slot]).start()

    def wait_gather(dst_slot):
        pltpu.make_async_copy(m_hbm.at[pl.ds(0, MOE_BLK), :], xbuf.at[dst_slot],
                              sem.at[dst_slot]).wait()

    @pl.when(jnp.logical_and(i == 0, nused > 0))
    def _():
        start_gather(tokc_ref, 0)
        start_gather(tokn_ref, 1)

    prev = be_ref[jnp.maximum(i - 1, 0)]
    new_expert = jnp.logical_or(i == 0, be_ref[i] != prev)

    @pl.when(jnp.logical_and(i < nused, new_expert))
    def _():
        wg_s[...] = wg_ref[...].astype(bf16)
        wu_s[...] = wu_ref[...].astype(bf16)
        wd_s[...] = wd_ref[...].astype(bf16)

    @pl.when(i < nused)
    def _():
        wait_gather(slot)
        x_lo, x_hi = _unpack_bf16_pair(xbuf[slot])
        x_lo = x_lo.astype(bf16)
        x_hi = x_hi.astype(bf16)
        start_gather(tokf_ref, far_slot)
        kh = x_lo.shape[1]
        hg = (jnp.dot(x_lo, wg_s[:kh, :], preferred_element_type=f32)
              + jnp.dot(x_hi, wg_s[kh:, :], preferred_element_type=f32))
        hu = (jnp.dot(x_lo, wu_s[:kh, :], preferred_element_type=f32)
              + jnp.dot(x_hi, wu_s[kh:, :], preferred_element_type=f32))
        act = (jax.nn.silu(hg) * hu).astype(bf16)
        o_ref[...] = _pack_bf16_pair(jnp.dot(act, wd_s[...], preferred_element_type=f32))

    @pl.when(i >= nused)
    def _():
        o_ref[...] = jnp.zeros_like(o_ref)

    @pl.when(jnp.logical_and(jnp.logical_and(i >= nused, i < nused + GATHER_AHEAD), nused > 0))
    def _():
        wait_gather(slot)


def moe_experts(m_rows, slot_tok, block_expert, n_used, wg, wu, wd, layer):
    n_steps = block_expert.shape[0]
    d = wg.shape[2]
    dp = m_rows.shape[1]
    hid = wg.shape[3]
    tok3 = slot_tok.reshape(n_steps, 1, MOE_BLK)
    smem_blk = lambda imap: pl.BlockSpec((1, 1, MOE_BLK), imap, memory_space=pltpu.SMEM)
    grid_spec = pltpu.PrefetchScalarGridSpec(
        num_scalar_prefetch=2,
        grid=(n_steps,),
        in_specs=[smem_blk(lambda i, be, nu: (i, 0, 0)),
                  smem_blk(lambda i, be, nu: (jnp.minimum(i + 1, n_steps - 1), 0, 0)),
                  smem_blk(lambda i, be, nu: (jnp.minimum(i + GATHER_AHEAD, n_steps - 1), 0, 0)),
                  pl.BlockSpec(memory_space=pl.ANY),
                  pl.BlockSpec((None, None, d, hid), lambda i, be, nu: (layer, be[i], 0, 0)),
                  pl.BlockSpec((None, None, d, hid), lambda i, be, nu: (layer, be[i], 0, 0)),
                  pl.BlockSpec((None, None, hid, d), lambda i, be, nu: (layer, be[i], 0, 0))],
        out_specs=pl.BlockSpec((MOE_BLK, dp), lambda i, be, nu: (i, 0)),
        scratch_shapes=[pltpu.VMEM((GATHER_SLOTS, MOE_BLK, dp), jnp.uint32),
                        pltpu.VMEM((d, hid), bf16), pltpu.VMEM((d, hid), bf16),
                        pltpu.VMEM((hid, d), bf16),
                        pltpu.SemaphoreType.DMA((GATHER_SLOTS,))],
    )
    return pl.pallas_call(
        _gmm_kernel,
        grid_spec=grid_spec,
        out_shape=jax.ShapeDtypeStruct((n_steps * MOE_BLK, dp), jnp.uint32),
        compiler_params=_cparams("arbitrary"),
        name="moe_experts",
    )(block_expert, n_used, tok3, tok3, tok3, m_rows, wg, wu, wd)


def _comb_kernel(*refs, n_tiles, with_next):
    dc_ref, dn_ref, df_ref, ys_hbm, w_ref, sh_ref, h_ref, mod_ref = refs[:8]
    if with_next:
        gn_ref, modn_ref, o_ref, a_ref = refs[8:12]
        gbufs, sem = refs[12:12 + GATHER_SLOTS], refs[12 + GATHER_SLOTS]
    else:
        o_ref = refs[8]
        gbufs, sem = refs[9:9 + GATHER_SLOTS], refs[9 + GATHER_SLOTS]
    i = pl.program_id(0)
    T = COMB_TILE
    S = GATHER_SLOTS

    def start_gather(d_ref, p):
        for r in range(T):
            for k in range(TOP_K):
                t = d_ref[0, k, r]
                pltpu.make_async_copy(ys_hbm.at[pl.ds(t, 1), :],
                                      gbufs[p].at[k, pl.ds(r, 1), :],
                                      sem.at[p]).start(priority=k % 2)

    @pl.when(i == 0)
    def _():
        start_gather(dc_ref, 0)
        start_gather(dn_ref, 1)

    def step(p, issue_far):
        for k in range(TOP_K):
            pltpu.make_async_copy(ys_hbm.at[pl.ds(0, T), :], gbufs[p].at[k],
                                  sem.at[p]).wait()
        if issue_far:
            start_gather(df_ref, (p + GATHER_AHEAD) % S)
        w = w_ref[...]
        half = o_ref.shape[1] // 2
        acc_lo = jnp.zeros((T, half), f32)
        acc_hi = jnp.zeros((T, half), f32)
        for k in range(TOP_K):
            lo, hi = _unpack_bf16_pair(gbufs[p][k])
            acc_lo = acc_lo + lo * w[:, k:k + 1]
            acc_hi = acc_hi + hi * w[:, k:k + 1]
        gate = mod_ref[0, 5:6, :]
        o_lo = h_ref[:, :half] + gate[:, :half] * (acc_lo + sh_ref[:, :half])
        o_hi = h_ref[:, half:] + gate[:, half:] * (acc_hi + sh_ref[:, half:])
        o_ref[:, :half] = o_lo
        o_ref[:, half:] = o_hi
        if with_next:
            ssq = (jnp.sum(o_lo * o_lo, axis=-1, keepdims=True)
                   + jnp.sum(o_hi * o_hi, axis=-1, keepdims=True))
            inv = lax.rsqrt(ssq / (2 * half) + EPS)
            scale = 1.0 + modn_ref[0, 1:2, :]
            shift = modn_ref[0, 0:1, :]
            gn = gn_ref[...]
            a_ref[:, :half] = (o_lo * inv * gn[:, :half] * scale[:, :half]
                               + shift[:, :half]).astype(a_ref.dtype)
            a_ref[:, half:] = (o_hi * inv * gn[:, half:] * scale[:, half:]
                               + shift[:, half:]).astype(a_ref.dtype)

    first_tail = n_tiles - GATHER_AHEAD
    for p in range(S):
        @pl.when(jnp.logical_and(i % S == p, i < first_tail))
        def _(p=p):
            step(p, True)

    for tail in range(first_tail, n_tiles):
        @pl.when(i == tail)
        def _(tail=tail):
            step(tail % S, False)


def moe_combine(y_sorted, dest_t, wts_t, shared, h, mod3, geom, lat_only, next_norm=None):
    m, d = h.shape
    bsz, L, n_ctx = geom
    T = COMB_TILE
    n_tiles = m // T
    if lat_only:
        lpb = (L - n_ctx) // T
        mod_row = lambda i: i // lpb
    else:
        mod_row = _mod_row_map(geom, T)
    smem_blk = lambda imap: pl.BlockSpec((1, TOP_K, T), imap, memory_space=pltpu.SMEM)
    row_blk = pl.BlockSpec((T, d), lambda i: (i, 0))
    mod_blk = pl.BlockSpec((1, 6, d), lambda i: (mod_row(i), 0, 0))
    in_specs = [smem_blk(lambda i: (i, 0, 0)),
                smem_blk(lambda i: (jnp.minimum(i + 1, n_tiles - 1), 0, 0)),
                smem_blk(lambda i: (jnp.minimum(i + GATHER_AHEAD, n_tiles - 1), 0, 0)),
                pl.BlockSpec(memory_space=pl.ANY),
                pl.BlockSpec((T, TOP_K), lambda i: (i, 0)),
                row_blk, row_blk, mod_blk]
    args = [dest_t, dest_t, dest_t, y_sorted, wts_t, shared, h, mod3]
    out_specs, out_shape = row_blk, jax.ShapeDtypeStruct((m, d), f32)
    if next_norm is not None:
        in_specs += [pl.BlockSpec((1, d), lambda i: (0, 0)), mod_blk]
        args += [next_norm[0].reshape(1, d), next_norm[1]]
        out_specs = [row_blk, row_blk]
        out_shape = [out_shape, jax.ShapeDtypeStruct((m, d), bf16)]
    return pl.pallas_call(
        functools.partial(_comb_kernel, n_tiles=n_tiles, with_next=next_norm is not None),
        grid=(n_tiles,),
        in_specs=in_specs,
        out_specs=out_specs,
        out_shape=out_shape,
        scratch_shapes=[pltpu.VMEM((TOP_K, T, d // 2), jnp.uint32)] * GATHER_SLOTS
        + [pltpu.SemaphoreType.DMA((GATHER_SLOTS,))],
        compiler_params=_cparams("arbitrary"),
        name="moe_combine",
    )(*args)


TOK_SPLIT_LOG2 = 7


def _slot_kernel(dest_ref, hi_ref, lo_ref):
    i = pl.program_id(0)
    T = dest_ref.shape[1]
    nb = hi_ref.shape[0]

    @pl.when(i == 0)
    def _():
        hi_ref[...] = jnp.zeros_like(hi_ref)
        lo_ref[...] = jnp.zeros_like(lo_ref)

    d = dest_ref[...]
    tok = i * T + lax.broadcasted_iota(i32, (1, T), 1)
    tok_hi = (tok >> TOK_SPLIT_LOG2).astype(f32)
    tok_lo = (tok & ((1 << TOK_SPLIT_LOG2) - 1)).astype(f32)
    brow = lax.broadcasted_iota(i32, (nb, T), 0)
    orow = lax.broadcasted_iota(i32, (MOE_BLK, T), 0)
    a_hi, a_lo, b = [], [], []
    for k in range(TOP_K):
        dk = d[k:k + 1, :]
        hit = brow == (dk >> MOE_BLK_LOG2)
        a_hi.append(jnp.where(hit, tok_hi, 0.0).astype(bf16))
        a_lo.append(jnp.where(hit, tok_lo, 0.0).astype(bf16))
        b.append((orow == (dk & (MOE_BLK - 1))).astype(f32).astype(bf16))
    bm = jnp.concatenate(b, axis=1)
    hi_ref[...] += lax.dot_general(jnp.concatenate(a_hi, axis=1), bm, _NT,
                                   preferred_element_type=f32)
    lo_ref[...] += lax.dot_general(jnp.concatenate(a_lo, axis=1), bm, _NT,
                                   preferred_element_type=f32)


def slot_table(dest, n_steps):
    m = dest.shape[1]
    T = ROW_TILE
    nb = -(-n_steps // 8) * 8
    hi, lo = pl.pallas_call(
        _slot_kernel,
        grid=(m // T,),
        in_specs=[pl.BlockSpec((TOP_K, T), lambda i: (0, i))],
        out_specs=[pl.BlockSpec((nb, MOE_BLK), lambda i: (0, 0)),
                   pl.BlockSpec((nb, MOE_BLK), lambda i: (0, 0))],
        out_shape=[jax.ShapeDtypeStruct((nb, MOE_BLK), f32),
                   jax.ShapeDtypeStruct((nb, MOE_BLK), f32)],
        compiler_params=_cparams("arbitrary"),
        name="slot_table",
    )(dest)
    tab = hi.astype(i32) * (1 << TOK_SPLIT_LOG2) + lo.astype(i32)
    return tab[:n_steps]


def moe_layer(h, g2, mod3, layer, rw, rbias, wg, wu, wd, wsg, wsu, wsd, geom, lat_only,
              next_norm=None):
    m, d = h.shape
    E = N_EXPERTS
    mt, shared, top_e, wts, pos, cnt = moe_route(
        h, g2, mod3, rw.T, rbias, wsg.astype(bf16), wsu.astype(bf16), wsd.astype(bf16),
        geom, lat_only)
    counts = cnt[:, 0]
    padded = ((counts + MOE_BLK - 1) // MOE_BLK) * MOE_BLK
    pad_end = jnp.cumsum(padded)
    pad_start = pad_end - padded
    eids = jnp.arange(E, dtype=i32)[:, None, None]
    dest = pos + jnp.sum(jnp.where(top_e[None] == eids, pad_start[:, None, None], 0), axis=0)
    n_steps = (m * TOP_K) // MOE_BLK + E + GATHER_AHEAD
    slot_tok = slot_table(dest, n_steps)
    block_start = jnp.arange(n_steps, dtype=i32) * MOE_BLK
    block_expert = jnp.minimum(
        jnp.sum((pad_end[None, :] <= block_start[:, None]).astype(i32), axis=1), E - 1)
    n_used = (pad_end[-1:] // MOE_BLK).astype(i32)
    y_sorted = moe_experts(mt, slot_tok, block_expert, n_used, wg, wu, wd, layer)
    T = COMB_TILE
    dest_t = dest.reshape(TOP_K, m // T, T).transpose(1, 0, 2)
    return moe_combine(y_sorted, dest_t, wts.T, shared, h, mod3, geom, lat_only, next_norm)


def _rope_tables_t(n_ctx, n_lat):
    n_rows = n_lat // GRID_W
    rows = jnp.repeat(jnp.arange(n_rows, dtype=f32), GRID_W)
    cols = jnp.tile(jnp.arange(GRID_W, dtype=f32), n_rows)
    n_freq = ATT_HEADDIM // 4
    inv_freq = ROPE_BASE ** (-jnp.arange(n_freq, dtype=f32) / n_freq)
    ar = rows[None, :] * inv_freq[:, None]
    ac = cols[None, :] * inv_freq[:, None]
    cos_t = jnp.concatenate([jnp.cos(ar), jnp.cos(ar), jnp.cos(ac), jnp.cos(ac)], axis=0)
    sin_t = jnp.concatenate([-jnp.sin(ar), jnp.sin(ar), -jnp.sin(ac), jnp.sin(ac)], axis=0)
    cos_t = jnp.concatenate([jnp.ones((ATT_HEADDIM, n_ctx), f32), cos_t], axis=1)
    sin_t = jnp.concatenate([jnp.zeros((ATT_HEADDIM, n_ctx), f32), sin_t], axis=1)
    return cos_t, sin_t


def even_layer_mix(h, a, mod3, w_in, gm_ws, gm_bs, gm_ln_g, gm_ln_b, conv_w, conv_b,
                   a_log, dt_bias, d_skip, ssd_norm_g, w_out, geom):
    n_main = w_in.shape[1] - 2 * SSD_HEADS
    proj = matmul_nn(a, w_in.astype(bf16), f32, n=n_main)
    w_dt = w_in[:, n_main:].reshape(-1, 2, SSD_GROUPS, SSD_HPG).transpose(2, 1, 3, 0)
    dt_t = dt_project(a, w_dt.reshape(2 * SSD_HEADS, -1))
    bexp = jnp.repeat(gm_bs.T, A_CHUNK, axis=1)
    g_out = gmlp_mix(proj, gm_ws.astype(bf16), bexp, gm_ln_g, gm_ln_b)
    y_out = ssd_mix(proj, dt_t, conv_w, conv_b, a_log, dt_bias, d_skip, ssd_norm_g, geom)
    aw = gm_ws.shape[0] * gm_ws.shape[1]
    w_o = w_out.astype(bf16)
    return out_project_joint([g_out, y_out], [w_o[:aw], w_o[aw:]], h, mod3, geom)


def odd_layer_mix_lat(h, a, mod3, w_q, w_kv, q_norm_g, k_norm_g, sink, w_o, geom):
    bsz, L, n_ctx = geom
    w_t = jnp.concatenate([w_q, w_kv], axis=1).astype(bf16)
    cos_t, sin_t = _rope_tables_t(n_ctx, L - n_ctx)
    qkv_t = qkv_project_t(a, w_t, q_norm_g, k_norm_g, cos_t, sin_t, geom)
    att = window_attention(qkv_t, sink, geom)
    return out_project_lat(att, w_o.astype(bf16), h, mod3, geom)


def kernel(x, c, ctx, c_ctx, mod_w, mod_b, norm1_g, norm2_g, ev_w_in, ev_gm_ws, ev_gm_bs, ev_gm_ln_g, ev_gm_ln_b, ev_conv_w, ev_conv_b, ev_a_log, ev_dt_bias, ev_d_skip, ev_ssd_norm_g, ev_w_out, od_w_q, od_w_kv, od_q_norm_g, od_k_norm_g, od_sink, od_w_o, moe_router_w, moe_router_bias, moe_w_gate, moe_w_up, moe_w_down, moe_ws_gate, moe_ws_up, moe_ws_down):
    bsz, n_lat, d = x.shape
    n_ctx = ctx.shape[1]
    L = n_ctx + n_lat
    geom = (bsz, L, n_ctx)
    depth = mod_w.shape[0]
    assert depth == 2 and n_ctx == ROW_TILE and n_lat % ROW_TILE == 0 and bsz < 16

    c_all = jnp.zeros((16, d), f32).at[:bsz].set(c).at[bsz].set(c_ctx)
    mod3 = modulation(c_all, mod_w, mod_b, 0).reshape(16, 6, d)
    mod3_1 = modulation(c_all, mod_w, mod_b, 1).reshape(16, 6, d)

    h, a = join_norm_modulate(ctx.reshape(bsz * n_ctx, d), x.reshape(bsz * n_lat, d),
                              norm1_g[0], mod3, geom)
    h = even_layer_mix(h, a, mod3, ev_w_in[0], ev_gm_ws[0], ev_gm_bs[0], ev_gm_ln_g[0],
                       ev_gm_ln_b[0], ev_conv_w[0], ev_conv_b[0], ev_a_log[0], ev_dt_bias[0],
                       ev_d_skip[0], ev_ssd_norm_g[0], ev_w_out[0], geom)
    h, a = moe_layer(h, norm2_g[0], mod3, 0, moe_router_w[0], moe_router_bias[0], moe_w_gate,
                     moe_w_up, moe_w_down, moe_ws_gate[0], moe_ws_up[0], moe_ws_down[0],
                     geom, lat_only=False, next_norm=(norm1_g[1], mod3_1))

    h_lat = odd_layer_mix_lat(h, a, mod3_1, od_w_q[0], od_w_kv[0], od_q_norm_g[0],
                              od_k_norm_g[0], od_sink[0], od_w_o[0], geom)
    h_lat = moe_layer(h_lat, norm2_g[1], mod3_1, 1, moe_router_w[1], moe_router_bias[1],
                      moe_w_gate, moe_w_up, moe_w_down, moe_ws_gate[1], moe_ws_up[1],
                      moe_ws_down[1], geom, lat_only=True)
    return h_lat.reshape(bsz, n_lat, d)
```
